```python
import jax, jax.numpy as jnp
from jax import lax
import numpy as np

D_MODEL = 1024
BATCH = 2
SEQ = 8192
DEPTH = 2

HEAD_DIM = 64
MIX_W = D_MODEL // 2
N_BRANCH = 3
POOL_WINDOWS = (2, 4, 8, 16)
POOL_GROUPS = len(POOL_WINDOWS)
POOL_GC = MIX_W // POOL_GROUPS
NSA_HEADS = MIX_W // HEAD_DIM
NSA_KV_GROUPS = 2
NSA_HPG = NSA_HEADS // NSA_KV_GROUPS
CMP_LEN = 32
CMP_STRIDE = 16
CMP_HID = 4 * HEAD_DIM
SEL_LEN = 64
N_SEL = 16
WINDOW = 512
FOX_HEADS = MIX_W // HEAD_DIM
Q_BLOCK = 128
N_EXPERTS = 32
TOP_K = 4
D_FF = D_MODEL
SWIGLU_LIMIT = 7.0
SWIGLU_ALPHA = 1.702
MOE_BLOCK = 256
LN_EPS = 1e-5
NEG_INF = -1e30
SCALE = HEAD_DIM ** -0.5
ALPHA = (2 * DEPTH) ** 0.25
BETA = (8 * DEPTH) ** -0.25
FORGET_BIAS_LO = 1.0
FORGET_BIAS_HI = 6.0

SPLIT_SIZES = (
    MIX_W,
    NSA_HEADS * HEAD_DIM,
    N_BRANCH * 2 * NSA_KV_GROUPS * HEAD_DIM,
    N_BRANCH * NSA_HEADS,
    3 * FOX_HEADS * HEAD_DIM,
    FOX_HEADS,
    N_BRANCH * D_MODEL,
)
SPLIT_IDX = tuple(sum(SPLIT_SIZES[:i + 1]) for i in range(len(SPLIT_SIZES) - 1))
IN_WIDTH = sum(SPLIT_SIZES)
FOX_F_OFFSET = SPLIT_IDX[4]

kernel_name = 'hybrid_pool_nsa_fox_moe_deepnorm'


def layer_norm(x, g, b):
    xf = x.astype(jnp.float32)
    mu = jnp.mean(xf, axis=-1, keepdims=True)
    var = jnp.mean(jnp.square(xf - mu), axis=-1, keepdims=True)
    return ((xf - mu) * lax.rsqrt(var + LN_EPS) * g + b).astype(x.dtype)


def masked_softmax(s, mask):
    s = jnp.where(mask, s.astype(jnp.float32), NEG_INF)
    return jnp.where(mask, jax.nn.softmax(s, axis=-1), 0.0)


def pool_mixer(u, w, b, scale):
    B, T, _ = u.shape
    uf = u.astype(jnp.float32)
    cs = jnp.concatenate([jnp.zeros((B, 1, MIX_W), jnp.float32), jnp.cumsum(uf, axis=1)], axis=1)
    t_idx = jnp.arange(T)
    outs = []
    for gi, win in enumerate(POOL_WINDOWS):
        sl = slice(gi * POOL_GC, (gi + 1) * POOL_GC)
        c = cs[:, :, sl]
        lag = jnp.pad(c, ((0, 0), (win, 0), (0, 0)))[:, :T + 1]
        wsum = c[:, 1:] - lag[:, 1:]
        cnt = jnp.minimum(t_idx + 1, win).astype(jnp.float32)
        outs.append(wsum / cnt[:, None] - uf[:, :, sl])
    d = jnp.stack(outs, axis=2)
    y = jnp.einsum('btgc,gcd->btgd', d, w.astype(jnp.float32)) + b
    return (y.reshape(B, T, MIX_W) * scale).astype(u.dtype)


def compress_blocks(k, pos, w1, b1, w2, b2):
    B, T, G, HD = k.shape
    n_chunk = T // CMP_STRIDE
    r = CMP_LEN // CMP_STRIDE
    nc = n_chunk - r + 1
    ch = k.reshape(B, n_chunk, CMP_STRIDE, G, HD)
    blocks = jnp.concatenate([ch[:, j:j + nc] for j in range(r)], axis=2)
    blocks = blocks + pos[:, None, :]
    flat = blocks.transpose(0, 1, 3, 2, 4).reshape(B, nc, G, CMP_LEN * HD)
    h = jax.nn.gelu(flat @ w1 + b1)
    return h @ w2 + b2


def nsa_attention(q, kv, g, cmp_pos, cmp_w1, cmp_b1, cmp_w2, cmp_b2):
    B, T, _ = q.shape
    G, HPG, HD = NSA_KV_GROUPS, NSA_HPG, HEAD_DIM
    q = q.reshape(B, T, G, HPG, HD)
    kv = kv.reshape(B, T, N_BRANCH, 2, G, HD)
    gates = jax.nn.sigmoid(g.astype(jnp.float32)).astype(q.dtype).reshape(B, T, G, HPG, N_BRANCH)
    kc = compress_blocks(kv[:, :, 0, 0], cmp_pos[0], cmp_w1[0], cmp_b1[0], cmp_w2[0], cmp_b2[0])
    vc = compress_blocks(kv[:, :, 0, 1], cmp_pos[1], cmp_w1[1], cmp_b1[1], cmp_w2[1], cmp_b2[1])
    NC = kc.shape[1]
    NS = T // SEL_LEN
    n_sel = min(N_SEL, NS)
    k_sel = kv[:, :, 1, 0].reshape(B, NS, SEL_LEN, G, HD).transpose(0, 3, 1, 2, 4)
    v_sel = kv[:, :, 1, 1].reshape(B, NS, SEL_LEN, G, HD).transpose(0, 3, 1, 2, 4)
    pad = ((0, 0), (WINDOW, 0), (0, 0), (0, 0))
    k_win = jnp.pad(kv[:, :, 2, 0], pad)
    v_win = jnp.pad(kv[:, :, 2, 1], pad)
    cmp_start = jnp.arange(NC) * CMP_STRIDE
    cmp_end = cmp_start + CMP_LEN - 1
    sel_blk = jnp.arange(NS)
    sel_start = sel_blk * SEL_LEN
    overlap = ((cmp_start[:, None] < sel_start[None, :] + SEL_LEN)
               & (cmp_start[:, None] + CMP_LEN > sel_start[None, :])).astype(jnp.float32)
    bi = jnp.arange(B)[:, None, None, None]
    gi = jnp.arange(G)[None, :, None, None]
    in_blk = jnp.arange(SEL_LEN)
    win_off = jnp.arange(WINDOW + Q_BLOCK) - WINDOW

    def block(i):
        t0 = i * Q_BLOCK
        tq = t0 + jnp.arange(Q_BLOCK)
        qb = lax.dynamic_slice_in_dim(q, t0, Q_BLOCK, axis=1)
        gb = lax.dynamic_slice_in_dim(gates, t0, Q_BLOCK, axis=1)
        s = jnp.einsum('bqghd,bcgd->bgqhc', qb, kc) * SCALE
        m = (cmp_end[None, :] <= tq[:, None])[None, None, :, None, :]
        p_cmp = masked_softmax(s, m)
        o_cmp = jnp.einsum('bgqhc,bcgd->bqghd', p_cmp.astype(vc.dtype), vc)
        imp = jnp.einsum('bgqhc,cs->bgqs', p_cmp, overlap)
        cur = tq // SEL_LEN
        forced = (sel_blk[None, :] == 0) | (sel_blk[None, :] == cur[:, None]) | (sel_blk[None, :] == cur[:, None] - 1)
        future = sel_start[None, :] > tq[:, None]
        score = jnp.where(future, -1.0, jnp.where(forced, 1e6, imp))
        _, idx = lax.top_k(score, n_sel)
        ks = k_sel[bi, gi, idx].reshape(B, G, Q_BLOCK, n_sel * SEL_LEN, HD)
        vs = v_sel[bi, gi, idx].reshape(B, G, Q_BLOCK, n_sel * SEL_LEN, HD)
        kpos = (idx[..., None] * SEL_LEN + in_blk).reshape(B, G, Q_BLOCK, n_sel * SEL_LEN)
        s = jnp.einsum('bqghd,bgqkd->bgqhk', qb, ks) * SCALE
        m = (kpos <= tq[None, None, :, None])[:, :, :, None, :]
        p = masked_softmax(s, m)
        o_slc = jnp.einsum('bgqhk,bgqkd->bqghd', p.astype(vs.dtype), vs)
        kw = lax.dynamic_slice_in_dim(k_win, t0, WINDOW + Q_BLOCK, axis=1)
        vw = lax.dynamic_slice_in_dim(v_win, t0, WINDOW + Q_BLOCK, axis=1)
        wpos = t0 + win_off
        dist = tq[:, None] - wpos[None, :]
        m = ((dist >= 0) & (dist < WINDOW) & (wpos[None, :] >= 0))[None, None, :, None, :]
        s = jnp.einsum('bqghd,bkgd->bgqhk', qb, kw) * SCALE
        p = masked_softmax(s, m)
        o_win = jnp.einsum('bgqhk,bkgd->bqghd', p.astype(vw.dtype), vw)
        return gb[..., 0:1] * o_cmp + gb[..., 1:2] * o_slc + gb[..., 2:3] * o_win

    o = lax.map(block, jnp.arange(T // Q_BLOCK))
    return o.transpose(1, 0, 2, 3, 4, 5).reshape(B, T, NSA_HEADS * HD)


def forgetting_attention(qkv, f_logit):
    B, T, _ = qkv.shape
    qkv = qkv.reshape(B, T, 3, FOX_HEADS, HEAD_DIM)
    q, k, v = qkv[:, :, 0], qkv[:, :, 1], qkv[:, :, 2]
    log_f = jax.nn.log_sigmoid(f_logit.astype(jnp.float32))
    cum = jnp.cumsum(log_f, axis=1).transpose(0, 2, 1)
    kpos = jnp.arange(T)

    def block(i):
        t0 = i * Q_BLOCK
        tq = t0 + jnp.arange(Q_BLOCK)
        qb = lax.dynamic_slice_in_dim(q, t0, Q_BLOCK, axis=1)
        cq = lax.dynamic_slice_in_dim(cum, t0, Q_BLOCK, axis=2)
        s = (jnp.einsum('bqhd,bkhd->bhqk', qb, k).astype(jnp.float32) * SCALE
             + cq[..., None] - cum[:, :, None, :])
        m = (kpos[None, :] <= tq[:, None])[None, None]
        p = masked_softmax(s, m)
        return jnp.einsum('bhqk,bkhd->bqhd', p.astype(v.dtype), v)

    o = lax.map(block, jnp.arange(T // Q_BLOCK))
    return o.transpose(1, 0, 2, 3, 4).reshape(B, T, FOX_HEADS * HEAD_DIM)


def hybrid_mixer(x, w_in, b_in, pool_w, pool_b, pool_scale, cmp_pos, cmp_w1, cmp_b1, cmp_w2, cmp_b2, w_up, w_o):
    B, T, D = x.shape
    z = x @ w_in + b_in
    u_pool, q_nsa, kv_nsa, g_nsa, qkv_fox, f_fox, g_mrg = jnp.split(z, list(SPLIT_IDX), axis=-1)
    y_pool = pool_mixer(u_pool, pool_w, pool_b, pool_scale)
    y_nsa = nsa_attention(q_nsa, kv_nsa, g_nsa, cmp_pos, cmp_w1, cmp_b1, cmp_w2, cmp_b2)
    y_fox = forgetting_attention(qkv_fox, f_fox)
    ys = jnp.stack([y_pool, y_nsa, y_fox], axis=2)
    up = jnp.einsum('btnc,ncd->btnd', ys, w_up)
    gate = jax.nn.sigmoid(g_mrg.astype(jnp.float32)).astype(x.dtype).reshape(B, T, N_BRANCH, D)
    return jnp.sum(gate * up, axis=2) @ w_o


def moe_ffn(x, router_w, router_b, w1, b1, w2, b2):
    B, T, D = x.shape
    N = B * T
    NK = N * TOP_K
    xf = x.reshape(N, D)
    logits = (xf @ router_w + router_b).astype(jnp.float32)
    top_v, top_i = lax.top_k(logits, TOP_K)
    gates = jax.nn.softmax(top_v, axis=-1)
    e_flat = top_i.reshape(NK)
    g_flat = gates.reshape(NK)
    tok = jnp.arange(NK) // TOP_K
    order = jnp.argsort(e_flat)
    e_sorted = e_flat[order]
    counts = jnp.bincount(e_flat, length=N_EXPERTS)
    start = jnp.cumsum(counts) - counts
    padded = (counts + MOE_BLOCK - 1) // MOE_BLOCK * MOE_BLOCK
    pend = jnp.cumsum(padded)
    pstart = pend - padded
    dest = pstart[e_sorted] + (jnp.arange(NK) - start[e_sorted])
    P = ((NK + MOE_BLOCK - 1) // MOE_BLOCK + N_EXPERTS) * MOE_BLOCK
    n_blk = P // MOE_BLOCK
    buf_tok = jnp.full((P,), N, jnp.int32).at[dest].set(tok[order])
    buf_gate = jnp.zeros((P,), jnp.float32).at[dest].set(g_flat[order])
    blk_e = jnp.minimum(jnp.sum(jnp.arange(n_blk)[:, None] * MOE_BLOCK >= pend[None, :], axis=1), N_EXPERTS - 1)
    xpad = jnp.concatenate([xf, jnp.zeros((1, D), xf.dtype)], axis=0)
    xb = xpad[buf_tok].reshape(n_blk, MOE_BLOCK, D)

    def expert_block(args):
        xblk, e = args
        h = xblk @ w1[e] + b1[e]
        gate, upv = jnp.split(h, 2, axis=-1)
        gate = jnp.minimum(gate, SWIGLU_LIMIT)
        upv = jnp.clip(upv, -SWIGLU_LIMIT, SWIGLU_LIMIT)
        act = (upv + 1.0) * (gate * jax.nn.sigmoid(SWIGLU_ALPHA * gate))
        return act @ w2[e] + b2[e]

    yb = lax.map(expert_block, (xb, blk_e)).reshape(P, D)
    y = jax.ops.segment_sum(yb * buf_gate[:, None].astype(yb.dtype), buf_tok, num_segments=N + 1)[:N]
    return y.reshape(B, T, D)


def setup_inputs(seed: int = 0) -> dict:
    key = jax.random.key(seed)
    ks = jax.random.split(key, 24)
    L, D = DEPTH, D_MODEL

    def nrm(k, shape, scale):
        return jax.random.normal(k, shape, jnp.float32) * scale

    b_in = nrm(ks[2], (L, IN_WIDTH), 0.02)
    b_in = b_in.at[:, FOX_F_OFFSET:FOX_F_OFFSET + FOX_HEADS].add(
        jnp.linspace(FORGET_BIAS_LO, FORGET_BIAS_HI, FOX_HEADS))
    return {
        'x': nrm(ks[0], (BATCH, SEQ, D), 1.0),
        'w_in': nrm(ks[1], (L, D, IN_WIDTH), D ** -0.5),
        'b_in': b_in,
        'pool_w': nrm(ks[3], (L, POOL_GROUPS, POOL_GC, POOL_GC), POOL_GC ** -0.5),
        'pool_b': nrm(ks[4], (L, POOL_GROUPS, POOL_GC), 0.02),
        'pool_scale': 1.0 + nrm(ks[5], (L, MIX_W), 0.02),
        'cmp_pos': nrm(ks[6], (L, 2, CMP_LEN, HEAD_DIM), 0.02),
        'cmp_w1': nrm(ks[7], (L, 2, CMP_LEN * HEAD_DIM, CMP_HID), (CMP_LEN * HEAD_DIM) ** -0.5),
        'cmp_b1': nrm(ks[8], (L, 2, CMP_HID), 0.02),
        'cmp_w2': nrm(ks[9], (L, 2, CMP_HID, HEAD_DIM), CMP_HID ** -0.5),
        'cmp_b2': nrm(ks[10], (L, 2, HEAD_DIM), 0.02),
        'w_up': nrm(ks[11], (L, N_BRANCH, MIX_W, D), MIX_W ** -0.5),
        'w_o': nrm(ks[12], (L, D, D), BETA * D ** -0.5),
        'ln1_g': 1.0 + nrm(ks[13], (L, D), 0.02),
        'ln1_b': nrm(ks[14], (L, D), 0.02),
        'router_w': nrm(ks[15], (L, D, N_EXPERTS), D ** -0.5),
        'router_b': nrm(ks[16], (L, N_EXPERTS), 0.01),
        'moe_w1': nrm(ks[17], (L, N_EXPERTS, D, 2 * D_FF), D ** -0.5),
        'moe_b1': nrm(ks[18], (L, N_EXPERTS, 2 * D_FF), 0.02),
        'moe_w2': nrm(ks[19], (L, N_EXPERTS, D_FF, D), BETA * D_FF ** -0.5),
        'moe_b2': nrm(ks[20], (L, N_EXPERTS, D), 0.02),
        'ln2_g': 1.0 + nrm(ks[21], (L, D), 0.02),
        'ln2_b': nrm(ks[22], (L, D), 0.02),
    }


def reference(x, w_in, b_in, pool_w, pool_b, pool_scale, cmp_pos, cmp_w1, cmp_b1, cmp_w2, cmp_b2,
              w_up, w_o, ln1_g, ln1_b, router_w, router_b, moe_w1, moe_b1, moe_w2, moe_b2, ln2_g, ln2_b):
    for l in range(DEPTH):
        h = hybrid_mixer(x, w_in[l], b_in[l], pool_w[l], pool_b[l], pool_scale[l], cmp_pos[l],
                         cmp_w1[l], cmp_b1[l], cmp_w2[l], cmp_b2[l], w_up[l], w_o[l])
        x = layer_norm(ALPHA * x + h, ln1_g[l], ln1_b[l])
        h = moe_ffn(x, router_w[l], router_b[l], moe_w1[l], moe_b1[l], moe_w2[l], moe_b2[l])
        x = layer_norm(ALPHA * x + h, ln2_g[l], ln2_b[l])
    return x
```

```python
import functools

import numpy as np
import jax
import jax.numpy as jnp
from jax import lax
from jax.experimental import pallas as pl
from jax.experimental.pallas import tpu as pltpu

F32 = jnp.float32
BF16 = jnp.bfloat16

HEAD_DIM = 64
POOL_WINDOWS = (2, 4, 8, 16)
POOL_GC = 128
POOL_HALO = 16
NSA_HEADS = 8
NSA_KV_GROUPS = 2
NSA_HPG = NSA_HEADS // NSA_KV_GROUPS
N_BRANCH = 3
CMP_LEN = 32
CMP_STRIDE = 16
SEL_LEN = 64
N_SEL = 16
WINDOW = 512
FOX_HEADS = 8
Q_BLOCK = 128
N_EXPERTS = 32
TOP_K = 4
SWIGLU_LIMIT = 7.0
SWIGLU_ALPHA = 1.702
MOE_BLOCK = 256
LN_EPS = 1e-5
NEG_INF = -1e30
SCALE = HEAD_DIM ** -0.5
MASK_BIG = 2.0 ** 100
LANES = 128
VMEM_LIMIT = 48 * 1024 * 1024

_NT = (((1,), (1,)), ((), ()))


def _cparams(sem):
    return pltpu.CompilerParams(dimension_semantics=sem, vmem_limit_bytes=VMEM_LIMIT)


def _split3(x):
    hi = x.astype(BF16)
    r1 = x - hi.astype(F32)
    mid = r1.astype(BF16)
    lo = (r1 - mid.astype(F32)).astype(BF16)
    return hi, mid, lo


def _layer_norm(r, g, b):
    mu = jnp.mean(r, axis=-1, keepdims=True)
    var = jnp.mean(jnp.square(r - mu), axis=-1, keepdims=True)
    return (r - mu) * lax.rsqrt(var + LN_EPS) * g + b


def _mm_kernel(x_ref, w_ref, b_ref, o_ref, *, act):
    acc = jnp.dot(x_ref[...].astype(BF16), w_ref[...], preferred_element_type=F32) + b_ref[...]
    if act == "sigmoid":
        acc = jax.nn.sigmoid(acc)
    o_ref[...] = acc.astype(o_ref.dtype)


def _matmul_bias(x, w, b, out_dtype, act=None, tm=1024, tn=512):
    m, k = x.shape
    n = w.shape[1]
    tn = min(tn, n)
    assert m % tm == 0 and n % tn == 0
    return pl.pallas_call(
        functools.partial(_mm_kernel, act=act),
        out_shape=jax.ShapeDtypeStruct((m, n), out_dtype),
        grid=(m // tm, n // tn),
        in_specs=[
            pl.BlockSpec((tm, k), lambda i, j: (i, 0)),
            pl.BlockSpec((k, tn), lambda i, j: (0, j)),
            pl.BlockSpec((1, tn), lambda i, j: (0, j)),
        ],
        out_specs=pl.BlockSpec((tm, tn), lambda i, j: (i, j)),
        compiler_params=_cparams(("parallel", "arbitrary")),
        name="in_proj",
    )(x, w, b.reshape(1, n))


def _pool_kernel(prev_ref, cur_ref, w_ref, b_ref, sc_ref, o_ref, ext_ref, *, tile):
    i = pl.program_id(1)
    ext_ref[0:POOL_HALO, :] = jnp.where(i > 0, prev_ref[...], 0.0)
    ext_ref[POOL_HALO:POOL_HALO + tile, :] = cur_ref[...]
    t_idx = i * tile + lax.broadcasted_iota(jnp.int32, (tile, 1), 0)
    for gi, win in enumerate(POOL_WINDOWS):
        cols = slice(gi * POOL_GC, (gi + 1) * POOL_GC)
        u = ext_ref[POOL_HALO:POOL_HALO + tile, cols]
        wsum = u
        for j in range(1, win):
            wsum = wsum + ext_ref[POOL_HALO - j:POOL_HALO - j + tile, cols]
        cnt = jnp.minimum(t_idx + 1, win).astype(F32)
        d = wsum / cnt - u
        y = jnp.dot(d.astype(BF16), w_ref[gi], preferred_element_type=F32) + b_ref[gi]
        o_ref[:, cols] = (y * sc_ref[:, cols]).astype(o_ref.dtype)


def _pool_mixer(u, w, b, scale, tile=512):
    bsz, t, c = u.shape
    return pl.pallas_call(
        functools.partial(_pool_kernel, tile=tile),
        out_shape=jax.ShapeDtypeStruct((bsz, t, c), BF16),
        grid=(bsz, t // tile),
        in_specs=[
            pl.BlockSpec((None, POOL_HALO, c),
                         lambda bb, i: (bb, jnp.maximum(i * (tile // POOL_HALO) - 1, 0), 0)),
            pl.BlockSpec((None, tile, c), lambda bb, i: (bb, i, 0)),
            pl.BlockSpec((len(POOL_WINDOWS), POOL_GC, POOL_GC), lambda bb, i: (0, 0, 0)),
            pl.BlockSpec((len(POOL_WINDOWS), 1, POOL_GC), lambda bb, i: (0, 0, 0)),
            pl.BlockSpec((1, c), lambda bb, i: (0, 0)),
        ],
        out_specs=pl.BlockSpec((None, tile, c), lambda bb, i: (bb, i, 0)),
        scratch_shapes=[pltpu.VMEM((POOL_HALO + tile, c), F32)],
        compiler_params=_cparams(("parallel", "arbitrary")),
        name="pool_mixer",
    )(u, u, w.astype(BF16), b.reshape(len(POOL_WINDOWS), 1, POOL_GC), scale.reshape(1, c))


def _compress_kernel(a_ref, pos_ref, w1_ref, b1_ref, w2_ref, b2_ref, o_ref):
    half = CMP_STRIDE * HEAD_DIM
    a = a_ref[...].astype(F32)
    n_chunk = a.shape[0]
    top = (a + pos_ref[:, 0:half]).astype(BF16)
    bot = (a + pos_ref[:, half:2 * half]).astype(BF16)
    p1 = jnp.dot(top, w1_ref[0:half, :], preferred_element_type=F32)
    p2 = jnp.dot(bot, w1_ref[half:2 * half, :], preferred_element_type=F32)
    h = p1 + pltpu.roll(p2, n_chunk - 1, axis=0) + b1_ref[...]
    h = jax.nn.gelu(h)
    o = jnp.dot(h.astype(BF16), w2_ref[...], preferred_element_type=F32) + b2_ref[...]
    o_ref[...] = o.astype(o_ref.dtype)


def _compress(a, pos, w1, b1, w2, b2):
    _, bg, n_chunk, width = a.shape
    hid = w1.shape[-1]
    return pl.pallas_call(
        _compress_kernel,
        out_shape=jax.ShapeDtypeStruct((2, bg, n_chunk, HEAD_DIM), BF16),
        grid=(2, bg),
        in_specs=[
            pl.BlockSpec((None, None, n_chunk, width), lambda s, j: (s, j, 0, 0)),
            pl.BlockSpec((None, 1, 2 * width), lambda s, j: (s, 0, 0)),
            pl.BlockSpec((None, 2 * width, hid), lambda s, j: (s, 0, 0)),
            pl.BlockSpec((None, 1, hid), lambda s, j: (s, 0, 0)),
            pl.BlockSpec((None, hid, HEAD_DIM), lambda s, j: (s, 0, 0)),
            pl.BlockSpec((None, 1, HEAD_DIM), lambda s, j: (s, 0, 0)),
        ],
        out_specs=pl.BlockSpec((None, None, n_chunk, HEAD_DIM), lambda s, j: (s, j, 0, 0)),
        compiler_params=_cparams(("arbitrary", "arbitrary")),
        name="nsa_compress",
    )(a, pos.reshape(2, 1, 2 * width), w1.astype(BF16), b1.reshape(2, 1, hid),
      w2.astype(BF16), b2.reshape(2, 1, HEAD_DIM))


def _nsa_cmp_kernel(q_ref, kc_ref, vc_ref, ovt_ref, ocmp_ref, qaug_ref):
    rows = NSA_HPG * Q_BLOCK
    t0 = pl.program_id(2) * Q_BLOCK
    q = q_ref[...].reshape(rows, HEAD_DIM)
    kc = kc_ref[...]
    n_c = kc.shape[0]
    s = lax.dot_general(q, kc, _NT, preferred_element_type=F32)
    tq = t0 + (lax.broadcasted_iota(jnp.int32, (rows, n_c), 0) & (Q_BLOCK - 1))
    c_end = lax.broadcasted_iota(jnp.int32, (rows, n_c), 1) * CMP_STRIDE + (CMP_LEN - 1)
    mask = c_end <= tq
    sm = jnp.where(mask, s, NEG_INF)
    m = jnp.max(sm, axis=-1, keepdims=True)
    e = jnp.where(mask, jnp.exp(sm - m), 0.0)
    l = jnp.sum(e, axis=-1, keepdims=True)
    p = e / jnp.where(l > 0.0, l, 1.0)
    o = jnp.dot(p.astype(BF16), vc_ref[...], preferred_element_type=F32)
    ocmp_ref[...] = o.reshape(NSA_HPG, Q_BLOCK, HEAD_DIM).astype(ocmp_ref.dtype)

    psum = p[0:Q_BLOCK]
    for h in range(1, NSA_HPG):
        psum = psum + p[h * Q_BLOCK:(h + 1) * Q_BLOCK]
    hi, mid, lo = _split3(psum)
    ovt = ovt_ref[...]
    imp_t = (lax.dot_general(ovt, hi, _NT, preferred_element_type=F32)
             + lax.dot_general(ovt, mid, _NT, preferred_element_type=F32)
             + lax.dot_general(ovt, lo, _NT, preferred_element_type=F32))
    n_s = imp_t.shape[0]
    blk = lax.broadcasted_iota(jnp.int32, (n_s, Q_BLOCK), 0)
    tqq = t0 + lax.broadcasted_iota(jnp.int32, (n_s, Q_BLOCK), 1)
    future = blk * SEL_LEN > tqq
    cur = lax.shift_right_logical(tqq, 6)
    forced = (blk == 0) | (blk == cur) | (blk == cur - 1)
    score = jnp.where(future, -1.0, jnp.where(forced, 1e6, imp_t))
    sel = jnp.zeros((n_s, Q_BLOCK), F32)
    for _ in range(min(N_SEL, n_s)):
        mx = jnp.max(score, axis=0, keepdims=True)
        first = jnp.min(jnp.where(score == mx, blk, n_s), axis=0, keepdims=True)
        pick = blk == first
        sel = jnp.where(pick, 1.0, sel)
        score = jnp.where(pick, -2.0, score)
    sel = jnp.where(future, 0.0, sel)
    bias = ((sel.T - 1.0) * MASK_BIG).astype(BF16)
    for h in range(NSA_HPG):
        qaug_ref[h * Q_BLOCK:(h + 1) * Q_BLOCK, 0:n_s] = bias
    qaug_ref[:, n_s:n_s + HEAD_DIM] = q


def _nsa_cmp(q, kc, vc, ovt):
    bsz, g, hpg, t, hd = q.shape
    n_c = kc.shape[2]
    n_s = ovt.shape[0]
    nq = t // Q_BLOCK
    rows = hpg * Q_BLOCK
    return pl.pallas_call(
        _nsa_cmp_kernel,
        out_shape=(jax.ShapeDtypeStruct((bsz, g, hpg, t, hd), BF16),
                   jax.ShapeDtypeStruct((bsz, g, nq, rows, n_s + hd), BF16)),
        grid=(bsz, g, nq),
        in_specs=[
            pl.BlockSpec((None, None, hpg, Q_BLOCK, hd), lambda b, gg, i: (b, gg, 0, i, 0)),
            pl.BlockSpec((None, None, n_c, hd), lambda b, gg, i: (b, gg, 0, 0)),
            pl.BlockSpec((None, None, n_c, hd), lambda b, gg, i: (b, gg, 0, 0)),
            pl.BlockSpec((n_s, n_c), lambda b, gg, i: (0, 0)),
        ],
        out_specs=(
            pl.BlockSpec((None, None, hpg, Q_BLOCK, hd), lambda b, gg, i: (b, gg, 0, i, 0)),
            pl.BlockSpec((None, None, None, rows, n_s + hd), lambda b, gg, i: (b, gg, i, 0, 0)),
        ),
        compiler_params=_cparams(("parallel", "parallel", "arbitrary")),
        name="nsa_cmp_select",
    )(q, kc, vc, ovt)


def _flash_step(q, k, v, carry, mask):
    m, l, acc = carry
    s = lax.dot_general(q, k, _NT, preferred_element_type=F32)
    if mask is not None:
        s = jnp.where(mask, s, NEG_INF)
    m_new = jnp.maximum(m, jnp.max(s, axis=-1, keepdims=True))
    alpha = jnp.exp(m - m_new)
    p = jnp.exp(s - m_new)
    l = alpha * l + jnp.sum(p, axis=-1, keepdims=True)
    acc = alpha * acc + jnp.dot(p.astype(BF16), v, preferred_element_type=F32)
    return m_new, l, acc


def _flash_init(rows, hd):
    return (jnp.full((rows, 1), NEG_INF, F32), jnp.zeros((rows, 1), F32), jnp.zeros((rows, hd), F32))


def _nsa_slc_kernel(qaug_ref, kaug_ref, v_ref, o_ref, *, tk):
    rows = NSA_HPG * Q_BLOCK
    t0 = pl.program_id(2) * Q_BLOCK
    q = qaug_ref[...]
    jd = t0 // tk

    def full_step(j, carry):
        k = kaug_ref[pl.ds(pl.multiple_of(j * tk, tk), tk), :]
        v = v_ref[pl.ds(pl.multiple_of(j * tk, tk), tk), :]
        return _flash_step(q, k, v, carry, None)

    carry = lax.fori_loop(0, jd, full_step, _flash_init(rows, HEAD_DIM))
    k = kaug_ref[pl.ds(pl.multiple_of(jd * tk, tk), tk), :]
    v = v_ref[pl.ds(pl.multiple_of(jd * tk, tk), tk), :]
    tq = t0 + (lax.broadcasted_iota(jnp.int32, (rows, tk), 0) & (Q_BLOCK - 1))
    kpos = jd * tk + lax.broadcasted_iota(jnp.int32, (rows, tk), 1)
    _, l, acc = _flash_step(q, k, v, carry, kpos <= tq)
    o_ref[...] = (acc / l).reshape(NSA_HPG, Q_BLOCK, HEAD_DIM).astype(o_ref.dtype)


def _nsa_slc(qaug, kaug, v, tk=256):
    bsz, g, nq, rows, wa = qaug.shape
    t = kaug.shape[2]
    tk = min(tk, t)
    return pl.pallas_call(
        functools.partial(_nsa_slc_kernel, tk=tk),
        out_shape=jax.ShapeDtypeStruct((bsz, g, NSA_HPG, t, HEAD_DIM), BF16),
        grid=(bsz, g, nq),
        in_specs=[
            pl.BlockSpec((None, None, None, rows, wa), lambda b, gg, i: (b, gg, i, 0, 0)),
            pl.BlockSpec((None, None, t, wa), lambda b, gg, i: (b, gg, 0, 0)),
            pl.BlockSpec((None, None, t, HEAD_DIM), lambda b, gg, i: (b, gg, 0, 0)),
        ],
        out_specs=pl.BlockSpec((None, None, NSA_HPG, Q_BLOCK, HEAD_DIM), lambda b, gg, i: (b, gg, 0, i, 0)),
        compiler_params=_cparams(("parallel", "parallel", "arbitrary")),
        name="nsa_selected",
    )(qaug, kaug, v)


def _nsa_win_kernel(q_ref, k_ref, v_ref, o_ref):
    rows = NSA_HPG * Q_BLOCK
    i = pl.program_id(2)
    t0 = i * Q_BLOCK
    n_back = WINDOW // Q_BLOCK
    q = q_ref[...].reshape(rows, HEAD_DIM)
    qi = lax.broadcasted_iota(jnp.int32, (rows, Q_BLOCK), 0) & (Q_BLOCK - 1)
    kj = lax.broadcasted_iota(jnp.int32, (rows, Q_BLOCK), 1)

    def step(j, carry):
        start = pl.multiple_of(t0 + (j - n_back) * Q_BLOCK, Q_BLOCK)
        k = k_ref[pl.ds(start, Q_BLOCK), :]
        v = v_ref[pl.ds(start, Q_BLOCK), :]
        dist = qi - kj + (n_back - j) * Q_BLOCK
        return _flash_step(q, k, v, carry, (dist >= 0) & (dist < WINDOW))

    _, l, acc = lax.fori_loop(jnp.maximum(n_back - i, 0), n_back + 1, step, _flash_init(rows, HEAD_DIM))
    o_ref[...] = (acc / l).reshape(NSA_HPG, Q_BLOCK, HEAD_DIM).astype(o_ref.dtype)


def _nsa_win(q, k, v):
    bsz, g, hpg, t, hd = q.shape
    return pl.pallas_call(
        _nsa_win_kernel,
        out_shape=jax.ShapeDtypeStruct((bsz, g, hpg, t, hd), BF16),
        grid=(bsz, g, t // Q_BLOCK),
        in_specs=[
            pl.BlockSpec((None, None, hpg, Q_BLOCK, hd), lambda b, gg, i: (b, gg, 0, i, 0)),
            pl.BlockSpec((None, None, t, hd), lambda b, gg, i: (b, gg, 0, 0)),
            pl.BlockSpec((None, None, t, hd), lambda b, gg, i: (b, gg, 0, 0)),
        ],
        out_specs=pl.BlockSpec((None, None, hpg, Q_BLOCK, hd), lambda b, gg, i: (b, gg, 0, i, 0)),
        compiler_params=_cparams(("parallel", "parallel", "arbitrary")),
        name="nsa_window",
    )(q, k, v)


def _fox_prep_kernel(z_ref, tri_ref, o_ref, carry_ref):
    @pl.when(pl.program_id(1) == 0)
    def _():
        carry_ref[...] = jnp.zeros_like(carry_ref)

    lf = jax.nn.log_sigmoid(z_ref[...])
    hi, mid, lo = _split3(lf)
    tri = tri_ref[...]
    cum = (jnp.dot(tri, hi, preferred_element_type=F32) + jnp.dot(tri, mid, preferred_element_type=F32)
           + jnp.dot(tri, lo, preferred_element_type=F32)) + carry_ref[...]
    carry_ref[...] = cum[cum.shape[0] - 1:cum.shape[0], :]
    for part, val in enumerate(_split3(cum)):
        o_ref[part] = val


def _fox_prep(z_small, tile=256):
    bsz, t, _ = z_small.shape
    tri = jnp.asarray(np.tril(np.ones((tile, tile), np.float32)), BF16)
    return pl.pallas_call(
        _fox_prep_kernel,
        out_shape=jax.ShapeDtypeStruct((bsz, 3, t, LANES), BF16),
        grid=(bsz, t // tile),
        in_specs=[
            pl.BlockSpec((None, tile, LANES), lambda b, i: (b, i, 0)),
            pl.BlockSpec((tile, tile), lambda b, i: (0, 0)),
        ],
        out_specs=pl.BlockSpec((None, 3, tile, LANES), lambda b, i: (b, 0, i, 0)),
        scratch_shapes=[pltpu.VMEM((1, LANES), F32)],
        compiler_params=_cparams(("parallel", "arbitrary")),
        name="fox_decay_cumsum",
    )(z_small, tri)


def _fox_kernel(q_ref, k_ref, v_ref, o_ref, *, tq):
    i = pl.program_id(2)
    q = q_ref[...]

    def full_step(j, carry):
        k = k_ref[pl.ds(pl.multiple_of(j * tq, tq), tq), :]
        v = v_ref[pl.ds(pl.multiple_of(j * tq, tq), tq), :]
        return _flash_step(q, k, v, carry, None)

    carry = lax.fori_loop(0, i, full_step, _flash_init(tq, HEAD_DIM))
    k = k_ref[pl.ds(pl.multiple_of(i * tq, tq), tq), :]
    v = v_ref[pl.ds(pl.multiple_of(i * tq, tq), tq), :]
    causal = lax.broadcasted_iota(jnp.int32, (tq, tq), 0) >= lax.broadcasted_iota(jnp.int32, (tq, tq), 1)
    _, l, acc = _flash_step(q, k, v, carry, causal)
    o_ref[...] = (acc / l).astype(o_ref.dtype)


def _fox_attention(qaug, kaug, v, tq=256):
    bsz, h, t, wa = qaug.shape
    tq = min(tq, t)
    return pl.pallas_call(
        functools.partial(_fox_kernel, tq=tq),
        out_shape=jax.ShapeDtypeStruct((bsz, h, t, HEAD_DIM), BF16),
        grid=(bsz, h, t // tq),
        in_specs=[
            pl.BlockSpec((None, None, tq, wa), lambda b, hh, i: (b, hh, i, 0)),
            pl.BlockSpec((None, None, t, wa), lambda b, hh, i: (b, hh, 0, 0)),
            pl.BlockSpec((None, None, t, HEAD_DIM), lambda b, hh, i: (b, hh, 0, 0)),
        ],
        out_specs=pl.BlockSpec((None, None, tq, HEAD_DIM), lambda b, hh, i: (b, hh, i, 0)),
        compiler_params=_cparams(("parallel", "parallel", "arbitrary")),
        name="fox_attention",
    )(qaug, kaug, v)


def _merge_kernel(x_ref, yp_ref, oc_ref, os_ref, ow_ref, yf_ref, zs_ref, gm_ref, ex_ref,
                  wup_ref, wo_ref, g_ref, b_ref, o_ref, *, alpha):
    mix_w = yp_ref.shape[-1]
    d = x_ref.shape[-1]
    sg = jax.nn.sigmoid(zs_ref[...])
    hi, mid, lo = _split3(sg)
    ex = ex_ref[...]
    ge = (jnp.dot(hi, ex, preferred_element_type=F32) + jnp.dot(mid, ex, preferred_element_type=F32)
          + jnp.dot(lo, ex, preferred_element_type=F32))
    yn = (ge[:, 0:mix_w] * oc_ref[...].astype(F32) + ge[:, mix_w:2 * mix_w] * os_ref[...].astype(F32)
          + ge[:, 2 * mix_w:3 * mix_w] * ow_ref[...].astype(F32))
    ys = (yp_ref[...], yn.astype(BF16), yf_ref[...])
    mix = None
    for n in range(N_BRANCH):
        up = jnp.dot(ys[n], wup_ref[n], preferred_element_type=F32)
        term = gm_ref[:, n * d:(n + 1) * d].astype(F32) * up
        mix = term if mix is None else mix + term
    h = jnp.dot(mix.astype(BF16), wo_ref[...], preferred_element_type=F32)
    o_ref[...] = _layer_norm(alpha * x_ref[...] + h, g_ref[...], b_ref[...])


def _merge(x, yp, oc, osl, ow, yf, zs, gm, expand, wup, wo, g, b, alpha, tm=512):
    n, d = x.shape
    mix_w = yp.shape[1]
    tok = lambda width: pl.BlockSpec((tm, width), lambda i: (i, 0))
    return pl.pallas_call(
        functools.partial(_merge_kernel, alpha=alpha),
        out_shape=jax.ShapeDtypeStruct((n, d), F32),
        grid=(n // tm,),
        in_specs=[
            tok(d), tok(mix_w), tok(mix_w), tok(mix_w), tok(mix_w), tok(mix_w),
            pl.BlockSpec((tm, LANES), lambda i: (i, 0)),
            tok(N_BRANCH * d),
            pl.BlockSpec((LANES, N_BRANCH * mix_w), lambda i: (0, 0)),
            pl.BlockSpec((N_BRANCH, mix_w, d), lambda i: (0, 0, 0)),
            pl.BlockSpec((d, d), lambda i: (0, 0)),
            pl.BlockSpec((1, d), lambda i: (0, 0)),
            pl.BlockSpec((1, d), lambda i: (0, 0)),
        ],
        out_specs=tok(d),
        compiler_params=_cparams(("parallel",)),
        name="merge_outproj_ln",
    )(x, yp, oc, osl, ow, yf, zs, gm, expand, wup, wo, g.reshape(1, d), b.reshape(1, d))


def _router_kernel(x_ref, w_ref, b_ref, idx_ref, gate_ref):
    logits = jnp.dot(x_ref[...], w_ref[...], preferred_element_type=F32,
                     precision=lax.Precision.HIGHEST) + b_ref[...]
    tm, lanes = logits.shape
    lane = lax.broadcasted_iota(jnp.int32, (tm, lanes), 1)
    work = jnp.where(lane < N_EXPERTS, logits, -jnp.inf)
    idx_out = jnp.zeros((tm, lanes), jnp.int32)
    val_out = jnp.zeros((tm, lanes), F32)
    top = None
    denom = jnp.zeros((tm, 1), F32)
    for k in range(TOP_K):
        mx = jnp.max(work, axis=-1, keepdims=True)
        first = jnp.min(jnp.where(work == mx, lane, lanes), axis=-1, keepdims=True)
        if top is None:
            top = mx
        e = jnp.exp(mx - top)
        denom = denom + e
        idx_out = jnp.where(lane == k, first, idx_out)
        val_out = jnp.where(lane == k, e, val_out)
        work = jnp.where(lane == first, -jnp.inf, work)
    idx_ref[...] = idx_out
    gate_ref[...] = val_out / denom


def _router(x, w, b, tm=512):
    n, d = x.shape
    wp = jnp.zeros((d, LANES), F32).at[:, :N_EXPERTS].set(w)
    bp = jnp.zeros((1, LANES), F32).at[0, :N_EXPERTS].set(b)
    return pl.pallas_call(
        _router_kernel,
        out_shape=(jax.ShapeDtypeStruct((n, LANES), jnp.int32), jax.ShapeDtypeStruct((n, LANES), F32)),
        grid=(n // tm,),
        in_specs=[
            pl.BlockSpec((tm, d), lambda i: (i, 0)),
            pl.BlockSpec((d, LANES), lambda i: (0, 0)),
            pl.BlockSpec((1, LANES), lambda i: (0, 0)),
        ],
        out_specs=(pl.BlockSpec((tm, LANES), lambda i: (i, 0)), pl.BlockSpec((tm, LANES), lambda i: (i, 0))),
        compiler_params=_cparams(("parallel",)),
        name="moe_router",
    )(x, wp, bp)


def _gather_kernel(tok_ref, x_hbm, o_ref, sem, *, rows):
    def row_copy(r):
        return pltpu.make_async_copy(x_hbm.at[pl.ds(tok_ref[0, r], 1), :], o_ref.at[pl.ds(r, 1), :], sem)

    def issue(r, c):
        row_copy(r).start()
        return c

    def drain(r, c):
        row_copy(r).wait()
        return c

    lax.fori_loop(0, rows, issue, 0)
    lax.fori_loop(0, rows, drain, 0)


def _gather_rows(x, buf_tok):
    n, d = x.shape
    n_blk = buf_tok.shape[0]
    return pl.pallas_call(
        functools.partial(_gather_kernel, rows=MOE_BLOCK),
        out_shape=jax.ShapeDtypeStruct((n_blk * MOE_BLOCK, d), x.dtype),
        grid=(n_blk,),
        in_specs=[
            pl.BlockSpec((None, 1, MOE_BLOCK), lambda i: (i, 0, 0), memory_space=pltpu.SMEM),
            pl.BlockSpec(memory_space=pl.ANY),
        ],
        out_specs=pl.BlockSpec((MOE_BLOCK, d), lambda i: (i, 0)),
        scratch_shapes=[pltpu.SemaphoreType.DMA(())],
        compiler_params=_cparams(("arbitrary",)),
        name="moe_gather",
    )(buf_tok, x)


def _expert_kernel(be_ref, nu_ref, x_ref, w1_ref, b1_ref, w2_ref, b2_ref, o_ref):
    @pl.when(pl.program_id(0) < nu_ref[0])
    def _():
        dff = w2_ref.shape[0]
        h = jnp.dot(x_ref[...].astype(BF16), w1_ref[...], preferred_element_type=F32) + b1_ref[...]
        gate = jnp.minimum(h[:, 0:dff], SWIGLU_LIMIT)
        upv = jnp.clip(h[:, dff:2 * dff], -SWIGLU_LIMIT, SWIGLU_LIMIT)
        act = (upv + 1.0) * (gate * jax.nn.sigmoid(SWIGLU_ALPHA * gate))
        o_ref[...] = jnp.dot(act.astype(BF16), w2_ref[...], preferred_element_type=F32) + b2_ref[...]

    @pl.when(pl.program_id(0) >= nu_ref[0])
    def _():
        o_ref[...] = jnp.zeros_like(o_ref)


def _expert_ffn(xs, blk_e, n_used, w1, b1, w2, b2):
    p, d = xs.shape
    n_blk = p // MOE_BLOCK
    dff = w2.shape[1]
    grid_spec = pltpu.PrefetchScalarGridSpec(
        num_scalar_prefetch=2,
        grid=(n_blk,),
        in_specs=[
            pl.BlockSpec((MOE_BLOCK, d), lambda i, be, nu: (i, 0)),
            pl.BlockSpec((None, d, 2 * dff), lambda i, be, nu: (be[i], 0, 0)),
            pl.BlockSpec((None, 1, 2 * dff), lambda i, be, nu: (be[i], 0, 0)),
            pl.BlockSpec((None, dff, d), lambda i, be, nu: (be[i], 0, 0)),
            pl.BlockSpec((None, 1, d), lambda i, be, nu: (be[i], 0, 0)),
        ],
        out_specs=pl.BlockSpec((MOE_BLOCK, d), lambda i, be, nu: (i, 0)),
    )
    return pl.pallas_call(
        _expert_kernel,
        out_shape=jax.ShapeDtypeStruct((p, d), F32),
        grid_spec=grid_spec,
        compiler_params=_cparams(("arbitrary",)),
        name="moe_experts",
    )(blk_e, n_used, xs, w1, b1.reshape(N_EXPERTS, 1, 2 * dff), w2, b2.reshape(N_EXPERTS, 1, d))


def _combine_kernel(row_ref, x_ref, gate_ref, yb_hbm, g_ref, b_ref, o_ref, buf, sem, *, tm, alpha):
    def row_copy(r, k):
        return pltpu.make_async_copy(yb_hbm.at[pl.ds(row_ref[0, r * TOP_K + k], 1), :],
                                     buf.at[k, pl.ds(r, 1), :], sem)

    def issue(r, c):
        for k in range(TOP_K):
            row_copy(r, k).start()
        return c

    def drain(r, c):
        for k in range(TOP_K):
            row_copy(r, k).wait()
        return c

    lax.fori_loop(0, tm, issue, 0)
    lax.fori_loop(0, tm, drain, 0)
    gates = gate_ref[...]
    y = gates[:, 0:1] * buf[0]
    for k in range(1, TOP_K):
        y = y + gates[:, k:k + 1] * buf[k]
    o_ref[...] = _layer_norm(alpha * x_ref[...] + y, g_ref[...], b_ref[...])


def _combine(x, gates, row_of, yb, g, b, alpha, tm=256):
    n, d = x.shape
    return pl.pallas_call(
        functools.partial(_combine_kernel, tm=tm, alpha=alpha),
        out_shape=jax.ShapeDtypeStruct((n, d), F32),
        grid=(n // tm,),
        in_specs=[
            pl.BlockSpec((None, 1, tm * TOP_K), lambda i: (i, 0, 0), memory_space=pltpu.SMEM),
            pl.BlockSpec((tm, d), lambda i: (i, 0)),
            pl.BlockSpec((tm, LANES), lambda i: (i, 0)),
            pl.BlockSpec(memory_space=pl.ANY),
            pl.BlockSpec((1, d), lambda i: (0, 0)),
            pl.BlockSpec((1, d), lambda i: (0, 0)),
        ],
        out_specs=pl.BlockSpec((tm, d), lambda i: (i, 0)),
        scratch_shapes=[pltpu.VMEM((TOP_K, tm, d), F32), pltpu.SemaphoreType.DMA(())],
        compiler_params=_cparams(("arbitrary",)),
        name="moe_combine_ln",
    )(row_of.reshape(n // tm, 1, tm * TOP_K), x, gates, yb, g.reshape(1, d), b.reshape(1, d))


def _head_major(z, n_heads):
    bsz, t, _ = z.shape
    return z.reshape(bsz, t, n_heads, HEAD_DIM).transpose(0, 2, 1, 3)


def _token_major(o):
    bsz, t = o.shape[0], o.shape[-2]
    o = o.reshape(bsz, -1, t, HEAD_DIM).transpose(0, 2, 1, 3)
    return o.reshape(bsz * t, -1)


def _mixer_layer(x, w_in, b_in, pool_w, pool_b, pool_scale, cmp_pos, cmp_w1, cmp_b1, cmp_w2, cmp_b2,
                 w_up, w_o, ln_g, ln_b, alpha):
    bsz, t, d = x.shape
    n = bsz * t
    mix_w = d // 2
    g, hpg = NSA_KV_GROUPS, NSA_HPG
    nq_w = NSA_HEADS * HEAD_DIM
    nkv_w = N_BRANCH * 2 * g * HEAD_DIM
    nfox_w = 3 * FOX_HEADS * HEAD_DIM
    sizes = (mix_w, nq_w, nkv_w, N_BRANCH * NSA_HEADS, nfox_w, FOX_HEADS, N_BRANCH * d)
    o_pool, o_q, o_kv, o_g, o_fox, o_f, o_gm = (int(v) for v in np.cumsum((0,) + sizes[:-1]))
    n_gate = N_BRANCH * NSA_HEADS

    def cols(a, lo, width):
        return lax.slice_in_dim(a, lo, lo + width, axis=-1)

    pad_small = 2 * LANES - n_gate - FOX_HEADS
    w_a = jnp.concatenate([cols(w_in, o_pool, mix_w), cols(w_in, o_g, n_gate), cols(w_in, o_f, FOX_HEADS),
                           jnp.zeros((d, pad_small), F32)], axis=1).astype(BF16)
    b_a = jnp.concatenate([cols(b_in, o_pool, mix_w), cols(b_in, o_g, n_gate), cols(b_in, o_f, FOX_HEADS),
                           jnp.zeros((pad_small,), F32)])
    w_b = jnp.concatenate([cols(w_in, o_q, nq_w) * SCALE, cols(w_in, o_kv, nkv_w),
                           cols(w_in, o_fox, FOX_HEADS * HEAD_DIM) * SCALE,
                           cols(w_in, o_fox + FOX_HEADS * HEAD_DIM, 2 * FOX_HEADS * HEAD_DIM)], axis=1).astype(BF16)
    b_b = jnp.concatenate([cols(b_in, o_q, nq_w) * SCALE, cols(b_in, o_kv, nkv_w),
                           cols(b_in, o_fox, FOX_HEADS * HEAD_DIM) * SCALE,
                           cols(b_in, o_fox + FOX_HEADS * HEAD_DIM, 2 * FOX_HEADS * HEAD_DIM)])
    w_c = cols(w_in, o_gm, N_BRANCH * d).astype(BF16)
    b_c = cols(b_in, o_gm, N_BRANCH * d)

    xf = x.reshape(n, d)
    z_a = _matmul_bias(xf, w_a, b_a, F32, tn=mix_w + 2 * LANES)
    z_b = _matmul_bias(xf, w_b, b_b, BF16, tn=256)
    gm = _matmul_bias(xf, w_c, b_c, BF16, act="sigmoid")

    z_a = z_a.reshape(bsz, t, -1)
    z_b = z_b.reshape(bsz, t, -1)
    u_pool = cols(z_a, 0, mix_w)
    z_small = cols(z_a, mix_w, 2 * LANES)

    y_pool = _pool_mixer(u_pool, pool_w, pool_b, pool_scale).reshape(n, mix_w)

    q_nsa = _head_major(cols(z_b, 0, nq_w), NSA_HEADS).reshape(bsz, g, hpg, t, HEAD_DIM)
    kv = _head_major(cols(z_b, nq_w, nkv_w), N_BRANCH * 2 * g).reshape(bsz, N_BRANCH, 2, g, t, HEAD_DIM)
    n_chunk = t // CMP_STRIDE
    a_cmp = kv[:, 0].transpose(1, 0, 2, 3, 4).reshape(2, bsz * g, n_chunk, CMP_STRIDE * HEAD_DIM)
    kvc = _compress(a_cmp, cmp_pos.reshape(2, CMP_LEN * HEAD_DIM), cmp_w1, cmp_b1, cmp_w2, cmp_b2)
    kvc = kvc.reshape(2, bsz, g, n_chunk, HEAD_DIM)
    n_s = t // SEL_LEN
    c_start = np.arange(n_chunk) * CMP_STRIDE
    s_start = np.arange(n_s) * SEL_LEN
    ov = (c_start[:, None] < s_start[None, :] + SEL_LEN) & (c_start[:, None] + CMP_LEN > s_start[None, :])
    ov[n_chunk - (CMP_LEN // CMP_STRIDE) + 1:] = False
    ovt = jnp.asarray(ov.T.astype(np.float32), BF16)
    o_cmp, qaug = _nsa_cmp(q_nsa, kvc[0], kvc[1], ovt)
    onehot = jnp.asarray((np.arange(t)[:, None] // SEL_LEN == np.arange(n_s)[None, :]).astype(np.float32), BF16)
    kaug = jnp.concatenate([jnp.broadcast_to(onehot, (bsz, g, t, n_s)), kv[:, 1, 0]], axis=-1)
    o_slc = _nsa_slc(qaug, kaug, kv[:, 1, 1])
    o_win = _nsa_win(q_nsa, kv[:, 2, 0], kv[:, 2, 1])

    cum = _fox_prep(z_small)
    cq = cols(cum, n_gate, FOX_HEADS).transpose(0, 3, 2, 1)
    fox = _head_major(cols(z_b, nq_w + nkv_w, nfox_w), 3 * FOX_HEADS).reshape(bsz, 3, FOX_HEADS, t, HEAD_DIM)
    ones = jnp.ones((bsz, FOX_HEADS, t, 3), BF16)
    zpad = jnp.zeros((bsz, FOX_HEADS, t, LANES - HEAD_DIM - 6), BF16)
    q_fox = jnp.concatenate([fox[:, 0], cq, ones, zpad], axis=-1)
    k_fox = jnp.concatenate([fox[:, 1], ones, -cq, zpad], axis=-1)
    y_fox = _fox_attention(q_fox, k_fox, fox[:, 2])

    hsel = np.arange(NSA_HEADS * HEAD_DIM) // HEAD_DIM
    expand = np.zeros((LANES, N_BRANCH * mix_w), np.float32)
    for br in range(N_BRANCH):
        expand[hsel * N_BRANCH + br, br * mix_w + np.arange(mix_w)] = 1.0
    return _merge(xf, y_pool, _token_major(o_cmp), _token_major(o_slc), _token_major(o_win),
                  _token_major(y_fox), cols(z_small, 0, LANES).reshape(n, LANES), gm,
                  jnp.asarray(expand, BF16), w_up.astype(BF16), w_o.astype(BF16), ln_g, ln_b, alpha)


def _moe_layer(x, router_w, router_b, w1, b1, w2, b2, ln_g, ln_b, alpha):
    n, d = x.shape
    nk = n * TOP_K
    idx_p, gate_p = _router(x, router_w, router_b)
    e_flat = idx_p[:, :TOP_K].reshape(nk)
    onehot = (e_flat[:, None] == jnp.arange(N_EXPERTS)[None, :]).astype(jnp.int32)
    incl = jnp.cumsum(onehot, axis=0)
    counts = incl[-1]
    rank = jnp.sum((incl - onehot) * onehot, axis=1)
    padded = (counts + MOE_BLOCK - 1) // MOE_BLOCK * MOE_BLOCK
    pend = jnp.cumsum(padded)
    pstart = pend - padded
    dest = pstart[e_flat] + rank
    n_blk = (nk + MOE_BLOCK - 1) // MOE_BLOCK + N_EXPERTS
    p = n_blk * MOE_BLOCK
    buf_tok = jnp.zeros((p,), jnp.int32).at[dest].set(jnp.arange(nk, dtype=jnp.int32) // TOP_K)
    blk_e = jnp.minimum(jnp.sum(jnp.arange(n_blk)[:, None] * MOE_BLOCK >= pend[None, :], axis=1),
                        N_EXPERTS - 1).astype(jnp.int32)
    n_used = (pend[-1] // MOE_BLOCK).astype(jnp.int32).reshape(1)

    xs = _gather_rows(x, buf_tok.reshape(n_blk, 1, MOE_BLOCK))
    yb = _expert_ffn(xs, blk_e, n_used, w1.astype(BF16), b1, w2.astype(BF16), b2)
    return _combine(x, gate_p, dest.astype(jnp.int32), yb, ln_g, ln_b, alpha)


def kernel(x, w_in, b_in, pool_w, pool_b, pool_scale, cmp_pos, cmp_w1, cmp_b1, cmp_w2, cmp_b2,
           w_up, w_o, ln1_g, ln1_b, router_w, router_b, moe_w1, moe_b1, moe_w2, moe_b2, ln2_g, ln2_b):
    depth = w_in.shape[0]
    alpha = (2 * depth) ** 0.25
    bsz, t, d = x.shape
    for l in range(depth):
        x1 = _mixer_layer(x, w_in[l], b_in[l], pool_w[l], pool_b[l], pool_scale[l], cmp_pos[l], cmp_w1[l],
                          cmp_b1[l], cmp_w2[l], cmp_b2[l], w_up[l], w_o[l], ln1_g[l], ln1_b[l], alpha)
        x2 = _moe_layer(x1, router_w[l], router_b[l], moe_w1[l], moe_b1[l], moe_w2[l], moe_b2[l],
                        ln2_g[l], ln2_b[l], alpha)
        x = x2.reshape(bsz, t, d)
    return x
```

```python
import functools

import numpy as np
import jax
import jax.numpy as jnp
from jax import lax
from jax.experimental import pallas as pl
from jax.experimental.pallas import tpu as pltpu

F32 = jnp.float32
BF16 = jnp.bfloat16

HEAD_DIM = 64
POOL_WINDOWS = (2, 4, 8, 16)
POOL_GC = 128
POOL_HALO = 16
NSA_HEADS = 8
NSA_KV_GROUPS = 2
NSA_HPG = NSA_HEADS // NSA_KV_GROUPS
N_BRANCH = 3
CMP_LEN = 32
CMP_STRIDE = 16
SEL_LEN = 64
N_SEL = 16
WINDOW = 512
FOX_HEADS = 8
Q_BLOCK = 128
N_EXPERTS = 32
TOP_K = 4
SWIGLU_LIMIT = 7.0
SWIGLU_ALPHA = 1.702
MOE_BLOCK = 256
LN_EPS = 1e-5
NEG_INF = -1e30
SCALE = HEAD_DIM ** -0.5
MASK_BIG = 2.0 ** 100
LOG2E = 1.4426950408889634
ONES_ROWS = 8
LANES = 128
SUBLANES = 8
VMEM_LIMIT = 48 * 1024 * 1024


def _cparams(sem):
    return pltpu.CompilerParams(dimension_semantics=sem, vmem_limit_bytes=VMEM_LIMIT)


def _split3(x):
    hi = x.astype(BF16)
    r1 = x - hi.astype(F32)
    mid = r1.astype(BF16)
    lo = (r1 - mid.astype(F32)).astype(BF16)
    return hi, mid, lo


def _layer_norm(r, g, b):
    mu = jnp.mean(r, axis=-1, keepdims=True)
    var = jnp.mean(jnp.square(r - mu), axis=-1, keepdims=True)
    return (r - mu) * lax.rsqrt(var + LN_EPS) * g + b


def _load_rows_tiled(ref, rows):
    return jnp.concatenate([ref[pl.ds(s, rows, stride=SUBLANES), :] for s in range(SUBLANES)], axis=1)


def _store_rows_tiled(ref, val):
    rows = val.shape[0]
    for s in range(SUBLANES):
        ref[pl.ds(s, rows, stride=SUBLANES), :] = val[:, s * LANES:(s + 1) * LANES]


def _row_tile_copy(src, src_row, dst, dst_row, sem):
    return pltpu.make_async_copy(src.at[pl.ds(pl.multiple_of(src_row * SUBLANES, SUBLANES), SUBLANES), :],
                                 dst.at[pl.ds(pl.multiple_of(dst_row * SUBLANES, SUBLANES), SUBLANES), :], sem)


def _mm_kernel(x_ref, w_ref, b_ref, o_ref, *, act):
    acc = jnp.dot(x_ref[...].astype(BF16), w_ref[...], preferred_element_type=F32) + b_ref[...]
    if act == "sigmoid":
        acc = jax.nn.sigmoid(acc)
    o_ref[...] = acc.astype(o_ref.dtype)


def _matmul_bias(x, w, b, out_dtype, act=None, tm=1024, tn=512):
    m, k = x.shape
    n = w.shape[1]
    tn = min(tn, n)
    assert m % tm == 0 and n % tn == 0
    return pl.pallas_call(
        functools.partial(_mm_kernel, act=act),
        out_shape=jax.ShapeDtypeStruct((m, n), out_dtype),
        grid=(m // tm, n // tn),
        in_specs=[
            pl.BlockSpec((tm, k), lambda i, j: (i, 0)),
            pl.BlockSpec((k, tn), lambda i, j: (0, j)),
            pl.BlockSpec((1, tn), lambda i, j: (0, j)),
        ],
        out_specs=pl.BlockSpec((tm, tn), lambda i, j: (i, j)),
        compiler_params=_cparams(("parallel", "arbitrary")),
        name="in_proj",
    )(x, w, b.reshape(1, n))


def _pool_kernel(prev_ref, cur_ref, w_ref, b_ref, sc_ref, o_ref, ext_ref, *, tile):
    i = pl.program_id(1)
    ext_ref[0:POOL_HALO, :] = jnp.where(i > 0, prev_ref[...], 0.0)
    ext_ref[POOL_HALO:POOL_HALO + tile, :] = cur_ref[...]
    t_idx = i * tile + lax.broadcasted_iota(jnp.int32, (tile, 1), 0)
    for gi, win in enumerate(POOL_WINDOWS):
        cols = slice(gi * POOL_GC, (gi + 1) * POOL_GC)
        u = ext_ref[POOL_HALO:POOL_HALO + tile, cols]
        wsum = u
        for j in range(1, win):
            wsum = wsum + ext_ref[POOL_HALO - j:POOL_HALO - j + tile, cols]
        cnt = jnp.minimum(t_idx + 1, win).astype(F32)
        d = wsum / cnt - u
        y = jnp.dot(d.astype(BF16), w_ref[gi], preferred_element_type=F32) + b_ref[gi]
        o_ref[:, cols] = (y * sc_ref[:, cols]).astype(o_ref.dtype)


def _pool_mixer(u, w, b, scale, tile=512):
    bsz, t, c = u.shape
    return pl.pallas_call(
        functools.partial(_pool_kernel, tile=tile),
        out_shape=jax.ShapeDtypeStruct((bsz, t, c), BF16),
        grid=(bsz, t // tile),
        in_specs=[
            pl.BlockSpec((None, POOL_HALO, c),
                         lambda bb, i: (bb, jnp.maximum(i * (tile // POOL_HALO) - 1, 0), 0)),
            pl.BlockSpec((None, tile, c), lambda bb, i: (bb, i, 0)),
            pl.BlockSpec((len(POOL_WINDOWS), POOL_GC, POOL_GC), lambda bb, i: (0, 0, 0)),
            pl.BlockSpec((len(POOL_WINDOWS), 1, POOL_GC), lambda bb, i: (0, 0, 0)),
            pl.BlockSpec((1, c), lambda bb, i: (0, 0)),
        ],
        out_specs=pl.BlockSpec((None, tile, c), lambda bb, i: (bb, i, 0)),
        scratch_shapes=[pltpu.VMEM((POOL_HALO + tile, c), F32)],
        compiler_params=_cparams(("parallel", "arbitrary")),
        name="pool_mixer",
    )(u, u, w.astype(BF16), b.reshape(len(POOL_WINDOWS), 1, POOL_GC), scale.reshape(1, c))


def _compress_kernel(a_ref, pos_ref, w1_ref, b1_ref, w2_ref, b2_ref, o_ref):
    half = CMP_STRIDE * HEAD_DIM
    a = a_ref[...].astype(F32)
    n_chunk = a.shape[0]
    top = (a + pos_ref[:, 0:half]).astype(BF16)
    bot = (a + pos_ref[:, half:2 * half]).astype(BF16)
    p1 = jnp.dot(top, w1_ref[0:half, :], preferred_element_type=F32)
    p2 = jnp.dot(bot, w1_ref[half:2 * half, :], preferred_element_type=F32)
    h = p1 + pltpu.roll(p2, n_chunk - 1, axis=0) + b1_ref[...]
    h = jax.nn.gelu(h)
    o = jnp.dot(h.astype(BF16), w2_ref[...], preferred_element_type=F32) + b2_ref[...]
    o_ref[...] = o.astype(o_ref.dtype)


def _compress(a, pos, w1, b1, w2, b2):
    _, bg, n_chunk, width = a.shape
    hid = w1.shape[-1]
    return pl.pallas_call(
        _compress_kernel,
        out_shape=jax.ShapeDtypeStruct((2, bg, n_chunk, HEAD_DIM), BF16),
        grid=(2, bg),
        in_specs=[
            pl.BlockSpec((None, None, n_chunk, width), lambda s, j: (s, j, 0, 0)),
            pl.BlockSpec((None, 1, 2 * width), lambda s, j: (s, 0, 0)),
            pl.BlockSpec((None, 2 * width, hid), lambda s, j: (s, 0, 0)),
            pl.BlockSpec((None, 1, hid), lambda s, j: (s, 0, 0)),
            pl.BlockSpec((None, hid, HEAD_DIM), lambda s, j: (s, 0, 0)),
            pl.BlockSpec((None, 1, HEAD_DIM), lambda s, j: (s, 0, 0)),
        ],
        out_specs=pl.BlockSpec((None, None, n_chunk, HEAD_DIM), lambda s, j: (s, j, 0, 0)),
        compiler_params=_cparams(("arbitrary", "arbitrary")),
        name="nsa_compress",
    )(a, pos.reshape(2, 1, 2 * width), w1.astype(BF16), b1.reshape(2, 1, hid),
      w2.astype(BF16), b2.reshape(2, 1, HEAD_DIM))


def _flash_steps_t(qts, ks, vts, carries, mask):
    scores = [jnp.dot(k, qt, preferred_element_type=F32) for k, qt in zip(ks, qts)]
    if mask is not None:
        scores = [jnp.where(mask, s, NEG_INF) for s in scores]
    probs, stats = [], []
    for s, (m, _) in zip(scores, carries):
        m_new = jnp.maximum(m, jnp.max(s, axis=0, keepdims=True))
        probs.append(jnp.exp2(s - m_new).astype(BF16))
        stats.append((m_new, jnp.exp2(m - m_new)))
    return tuple((m_new, alpha * acc + jnp.dot(vt, p, preferred_element_type=F32))
                 for vt, p, (m_new, alpha), (_, acc) in zip(vts, probs, stats, carries))


def _flash_init_t(cols):
    return (jnp.full((1, cols), NEG_INF, F32), jnp.zeros((HEAD_DIM + ONES_ROWS, cols), F32))


def _flash_finish_t(acc):
    return acc[0:HEAD_DIM] / acc[HEAD_DIM:HEAD_DIM + 1]


def _store_heads(o_ref, o, lead=()):
    for h in range(NSA_HPG):
        o_ref[lead + (h,)] = o[:, h * Q_BLOCK:(h + 1) * Q_BLOCK].astype(o_ref.dtype)


def _nsa_cmp_kernel(qt_ref, kc_ref, vct_ref, ovt_ref, ocmp_ref, qaug_ref):
    cols = NSA_HPG * Q_BLOCK
    t0 = pl.program_id(2) * Q_BLOCK
    qt = jnp.concatenate([qt_ref[h] for h in range(NSA_HPG)], axis=1)
    kc = kc_ref[...]
    n_c = kc.shape[0]
    s = jnp.dot(kc, qt, preferred_element_type=F32)
    tq = t0 + (lax.broadcasted_iota(jnp.int32, (n_c, cols), 1) & (Q_BLOCK - 1))
    c_end = lax.broadcasted_iota(jnp.int32, (n_c, cols), 0) * CMP_STRIDE + (CMP_LEN - 1)
    mask = c_end <= tq
    sm = jnp.where(mask, s, NEG_INF)
    m = jnp.max(sm, axis=0, keepdims=True)
    e = jnp.where(mask, jnp.exp2(sm - m), 0.0)
    l = jnp.sum(e, axis=0, keepdims=True)
    p = e / jnp.where(l > 0.0, l, 1.0)
    _store_heads(ocmp_ref, jnp.dot(vct_ref[...], p.astype(BF16), preferred_element_type=F32))

    psum = p[:, 0:Q_BLOCK]
    for h in range(1, NSA_HPG):
        psum = psum + p[:, h * Q_BLOCK:(h + 1) * Q_BLOCK]
    ovt = ovt_ref[...]
    imp = None
    for part in _split3(psum):
        term = jnp.dot(ovt, part, preferred_element_type=F32)
        imp = term if imp is None else imp + term
    n_s = imp.shape[0]
    blk = lax.broadcasted_iota(jnp.int32, (n_s, Q_BLOCK), 0)
    tqq = t0 + lax.broadcasted_iota(jnp.int32, (n_s, Q_BLOCK), 1)
    future = blk * SEL_LEN > tqq
    cur = lax.shift_right_logical(tqq, 6)
    forced = (blk == 0) | (blk == cur) | (blk == cur - 1)
    score = jnp.where(future, -1.0, jnp.where(forced, 1e6, imp))
    sel = jnp.zeros((n_s, Q_BLOCK), F32)
    for _ in range(min(N_SEL, n_s)):
        mx = jnp.max(score, axis=0, keepdims=True)
        first = jnp.min(jnp.where(score == mx, blk, n_s), axis=0, keepdims=True)
        pick = blk == first
        sel = jnp.where(pick, 1.0, sel)
        score = jnp.where(pick, -2.0, score)
    sel = jnp.where(future, 0.0, sel)
    bias = ((sel - 1.0) * MASK_BIG).astype(BF16)
    for h in range(NSA_HPG):
        qaug_ref[0:n_s, h * Q_BLOCK:(h + 1) * Q_BLOCK] = bias
    qaug_ref[n_s:n_s + HEAD_DIM, :] = qt


def _nsa_cmp(qt, kc, vct, ovt):
    bsz, g, hpg, hd, t = qt.shape
    n_c = kc.shape[2]
    n_s = ovt.shape[0]
    nq = t // Q_BLOCK
    cols = hpg * Q_BLOCK
    return pl.pallas_call(
        _nsa_cmp_kernel,
        out_shape=(jax.ShapeDtypeStruct((bsz, g, hpg, hd, t), BF16),
                   jax.ShapeDtypeStruct((bsz, g, nq, n_s + hd, cols), BF16)),
        grid=(bsz, g, nq),
        in_specs=[
            pl.BlockSpec((None, None, hpg, hd, Q_BLOCK), lambda b, gg, i: (b, gg, 0, 0, i)),
            pl.BlockSpec((None, None, n_c, hd), lambda b, gg, i: (b, gg, 0, 0)),
            pl.BlockSpec((None, None, hd, n_c), lambda b, gg, i: (b, gg, 0, 0)),
            pl.BlockSpec((n_s, n_c), lambda b, gg, i: (0, 0)),
        ],
        out_specs=(
            pl.BlockSpec((None, None, hpg, hd, Q_BLOCK), lambda b, gg, i: (b, gg, 0, 0, i)),
            pl.BlockSpec((None, None, None, n_s + hd, cols), lambda b, gg, i: (b, gg, i, 0, 0)),
        ),
        compiler_params=_cparams(("parallel", "parallel", "arbitrary")),
        name="nsa_cmp_select",
    )(qt, kc, vct, ovt)


def _nsa_slc_kernel(qaug_ref, kaug_ref, vt_ref, o_ref, *, tk):
    groups = qaug_ref.shape[0]
    cols = NSA_HPG * Q_BLOCK
    t0 = pl.program_id(1) * Q_BLOCK
    jd = t0 // tk

    def chunk(j, carries, mask):
        start = pl.multiple_of(j * tk, tk)
        return _flash_steps_t([qaug_ref[g] for g in range(groups)],
                              [kaug_ref[g, pl.ds(start, tk), :] for g in range(groups)],
                              [vt_ref[g, :, pl.ds(start, tk)] for g in range(groups)], carries, mask)

    carries = lax.fori_loop(0, jd, lambda j, c: chunk(j, c, None), tuple(_flash_init_t(cols) for _ in range(groups)))
    kpos = jd * tk + lax.broadcasted_iota(jnp.int32, (tk, cols), 0)
    tq = t0 + (lax.broadcasted_iota(jnp.int32, (tk, cols), 1) & (Q_BLOCK - 1))
    carries = chunk(jd, carries, kpos <= tq)
    for g in range(groups):
        _store_heads(o_ref, _flash_finish_t(carries[g][1]), (g,))


def _nsa_slc(qaug, kaug, vt, tk=512):
    bsz, g, nq, wa, cols = qaug.shape
    t = kaug.shape[2]
    tk = min(tk, t)
    return pl.pallas_call(
        functools.partial(_nsa_slc_kernel, tk=tk),
        out_shape=jax.ShapeDtypeStruct((bsz, g, NSA_HPG, HEAD_DIM, t), BF16),
        grid=(bsz, nq),
        in_specs=[
            pl.BlockSpec((None, g, None, wa, cols), lambda b, i: (b, 0, i, 0, 0)),
            pl.BlockSpec((None, g, t, wa), lambda b, i: (b, 0, 0, 0)),
            pl.BlockSpec((None, g, HEAD_DIM + ONES_ROWS, t), lambda b, i: (b, 0, 0, 0)),
        ],
        out_specs=pl.BlockSpec((None, g, NSA_HPG, HEAD_DIM, Q_BLOCK), lambda b, i: (b, 0, 0, 0, i)),
        compiler_params=_cparams(("parallel", "arbitrary")),
        name="nsa_selected",
    )(qaug, kaug, vt)


def _nsa_win_kernel(qt_ref, k_ref, vt_ref, o_ref):
    groups = qt_ref.shape[0]
    cols = NSA_HPG * Q_BLOCK
    span = WINDOW + Q_BLOCK
    t0 = pl.program_id(1) * Q_BLOCK
    start = pl.multiple_of(jnp.maximum(t0 - WINDOW, 0), Q_BLOCK)
    dist = (t0 - start) + (lax.broadcasted_iota(jnp.int32, (span, cols), 1) & (Q_BLOCK - 1)) \
        - lax.broadcasted_iota(jnp.int32, (span, cols), 0)
    mask = (dist >= 0) & (dist < WINDOW)
    scores = []
    for g in range(groups):
        qt = jnp.concatenate([qt_ref[g, h] for h in range(NSA_HPG)], axis=1)
        s = jnp.dot(k_ref[g, pl.ds(start, span), :], qt, preferred_element_type=F32)
        scores.append(jnp.where(mask, s, NEG_INF))
    probs = [jnp.exp2(s - jnp.max(s, axis=0, keepdims=True)).astype(BF16) for s in scores]
    for g in range(groups):
        acc = jnp.dot(vt_ref[g, :, pl.ds(start, span)], probs[g], preferred_element_type=F32)
        _store_heads(o_ref, _flash_finish_t(acc), (g,))


def _nsa_win(qt, k, vt):
    bsz, g, hpg, hd, t = qt.shape
    assert t >= WINDOW + Q_BLOCK
    return pl.pallas_call(
        _nsa_win_kernel,
        out_shape=jax.ShapeDtypeStruct((bsz, g, hpg, hd, t), BF16),
        grid=(bsz, t // Q_BLOCK),
        in_specs=[
            pl.BlockSpec((None, g, hpg, hd, Q_BLOCK), lambda b, i: (b, 0, 0, 0, i)),
            pl.BlockSpec((None, g, t, hd), lambda b, i: (b, 0, 0, 0)),
            pl.BlockSpec((None, g, hd + ONES_ROWS, t), lambda b, i: (b, 0, 0, 0)),
        ],
        out_specs=pl.BlockSpec((None, g, hpg, hd, Q_BLOCK), lambda b, i: (b, 0, 0, 0, i)),
        compiler_params=_cparams(("parallel", "arbitrary")),
        name="nsa_window",
    )(qt, k, vt)


def _fox_prep_kernel(z_ref, tri_ref, o_ref, carry_ref):
    @pl.when(pl.program_id(1) == 0)
    def _():
        carry_ref[...] = jnp.zeros_like(carry_ref)

    lf = jax.nn.log_sigmoid(z_ref[...])
    hi, mid, lo = _split3(lf)
    tri = tri_ref[...]
    cum = (jnp.dot(tri, hi, preferred_element_type=F32) + jnp.dot(tri, mid, preferred_element_type=F32)
           + jnp.dot(tri, lo, preferred_element_type=F32)) + carry_ref[...]
    carry_ref[...] = cum[cum.shape[0] - 1:cum.shape[0], :]
    for part, val in enumerate(_split3(cum * LOG2E)):
        o_ref[part] = val


def _fox_prep(z_small, tile=256):
    bsz, t, _ = z_small.shape
    tri = jnp.asarray(np.tril(np.ones((tile, tile), np.float32)), BF16)
    return pl.pallas_call(
        _fox_prep_kernel,
        out_shape=jax.ShapeDtypeStruct((bsz, 3, t, LANES), BF16),
        grid=(bsz, t // tile),
        in_specs=[
            pl.BlockSpec((None, tile, LANES), lambda b, i: (b, i, 0)),
            pl.BlockSpec((tile, tile), lambda b, i: (0, 0)),
        ],
        out_specs=pl.BlockSpec((None, 3, tile, LANES), lambda b, i: (b, 0, i, 0)),
        scratch_shapes=[pltpu.VMEM((1, LANES), F32)],
        compiler_params=_cparams(("parallel", "arbitrary")),
        name="fox_decay_cumsum",
    )(z_small, tri)


def _fox_kernel(qt_ref, k_ref, vt_ref, o_ref, *, tq, heads):
    i = pl.program_id(2)

    def chunk(j, carries, mask):
        start = pl.multiple_of(j * tq, tq)
        return _flash_steps_t([qt_ref[h] for h in range(heads)],
                              [k_ref[h, pl.ds(start, tq), :] for h in range(heads)],
                              [vt_ref[h, :, pl.ds(start, tq)] for h in range(heads)], carries, mask)

    carries = lax.fori_loop(0, i, lambda j, c: chunk(j, c, None), tuple(_flash_init_t(tq) for _ in range(heads)))
    causal = lax.broadcasted_iota(jnp.int32, (tq, tq), 0) <= lax.broadcasted_iota(jnp.int32, (tq, tq), 1)
    carries = chunk(i, carries, causal)
    for h in range(heads):
        o_ref[h] = _flash_finish_t(carries[h][1]).astype(o_ref.dtype)


def _fox_attention(qt, kaug, vt, tq=512, heads=2):
    bsz, h, wa, t = qt.shape
    tq = min(tq, t)
    return pl.pallas_call(
        functools.partial(_fox_kernel, tq=tq, heads=heads),
        out_shape=jax.ShapeDtypeStruct((bsz, h, HEAD_DIM, t), BF16),
        grid=(bsz, h // heads, t // tq),
        in_specs=[
            pl.BlockSpec((None, heads, wa, tq), lambda b, hh, i: (b, hh, 0, i)),
            pl.BlockSpec((None, heads, t, wa), lambda b, hh, i: (b, hh, 0, 0)),
            pl.BlockSpec((None, heads, HEAD_DIM + ONES_ROWS, t), lambda b, hh, i: (b, hh, 0, 0)),
        ],
        out_specs=pl.BlockSpec((None, heads, HEAD_DIM, tq), lambda b, hh, i: (b, hh, 0, i)),
        compiler_params=_cparams(("parallel", "parallel", "arbitrary")),
        name="fox_attention",
    )(qt, kaug, vt)


def _merge_kernel(x_ref, yp_ref, oc_ref, os_ref, ow_ref, yf_ref, zs_ref, gm_ref, ex_ref,
                  wup_ref, wo_ref, g_ref, b_ref, o_ref, *, alpha):
    mix_w = yp_ref.shape[-1]
    d = x_ref.shape[-1]
    sg = jax.nn.sigmoid(zs_ref[...])
    hi, mid, lo = _split3(sg)
    ex = ex_ref[...]
    ge = (jnp.dot(hi, ex, preferred_element_type=F32) + jnp.dot(mid, ex, preferred_element_type=F32)
          + jnp.dot(lo, ex, preferred_element_type=F32))
    yn = (ge[:, 0:mix_w] * oc_ref[...].astype(F32) + ge[:, mix_w:2 * mix_w] * os_ref[...].astype(F32)
          + ge[:, 2 * mix_w:3 * mix_w] * ow_ref[...].astype(F32))
    ys = (yp_ref[...], yn.astype(BF16), yf_ref[...])
    mix = None
    for n in range(N_BRANCH):
        up = jnp.dot(ys[n], wup_ref[n], preferred_element_type=F32)
        term = gm_ref[:, n * d:(n + 1) * d].astype(F32) * up
        mix = term if mix is None else mix + term
    h = jnp.dot(mix.astype(BF16), wo_ref[...], preferred_element_type=F32)
    _store_rows_tiled(o_ref, _layer_norm(alpha * x_ref[...] + h, g_ref[...], b_ref[...]))


def _merge(x, yp, oc, osl, ow, yf, zs, gm, expand, wup, wo, g, b, alpha, tm=512):
    n, d = x.shape
    mix_w = yp.shape[1]
    tok = lambda width: pl.BlockSpec((tm, width), lambda i: (i, 0))
    return pl.pallas_call(
        functools.partial(_merge_kernel, alpha=alpha),
        out_shape=jax.ShapeDtypeStruct((n * SUBLANES, d // SUBLANES), F32),
        grid=(n // tm,),
        in_specs=[
            tok(d), tok(mix_w), tok(mix_w), tok(mix_w), tok(mix_w), tok(mix_w),
            pl.BlockSpec((tm, LANES), lambda i: (i, 0)),
            tok(N_BRANCH * d),
            pl.BlockSpec((LANES, N_BRANCH * mix_w), lambda i: (0, 0)),
            pl.BlockSpec((N_BRANCH, mix_w, d), lambda i: (0, 0, 0)),
            pl.BlockSpec((d, d), lambda i: (0, 0)),
            pl.BlockSpec((1, d), lambda i: (0, 0)),
            pl.BlockSpec((1, d), lambda i: (0, 0)),
        ],
        out_specs=pl.BlockSpec((tm * SUBLANES, d // SUBLANES), lambda i: (i, 0)),
        compiler_params=_cparams(("parallel",)),
        name="merge_outproj_ln",
    )(x, yp, oc, osl, ow, yf, zs, gm, expand, wup, wo, g.reshape(1, d), b.reshape(1, d))


def _router_kernel(x_ref, w_ref, b_ref, idx_ref, gate_ref):
    x = _load_rows_tiled(x_ref, idx_ref.shape[0])
    logits = jnp.dot(x, w_ref[...], preferred_element_type=F32,
                     precision=lax.Precision.HIGHEST) + b_ref[...]
    tm, lanes = logits.shape
    lane = lax.broadcasted_iota(jnp.int32, (tm, lanes), 1)
    work = jnp.where(lane < N_EXPERTS, logits, -jnp.inf)
    idx_out = jnp.zeros((tm, lanes), jnp.int32)
    val_out = jnp.zeros((tm, lanes), F32)
    top = None
    denom = jnp.zeros((tm, 1), F32)
    for k in range(TOP_K):
        mx = jnp.max(work, axis=-1, keepdims=True)
        first = jnp.min(jnp.where(work == mx, lane, lanes), axis=-1, keepdims=True)
        if top is None:
            top = mx
        e = jnp.exp(mx - top)
        denom = denom + e
        idx_out = jnp.where(lane == k, first, idx_out)
        val_out = jnp.where(lane == k, e, val_out)
        work = jnp.where(lane == first, -jnp.inf, work)
    idx_ref[...] = idx_out
    gate_ref[...] = val_out / denom


def _router(x, w, b, tm=512):
    n, d = x.shape[0] // SUBLANES, x.shape[1] * SUBLANES
    wp = jnp.zeros((d, LANES), F32).at[:, :N_EXPERTS].set(w)
    bp = jnp.zeros((1, LANES), F32).at[0, :N_EXPERTS].set(b)
    return pl.pallas_call(
        _router_kernel,
        out_shape=(jax.ShapeDtypeStruct((n, LANES), jnp.int32), jax.ShapeDtypeStruct((n, LANES), F32)),
        grid=(n // tm,),
        in_specs=[
            pl.BlockSpec((tm * SUBLANES, d // SUBLANES), lambda i: (i, 0)),
            pl.BlockSpec((d, LANES), lambda i: (0, 0)),
            pl.BlockSpec((1, LANES), lambda i: (0, 0)),
        ],
        out_specs=(pl.BlockSpec((tm, LANES), lambda i: (i, 0)), pl.BlockSpec((tm, LANES), lambda i: (i, 0))),
        compiler_params=_cparams(("parallel",)),
        name="moe_router",
    )(x, wp, bp)


def _expert_kernel(be_ref, nu_ref, tok_cur_ref, tok_next_ref, x_hbm, w1_ref, b1_ref, w2_ref, b2_ref, o_ref,
                   xbuf, sems):
    i = pl.program_id(0)
    n_used = nu_ref[0]
    slot = lax.rem(i, 2)

    def fetch(tok_ref, s):
        def body(r, c):
            _row_tile_copy(x_hbm, tok_ref[0, r], xbuf.at[s], r, sems.at[s]).start()
            return c
        lax.fori_loop(0, MOE_BLOCK, body, 0)

    @pl.when((i == 0) & (n_used > 0))
    def _():
        fetch(tok_cur_ref, 0)

    @pl.when(i + 1 < n_used)
    def _():
        fetch(tok_next_ref, 1 - slot)

    @pl.when(i < n_used)
    def _():
        def body(r, c):
            _row_tile_copy(x_hbm, 0, xbuf.at[slot], r, sems.at[slot]).wait()
            return c
        lax.fori_loop(0, MOE_BLOCK, body, 0)
        dff = w2_ref.shape[0]
        x = _load_rows_tiled(xbuf.at[slot], MOE_BLOCK).astype(BF16)
        h = jnp.dot(x, w1_ref[...], preferred_element_type=F32) + b1_ref[...]
        gate = jnp.minimum(h[:, 0:dff], SWIGLU_LIMIT)
        upv = jnp.clip(h[:, dff:2 * dff], -SWIGLU_LIMIT, SWIGLU_LIMIT)
        act = (upv + 1.0) * (gate * jax.nn.sigmoid(SWIGLU_ALPHA * gate))
        _store_rows_tiled(o_ref, jnp.dot(act.astype(BF16), w2_ref[...], preferred_element_type=F32) + b2_ref[...])

    @pl.when(i >= n_used)
    def _():
        o_ref[...] = jnp.zeros_like(o_ref)


def _expert_ffn(x, buf_tok, blk_e, n_used, w1, b1, w2, b2):
    n_blk = buf_tok.shape[0]
    dl = x.shape[1]
    d = dl * SUBLANES
    dff = w2.shape[1]
    grid_spec = pltpu.PrefetchScalarGridSpec(
        num_scalar_prefetch=2,
        grid=(n_blk,),
        in_specs=[
            pl.BlockSpec((None, 1, MOE_BLOCK), lambda i, be, nu: (i, 0, 0), memory_space=pltpu.SMEM),
            pl.BlockSpec((None, 1, MOE_BLOCK), lambda i, be, nu: (jnp.minimum(i + 1, n_blk - 1), 0, 0),
                         memory_space=pltpu.SMEM),
            pl.BlockSpec(memory_space=pl.ANY),
            pl.BlockSpec((None, d, 2 * dff), lambda i, be, nu: (be[i], 0, 0)),
            pl.BlockSpec((None, 1, 2 * dff), lambda i, be, nu: (be[i], 0, 0)),
            pl.BlockSpec((None, dff, d), lambda i, be, nu: (be[i], 0, 0)),
            pl.BlockSpec((None, 1, d), lambda i, be, nu: (be[i], 0, 0)),
        ],
        out_specs=pl.BlockSpec((MOE_BLOCK * SUBLANES, dl), lambda i, be, nu: (i, 0)),
        scratch_shapes=[pltpu.VMEM((2, MOE_BLOCK * SUBLANES, dl), F32), pltpu.SemaphoreType.DMA((2,))],
    )
    return pl.pallas_call(
        _expert_kernel,
        out_shape=jax.ShapeDtypeStruct((n_blk * MOE_BLOCK * SUBLANES, dl), F32),
        grid_spec=grid_spec,
        compiler_params=_cparams(("arbitrary",)),
        name="moe_experts",
    )(blk_e, n_used, buf_tok, buf_tok, x, w1, b1.reshape(N_EXPERTS, 1, 2 * dff), w2, b2.reshape(N_EXPERTS, 1, d))


def _combine_kernel(row_cur_ref, row_next_ref, x_ref, gate_ref, yb_hbm, g_ref, b_ref, o_ref, buf, sems,
                    *, tm, alpha):
    i = pl.program_id(0)
    n_tiles = pl.num_programs(0)
    slot = lax.rem(i, 2)

    def fetch(row_ref, s):
        def body(r, c):
            for k in range(TOP_K):
                _row_tile_copy(yb_hbm, row_ref[0, r * TOP_K + k], buf.at[s, k], r, sems.at[s]).start()
            return c
        lax.fori_loop(0, tm, body, 0)

    @pl.when(i == 0)
    def _():
        fetch(row_cur_ref, 0)

    @pl.when(i + 1 < n_tiles)
    def _():
        fetch(row_next_ref, 1 - slot)

    def drain(r, c):
        for k in range(TOP_K):
            _row_tile_copy(yb_hbm, 0, buf.at[slot, k], r, sems.at[slot]).wait()
        return c
    lax.fori_loop(0, tm, drain, 0)
    gates = gate_ref[...]
    y = gates[:, 0:1] * _load_rows_tiled(buf.at[slot, 0], tm)
    for k in range(1, TOP_K):
        y = y + gates[:, k:k + 1] * _load_rows_tiled(buf.at[slot, k], tm)
    o_ref[...] = _layer_norm(alpha * _load_rows_tiled(x_ref, tm) + y, g_ref[...], b_ref[...])


def _combine(x, gates, row_of, yb, g, b, alpha, tm=256):
    dl = x.shape[1]
    n, d = x.shape[0] // SUBLANES, dl * SUBLANES
    n_tiles = n // tm
    rows = row_of.reshape(n_tiles, 1, tm * TOP_K)
    return pl.pallas_call(
        functools.partial(_combine_kernel, tm=tm, alpha=alpha),
        out_shape=jax.ShapeDtypeStruct((n, d), F32),
        grid=(n_tiles,),
        in_specs=[
            pl.BlockSpec((None, 1, tm * TOP_K), lambda i: (i, 0, 0), memory_space=pltpu.SMEM),
            pl.BlockSpec((None, 1, tm * TOP_K), lambda i: (jnp.minimum(i + 1, n_tiles - 1), 0, 0),
                         memory_space=pltpu.SMEM),
            pl.BlockSpec((tm * SUBLANES, dl), lambda i: (i, 0)),
            pl.BlockSpec((tm, LANES), lambda i: (i, 0)),
            pl.BlockSpec(memory_space=pl.ANY),
            pl.BlockSpec((1, d), lambda i: (0, 0)),
            pl.BlockSpec((1, d), lambda i: (0, 0)),
        ],
        out_specs=pl.BlockSpec((tm, d), lambda i: (i, 0)),
        scratch_shapes=[pltpu.VMEM((2, TOP_K, tm * SUBLANES, dl), F32), pltpu.SemaphoreType.DMA((2,))],
        compiler_params=_cparams(("arbitrary",)),
        name="moe_combine_ln",
    )(rows, rows, x, gates, yb, g.reshape(1, d), b.reshape(1, d))


def _head_major(z, n_heads):
    bsz, t, _ = z.shape
    return z.reshape(bsz, t, n_heads, HEAD_DIM).transpose(0, 2, 1, 3)


def _mixer_layer(x, w_in, b_in, pool_w, pool_b, pool_scale, cmp_pos, cmp_w1, cmp_b1, cmp_w2, cmp_b2,
                 w_up, w_o, ln_g, ln_b, alpha):
    bsz, t, d = x.shape
    n = bsz * t
    mix_w = d // 2
    g, hpg = NSA_KV_GROUPS, NSA_HPG
    nq_w = NSA_HEADS * HEAD_DIM
    nkv_w = N_BRANCH * 2 * g * HEAD_DIM
    nfox_w = 3 * FOX_HEADS * HEAD_DIM
    sizes = (mix_w, nq_w, nkv_w, N_BRANCH * NSA_HEADS, nfox_w, FOX_HEADS, N_BRANCH * d)
    o_pool, o_q, o_kv, o_g, o_fox, o_f, o_gm = (int(v) for v in np.cumsum((0,) + sizes[:-1]))
    n_gate = N_BRANCH * NSA_HEADS

    def cols(a, lo, width):
        return lax.slice_in_dim(a, lo, lo + width, axis=-1)

    qs = SCALE * LOG2E
    pad_small = 2 * LANES - n_gate - FOX_HEADS
    w_a = jnp.concatenate([cols(w_in, o_pool, mix_w), cols(w_in, o_g, n_gate), cols(w_in, o_f, FOX_HEADS),
                           jnp.zeros((d, pad_small), F32)], axis=1).astype(BF16)
    b_a = jnp.concatenate([cols(b_in, o_pool, mix_w), cols(b_in, o_g, n_gate), cols(b_in, o_f, FOX_HEADS),
                           jnp.zeros((pad_small,), F32)])
    w_b = jnp.concatenate([cols(w_in, o_q, nq_w) * qs, cols(w_in, o_kv, nkv_w),
                           cols(w_in, o_fox, FOX_HEADS * HEAD_DIM) * qs,
                           cols(w_in, o_fox + FOX_HEADS * HEAD_DIM, 2 * FOX_HEADS * HEAD_DIM)], axis=1).astype(BF16)
    b_b = jnp.concatenate([cols(b_in, o_q, nq_w) * qs, cols(b_in, o_kv, nkv_w),
                           cols(b_in, o_fox, FOX_HEADS * HEAD_DIM) * qs,
                           cols(b_in, o_fox + FOX_HEADS * HEAD_DIM, 2 * FOX_HEADS * HEAD_DIM)])
    w_c = cols(w_in, o_gm, N_BRANCH * d).astype(BF16)
    b_c = cols(b_in, o_gm, N_BRANCH * d)

    xf = x.reshape(n, d)
    z_a = _matmul_bias(xf, w_a, b_a, F32, tn=mix_w + 2 * LANES)
    z_b = _matmul_bias(xf, w_b, b_b, BF16, tn=256)
    gm = _matmul_bias(xf, w_c, b_c, BF16, act="sigmoid")

    z_a = z_a.reshape(bsz, t, -1)
    z_b = z_b.reshape(bsz, t, -1)
    u_pool = cols(z_a, 0, mix_w)
    z_small = cols(z_a, mix_w, 2 * LANES)

    y_pool = _pool_mixer(u_pool, pool_w, pool_b, pool_scale).reshape(n, mix_w)

    def feature_major(z, n_heads):
        return z.reshape(bsz, t, n_heads, HEAD_DIM).transpose(0, 2, 3, 1)

    def with_ones(vt):
        return jnp.concatenate([vt, jnp.ones(vt.shape[:-2] + (ONES_ROWS, t), BF16)], axis=-2)

    def token_major(ot):
        return ot.reshape(bsz, -1, t).transpose(0, 2, 1).reshape(n, -1)

    qt_nsa = feature_major(cols(z_b, 0, nq_w), NSA_HEADS).reshape(bsz, g, hpg, HEAD_DIM, t)
    kv = _head_major(cols(z_b, nq_w, nkv_w), N_BRANCH * 2 * g).reshape(bsz, N_BRANCH, 2, g, t, HEAD_DIM)
    n_chunk = t // CMP_STRIDE
    a_cmp = kv[:, 0].transpose(1, 0, 2, 3, 4).reshape(2, bsz * g, n_chunk, CMP_STRIDE * HEAD_DIM)
    kvc = _compress(a_cmp, cmp_pos.reshape(2, CMP_LEN * HEAD_DIM), cmp_w1, cmp_b1, cmp_w2, cmp_b2)
    kvc = kvc.reshape(2, bsz, g, n_chunk, HEAD_DIM)
    n_s = t // SEL_LEN
    c_start = np.arange(n_chunk) * CMP_STRIDE
    s_start = np.arange(n_s) * SEL_LEN
    ov = (c_start[:, None] < s_start[None, :] + SEL_LEN) & (c_start[:, None] + CMP_LEN > s_start[None, :])
    ov[n_chunk - (CMP_LEN // CMP_STRIDE) + 1:] = False
    ovt = jnp.asarray(ov.T.astype(np.float32), BF16)
    o_cmp, qaug = _nsa_cmp(qt_nsa, kvc[0], kvc[1].transpose(0, 1, 3, 2), ovt)
    onehot = jnp.asarray((np.arange(t)[:, None] // SEL_LEN == np.arange(n_s)[None, :]).astype(np.float32), BF16)
    kaug = jnp.concatenate([jnp.broadcast_to(onehot, (bsz, g, t, n_s)), kv[:, 1, 0]], axis=-1)
    o_slc = _nsa_slc(qaug, kaug, with_ones(kv[:, 1, 1].transpose(0, 1, 3, 2)))
    o_win = _nsa_win(qt_nsa, kv[:, 2, 0], with_ones(kv[:, 2, 1].transpose(0, 1, 3, 2)))

    cum = _fox_prep(z_small)
    cq = cols(cum, n_gate, FOX_HEADS).transpose(0, 3, 2, 1)
    fox = _head_major(cols(z_b, nq_w + nkv_w, nfox_w), 3 * FOX_HEADS).reshape(bsz, 3, FOX_HEADS, t, HEAD_DIM)
    ones = jnp.ones((bsz, FOX_HEADS, t, 3), BF16)
    zpad = jnp.zeros((bsz, FOX_HEADS, t, LANES - HEAD_DIM - 6), BF16)
    q_fox = jnp.concatenate([fox[:, 0], cq, ones, zpad], axis=-1)
    k_fox = jnp.concatenate([fox[:, 1], ones, -cq, zpad], axis=-1)
    y_fox = _fox_attention(q_fox.transpose(0, 1, 3, 2), k_fox, with_ones(fox[:, 2].transpose(0, 1, 3, 2)))

    hsel = np.arange(NSA_HEADS * HEAD_DIM) // HEAD_DIM
    expand = np.zeros((LANES, N_BRANCH * mix_w), np.float32)
    for br in range(N_BRANCH):
        expand[hsel * N_BRANCH + br, br * mix_w + np.arange(mix_w)] = 1.0
    return _merge(xf, y_pool, token_major(o_cmp), token_major(o_slc), token_major(o_win),
                  token_major(y_fox), cols(z_small, 0, LANES).reshape(n, LANES), gm,
                  jnp.asarray(expand, BF16), w_up.astype(BF16), w_o.astype(BF16), ln_g, ln_b, alpha)


def _moe_layer(x, router_w, router_b, w1, b1, w2, b2, ln_g, ln_b, alpha):
    n = x.shape[0] // SUBLANES
    nk = n * TOP_K
    idx_p, gate_p = _router(x, router_w, router_b)
    e_flat = idx_p[:, :TOP_K].reshape(nk)
    onehot = (e_flat[:, None] == jnp.arange(N_EXPERTS)[None, :]).astype(jnp.int32)
    incl = jnp.cumsum(onehot, axis=0)
    counts = incl[-1]
    rank = jnp.sum((incl - onehot) * onehot, axis=1)
    padded = (counts + MOE_BLOCK - 1) // MOE_BLOCK * MOE_BLOCK
    pend = jnp.cumsum(padded)
    pstart = pend - padded
    dest = pstart[e_flat] + rank
    n_blk = (nk + MOE_BLOCK - 1) // MOE_BLOCK + N_EXPERTS
    p = n_blk * MOE_BLOCK
    buf_tok = jnp.zeros((p,), jnp.int32).at[dest].set(jnp.arange(nk, dtype=jnp.int32) // TOP_K)
    blk_e = jnp.minimum(jnp.sum(jnp.arange(n_blk)[:, None] * MOE_BLOCK >= pend[None, :], axis=1),
                        N_EXPERTS - 1).astype(jnp.int32)
    n_used = (pend[-1] // MOE_BLOCK).astype(jnp.int32).reshape(1)

    yb = _expert_ffn(x, buf_tok.reshape(n_blk, 1, MOE_BLOCK), blk_e, n_used, w1.astype(BF16), b1,
                     w2.astype(BF16), b2)
    return _combine(x, gate_p, dest.astype(jnp.int32), yb, ln_g, ln_b, alpha)


def kernel(x, w_in, b_in, pool_w, pool_b, pool_scale, cmp_pos, cmp_w1, cmp_b1, cmp_w2, cmp_b2,
           w_up, w_o, ln1_g, ln1_b, router_w, router_b, moe_w1, moe_b1, moe_w2, moe_b2, ln2_g, ln2_b):
    depth = w_in.shape[0]
    alpha = (2 * depth) ** 0.25
    bsz, t, d = x.shape
    for l in range(depth):
        x1 = _mixer_layer(x, w_in[l], b_in[l], pool_w[l], pool_b[l], pool_scale[l], cmp_pos[l], cmp_w1[l],
                          cmp_b1[l], cmp_w2[l], cmp_b2[l], w_up[l], w_o[l], ln1_g[l], ln1_b[l], alpha)
        x2 = _moe_layer(x1, router_w[l], router_b[l], moe_w1[l], moe_b1[l], moe_w2[l], moe_b2[l],
                        ln2_g[l], ln2_b[l], alpha)
        x = x2.reshape(bsz, t, d)
    return x
```

```python
import functools

import numpy as np
import jax
import jax.numpy as jnp
from jax import lax
from jax.experimental import pallas as pl
from jax.experimental.pallas import tpu as pltpu

F32 = jnp.float32
BF16 = jnp.bfloat16

HEAD_DIM = 64
POOL_WINDOWS = (2, 4, 8, 16)
POOL_GC = 128
POOL_HALO = 16
NSA_HEADS = 8
NSA_KV_GROUPS = 2
NSA_HPG = NSA_HEADS // NSA_KV_GROUPS
N_BRANCH = 3
CMP_LEN = 32
CMP_STRIDE = 16
SEL_LEN = 64
N_SEL = 16
WINDOW = 512
FOX_HEADS = 8
Q_BLOCK = 128
N_EXPERTS = 32
TOP_K = 4
SWIGLU_LIMIT = 7.0
SWIGLU_ALPHA = 1.702
MOE_BLOCK = 256
LN_EPS = 1e-5
NEG_INF = -1e30
SCALE = HEAD_DIM ** -0.5
MASK_BIG = 2.0 ** 100
LOG2E = 1.4426950408889634
ONES_ROWS = 8
LANES = 128
SUBLANES = 8
VMEM_LIMIT = 48 * 1024 * 1024
EXPERT_VMEM_LIMIT = 58 * 1024 * 1024
DMA_ISSUE_UNROLL = 8


def _cparams(sem):
    return pltpu.CompilerParams(dimension_semantics=sem, vmem_limit_bytes=VMEM_LIMIT)


def _split3(x):
    hi = x.astype(BF16)
    r1 = x - hi.astype(F32)
    mid = r1.astype(BF16)
    lo = (r1 - mid.astype(F32)).astype(BF16)
    return hi, mid, lo


def _layer_norm(r, g, b):
    mu = jnp.mean(r, axis=-1, keepdims=True)
    var = jnp.mean(jnp.square(r - mu), axis=-1, keepdims=True)
    return (r - mu) * lax.rsqrt(var + LN_EPS) * g + b


def _load_rows_tiled(ref, rows):
    return jnp.concatenate([ref[pl.ds(s, rows, stride=SUBLANES), :] for s in range(SUBLANES)], axis=1)


def _store_rows_tiled(ref, val):
    rows = val.shape[0]
    for s in range(SUBLANES):
        ref[pl.ds(s, rows, stride=SUBLANES), :] = val[:, s * LANES:(s + 1) * LANES]


def _row_tile_copy(src, src_row, dst, dst_row, sem):
    return pltpu.make_async_copy(src.at[pl.ds(pl.multiple_of(src_row * SUBLANES, SUBLANES), SUBLANES), :],
                                 dst.at[pl.ds(pl.multiple_of(dst_row * SUBLANES, SUBLANES), SUBLANES), :], sem)


def _mm_kernel(x_ref, w_ref, b_ref, o_ref, *, act):
    acc = jnp.dot(x_ref[...].astype(BF16), w_ref[...], preferred_element_type=F32) + b_ref[...]
    if act == "sigmoid":
        acc = jax.nn.sigmoid(acc)
    o_ref[...] = acc.astype(o_ref.dtype)


def _matmul_bias(x, w, b, out_dtype, act=None, tm=1024, tn=512):
    m, k = x.shape
    n = w.shape[1]
    tn = min(tn, n)
    assert m % tm == 0 and n % tn == 0
    return pl.pallas_call(
        functools.partial(_mm_kernel, act=act),
        out_shape=jax.ShapeDtypeStruct((m, n), out_dtype),
        grid=(m // tm, n // tn),
        in_specs=[
            pl.BlockSpec((tm, k), lambda i, j: (i, 0)),
            pl.BlockSpec((k, tn), lambda i, j: (0, j)),
            pl.BlockSpec((1, tn), lambda i, j: (0, j)),
        ],
        out_specs=pl.BlockSpec((tm, tn), lambda i, j: (i, j)),
        compiler_params=_cparams(("parallel", "arbitrary")),
        name="in_proj",
    )(x, w, b.reshape(1, n))


def _pool_kernel(prev_ref, cur_ref, w_ref, b_ref, sc_ref, o_ref, ext_ref, *, tile):
    i = pl.program_id(1)
    ext_ref[0:POOL_HALO, :] = jnp.where(i > 0, prev_ref[...], 0.0)
    ext_ref[POOL_HALO:POOL_HALO + tile, :] = cur_ref[...]
    t_idx = i * tile + lax.broadcasted_iota(jnp.int32, (tile, 1), 0)
    for gi, win in enumerate(POOL_WINDOWS):
        cols = slice(gi * POOL_GC, (gi + 1) * POOL_GC)
        u = ext_ref[POOL_HALO:POOL_HALO + tile, cols]
        wsum = u
        for j in range(1, win):
            wsum = wsum + ext_ref[POOL_HALO - j:POOL_HALO - j + tile, cols]
        cnt = jnp.minimum(t_idx + 1, win).astype(F32)
        d = wsum / cnt - u
        y = jnp.dot(d.astype(BF16), w_ref[gi], preferred_element_type=F32) + b_ref[gi]
        o_ref[:, cols] = (y * sc_ref[:, cols]).astype(o_ref.dtype)


def _pool_mixer(u, w, b, scale, tile=512):
    bsz, t, c = u.shape
    return pl.pallas_call(
        functools.partial(_pool_kernel, tile=tile),
        out_shape=jax.ShapeDtypeStruct((bsz, t, c), BF16),
        grid=(bsz, t // tile),
        in_specs=[
            pl.BlockSpec((None, POOL_HALO, c),
                         lambda bb, i: (bb, jnp.maximum(i * (tile // POOL_HALO) - 1, 0), 0)),
            pl.BlockSpec((None, tile, c), lambda bb, i: (bb, i, 0)),
            pl.BlockSpec((len(POOL_WINDOWS), POOL_GC, POOL_GC), lambda bb, i: (0, 0, 0)),
            pl.BlockSpec((len(POOL_WINDOWS), 1, POOL_GC), lambda bb, i: (0, 0, 0)),
            pl.BlockSpec((1, c), lambda bb, i: (0, 0)),
        ],
        out_specs=pl.BlockSpec((None, tile, c), lambda bb, i: (bb, i, 0)),
        scratch_shapes=[pltpu.VMEM((POOL_HALO + tile, c), F32)],
        compiler_params=_cparams(("parallel", "arbitrary")),
        name="pool_mixer",
    )(u, u, w.astype(BF16), b.reshape(len(POOL_WINDOWS), 1, POOL_GC), scale.reshape(1, c))


def _compress_kernel(a_ref, pos_ref, w1_ref, b1_ref, w2_ref, b2_ref, o_ref):
    half = CMP_STRIDE * HEAD_DIM
    a = a_ref[...].astype(F32)
    n_chunk = a.shape[0]
    top = (a + pos_ref[:, 0:half]).astype(BF16)
    bot = (a + pos_ref[:, half:2 * half]).astype(BF16)
    p1 = jnp.dot(top, w1_ref[0:half, :], preferred_element_type=F32)
    p2 = jnp.dot(bot, w1_ref[half:2 * half, :], preferred_element_type=F32)
    h = p1 + pltpu.roll(p2, n_chunk - 1, axis=0) + b1_ref[...]
    h = jax.nn.gelu(h)
    o = jnp.dot(h.astype(BF16), w2_ref[...], preferred_element_type=F32) + b2_ref[...]
    o_ref[...] = o.astype(o_ref.dtype)


def _compress(a, pos, w1, b1, w2, b2):
    _, bg, n_chunk, width = a.shape
    hid = w1.shape[-1]
    return pl.pallas_call(
        _compress_kernel,
        out_shape=jax.ShapeDtypeStruct((2, bg, n_chunk, HEAD_DIM), BF16),
        grid=(2, bg),
        in_specs=[
            pl.BlockSpec((None, None, n_chunk, width), lambda s, j: (s, j, 0, 0)),
            pl.BlockSpec((None, 1, 2 * width), lambda s, j: (s, 0, 0)),
            pl.BlockSpec((None, 2 * width, hid), lambda s, j: (s, 0, 0)),
            pl.BlockSpec((None, 1, hid), lambda s, j: (s, 0, 0)),
            pl.BlockSpec((None, hid, HEAD_DIM), lambda s, j: (s, 0, 0)),
            pl.BlockSpec((None, 1, HEAD_DIM), lambda s, j: (s, 0, 0)),
        ],
        out_specs=pl.BlockSpec((None, None, n_chunk, HEAD_DIM), lambda s, j: (s, j, 0, 0)),
        compiler_params=_cparams(("arbitrary", "arbitrary")),
        name="nsa_compress",
    )(a, pos.reshape(2, 1, 2 * width), w1.astype(BF16), b1.reshape(2, 1, hid),
      w2.astype(BF16), b2.reshape(2, 1, HEAD_DIM))


def _flash_steps_t(qts, ks, vts, carries, mask):
    scores = [jnp.dot(k, qt, preferred_element_type=F32) for k, qt in zip(ks, qts)]
    if mask is not None:
        masks = mask if isinstance(mask, (list, tuple)) else [mask] * len(scores)
        scores = [jnp.where(mk, s, NEG_INF) for mk, s in zip(masks, scores)]
    probs, stats = [], []
    for s, (m, _) in zip(scores, carries):
        m_new = jnp.maximum(m, jnp.max(s, axis=0, keepdims=True))
        probs.append(jnp.exp2(s - m_new).astype(BF16))
        stats.append((m_new, jnp.exp2(m - m_new)))
    return tuple((m_new, alpha * acc + jnp.dot(vt, p, preferred_element_type=F32))
                 for vt, p, (m_new, alpha), (_, acc) in zip(vts, probs, stats, carries))


def _flash_init_t(cols):
    return (jnp.full((1, cols), NEG_INF, F32), jnp.zeros((HEAD_DIM + ONES_ROWS, cols), F32))


def _flash_finish_t(acc):
    return acc[0:HEAD_DIM] / acc[HEAD_DIM:HEAD_DIM + 1]


def _store_heads(o_ref, o, lead=()):
    for h in range(NSA_HPG):
        o_ref[lead + (h,)] = o[:, h * Q_BLOCK:(h + 1) * Q_BLOCK].astype(o_ref.dtype)


def _nsa_cmp_kernel(qt_ref, kc_ref, vct_ref, ovt_ref, ocmp_ref, qaug_ref):
    groups = range(qt_ref.shape[0])
    cols = NSA_HPG * Q_BLOCK
    t0 = pl.program_id(1) * Q_BLOCK
    n_c = kc_ref.shape[1]
    n_s = ovt_ref.shape[0]
    qts = [jnp.concatenate([qt_ref[g, h] for h in range(NSA_HPG)], axis=1) for g in groups]
    tq = t0 + (lax.broadcasted_iota(jnp.int32, (n_c, cols), 1) & (Q_BLOCK - 1))
    c_end = lax.broadcasted_iota(jnp.int32, (n_c, cols), 0) * CMP_STRIDE + (CMP_LEN - 1)
    mask = c_end <= tq
    ps = []
    for g in groups:
        sm = jnp.where(mask, jnp.dot(kc_ref[g], qts[g], preferred_element_type=F32), NEG_INF)
        e = jnp.where(mask, jnp.exp2(sm - jnp.max(sm, axis=0, keepdims=True)), 0.0)
        l = jnp.sum(e, axis=0, keepdims=True)
        ps.append(e / jnp.where(l > 0.0, l, 1.0))
    for g in groups:
        _store_heads(ocmp_ref, jnp.dot(vct_ref[g], ps[g].astype(BF16), preferred_element_type=F32), (g,))

    ovt = ovt_ref[...]
    blk = lax.broadcasted_iota(jnp.int32, (n_s, Q_BLOCK), 0)
    tqq = t0 + lax.broadcasted_iota(jnp.int32, (n_s, Q_BLOCK), 1)
    future = blk * SEL_LEN > tqq
    cur = lax.shift_right_logical(tqq, 6)
    forced = (blk == 0) | (blk == cur) | (blk == cur - 1)
    scores = []
    for g in groups:
        psum = ps[g][:, 0:Q_BLOCK]
        for h in range(1, NSA_HPG):
            psum = psum + ps[g][:, h * Q_BLOCK:(h + 1) * Q_BLOCK]
        imp = None
        for part in _split3(psum):
            term = jnp.dot(ovt, part, preferred_element_type=F32)
            imp = term if imp is None else imp + term
        scores.append(jnp.where(future, -1.0, jnp.where(forced, 1e6, imp)))
    sels = [jnp.zeros((n_s, Q_BLOCK), F32) for _ in groups]
    for _ in range(min(N_SEL, n_s)):
        for g in groups:
            mx = jnp.max(scores[g], axis=0, keepdims=True)
            first = jnp.min(jnp.where(scores[g] == mx, blk, n_s), axis=0, keepdims=True)
            pick = blk == first
            sels[g] = jnp.where(pick, 1.0, sels[g])
            scores[g] = jnp.where(pick, -2.0, scores[g])
    for g in groups:
        sel = jnp.where(future, 0.0, sels[g])
        bias = ((sel - 1.0) * MASK_BIG).astype(BF16)
        for h in range(NSA_HPG):
            qaug_ref[g, 0:n_s, h * Q_BLOCK:(h + 1) * Q_BLOCK] = bias
        qaug_ref[g, n_s:n_s + HEAD_DIM, :] = qts[g]


def _nsa_cmp(qt, kc, vct, ovt):
    bsz, g, hpg, hd, t = qt.shape
    n_c = kc.shape[2]
    n_s = ovt.shape[0]
    nq = t // Q_BLOCK
    cols = hpg * Q_BLOCK
    return pl.pallas_call(
        _nsa_cmp_kernel,
        out_shape=(jax.ShapeDtypeStruct((bsz, g, hpg, hd, t), BF16),
                   jax.ShapeDtypeStruct((bsz, g, nq, n_s + hd, cols), BF16)),
        grid=(bsz, nq),
        in_specs=[
            pl.BlockSpec((None, g, hpg, hd, Q_BLOCK), lambda b, i: (b, 0, 0, 0, i)),
            pl.BlockSpec((None, g, n_c, hd), lambda b, i: (b, 0, 0, 0)),
            pl.BlockSpec((None, g, hd, n_c), lambda b, i: (b, 0, 0, 0)),
            pl.BlockSpec((n_s, n_c), lambda b, i: (0, 0)),
        ],
        out_specs=(
            pl.BlockSpec((None, g, hpg, hd, Q_BLOCK), lambda b, i: (b, 0, 0, 0, i)),
            pl.BlockSpec((None, g, None, n_s + hd, cols), lambda b, i: (b, 0, i, 0, 0)),
        ),
        compiler_params=_cparams(("parallel", "arbitrary")),
        name="nsa_cmp_select",
    )(qt, kc, vct, ovt)


def _nsa_slc_kernel(qaug_ref, kaug_ref, vt_ref, o_ref, *, tk, qb):
    groups = qaug_ref.shape[0]
    cols = NSA_HPG * Q_BLOCK
    t0 = pl.program_id(1) * (qb * Q_BLOCK)
    jd = t0 // tk
    probs = [(g, b) for g in range(groups) for b in range(qb)]

    def chunk(j, carries, mask):
        start = pl.multiple_of(j * tk, tk)
        return _flash_steps_t([qaug_ref[g, b] for g, b in probs],
                              [kaug_ref[g, pl.ds(start, tk), :] for g, _ in probs],
                              [vt_ref[g, :, pl.ds(start, tk)] for g, _ in probs], carries, mask)

    carries = lax.fori_loop(0, jd, lambda j, c: chunk(j, c, None), tuple(_flash_init_t(cols) for _ in probs))
    kpos = jd * tk + lax.broadcasted_iota(jnp.int32, (tk, cols), 0)
    tq = t0 + (lax.broadcasted_iota(jnp.int32, (tk, cols), 1) & (Q_BLOCK - 1))
    causal = [kpos <= tq + b * Q_BLOCK for b in range(qb)]
    carries = chunk(jd, carries, [causal[b] for _, b in probs])
    for (g, b), (_, acc) in zip(probs, carries):
        o = _flash_finish_t(acc)
        for h in range(NSA_HPG):
            o_ref[g, h, :, b * Q_BLOCK:(b + 1) * Q_BLOCK] = o[:, h * Q_BLOCK:(h + 1) * Q_BLOCK].astype(o_ref.dtype)


def _nsa_slc(qaug, kaug, vt, tk=512, qb=2):
    bsz, g, nq, wa, cols = qaug.shape
    t = kaug.shape[2]
    tk = min(tk, t)
    assert tk % (qb * Q_BLOCK) == 0 and nq % qb == 0
    return pl.pallas_call(
        functools.partial(_nsa_slc_kernel, tk=tk, qb=qb),
        out_shape=jax.ShapeDtypeStruct((bsz, g, NSA_HPG, HEAD_DIM, t), BF16),
        grid=(bsz, nq // qb),
        in_specs=[
            pl.BlockSpec((None, g, qb, wa, cols), lambda b, i: (b, 0, i, 0, 0)),
            pl.BlockSpec((None, g, t, wa), lambda b, i: (b, 0, 0, 0)),
            pl.BlockSpec((None, g, HEAD_DIM + ONES_ROWS, t), lambda b, i: (b, 0, 0, 0)),
        ],
        out_specs=pl.BlockSpec((None, g, NSA_HPG, HEAD_DIM, qb * Q_BLOCK), lambda b, i: (b, 0, 0, 0, i)),
        compiler_params=_cparams(("parallel", "arbitrary")),
        name="nsa_selected",
    )(qaug, kaug, vt)


def _nsa_win_kernel(qt_ref, k_ref, vt_ref, o_ref):
    groups = qt_ref.shape[0]
    cols = NSA_HPG * Q_BLOCK
    span = WINDOW + Q_BLOCK
    t0 = pl.program_id(1) * Q_BLOCK
    start = pl.multiple_of(jnp.maximum(t0 - WINDOW, 0), Q_BLOCK)
    dist = (t0 - start) + (lax.broadcasted_iota(jnp.int32, (span, cols), 1) & (Q_BLOCK - 1)) \
        - lax.broadcasted_iota(jnp.int32, (span, cols), 0)
    mask = (dist >= 0) & (dist < WINDOW)
    scores = []
    for g in range(groups):
        qt = jnp.concatenate([qt_ref[g, h] for h in range(NSA_HPG)], axis=1)
        s = jnp.dot(k_ref[g, pl.ds(start, span), :], qt, preferred_element_type=F32)
        scores.append(jnp.where(mask, s, NEG_INF))
    probs = [jnp.exp2(s - jnp.max(s, axis=0, keepdims=True)).astype(BF16) for s in scores]
    for g in range(groups):
        acc = jnp.dot(vt_ref[g, :, pl.ds(start, span)], probs[g], preferred_element_type=F32)
        _store_heads(o_ref, _flash_finish_t(acc), (g,))


def _nsa_win(qt, k, vt):
    bsz, g, hpg, hd, t = qt.shape
    assert t >= WINDOW + Q_BLOCK
    return pl.pallas_call(
        _nsa_win_kernel,
        out_shape=jax.ShapeDtypeStruct((bsz, g, hpg, hd, t), BF16),
        grid=(bsz, t // Q_BLOCK),
        in_specs=[
            pl.BlockSpec((None, g, hpg, hd, Q_BLOCK), lambda b, i: (b, 0, 0, 0, i)),
            pl.BlockSpec((None, g, t, hd), lambda b, i: (b, 0, 0, 0)),
            pl.BlockSpec((None, g, hd + ONES_ROWS, t), lambda b, i: (b, 0, 0, 0)),
        ],
        out_specs=pl.BlockSpec((None, g, hpg, hd, Q_BLOCK), lambda b, i: (b, 0, 0, 0, i)),
        compiler_params=_cparams(("parallel", "arbitrary")),
        name="nsa_window",
    )(qt, k, vt)


def _fox_prep_kernel(z_ref, tri_ref, o_ref, carry_ref):
    @pl.when(pl.program_id(1) == 0)
    def _():
        carry_ref[...] = jnp.zeros_like(carry_ref)

    lf = jax.nn.log_sigmoid(z_ref[...])
    hi, mid, lo = _split3(lf)
    tri = tri_ref[...]
    cum = (jnp.dot(tri, hi, preferred_element_type=F32) + jnp.dot(tri, mid, preferred_element_type=F32)
           + jnp.dot(tri, lo, preferred_element_type=F32)) + carry_ref[...]
    carry_ref[...] = cum[cum.shape[0] - 1:cum.shape[0], :]
    for part, val in enumerate(_split3(cum * LOG2E)):
        o_ref[part] = val


def _fox_prep(z_small, tile=256):
    bsz, t, _ = z_small.shape
    tri = jnp.asarray(np.tril(np.ones((tile, tile), np.float32)), BF16)
    return pl.pallas_call(
        _fox_prep_kernel,
        out_shape=jax.ShapeDtypeStruct((bsz, 3, t, LANES), BF16),
        grid=(bsz, t // tile),
        in_specs=[
            pl.BlockSpec((None, tile, LANES), lambda b, i: (b, i, 0)),
            pl.BlockSpec((tile, tile), lambda b, i: (0, 0)),
        ],
        out_specs=pl.BlockSpec((None, 3, tile, LANES), lambda b, i: (b, 0, i, 0)),
        scratch_shapes=[pltpu.VMEM((1, LANES), F32)],
        compiler_params=_cparams(("parallel", "arbitrary")),
        name="fox_decay_cumsum",
    )(z_small, tri)


def _fox_kernel(qt_ref, k_ref, vt_ref, o_ref, *, tq, heads):
    i = pl.program_id(2)

    def chunk(j, carries, mask):
        start = pl.multiple_of(j * tq, tq)
        return _flash_steps_t([qt_ref[h] for h in range(heads)],
                              [k_ref[h, pl.ds(start, tq), :] for h in range(heads)],
                              [vt_ref[h, :, pl.ds(start, tq)] for h in range(heads)], carries, mask)

    carries = lax.fori_loop(0, i, lambda j, c: chunk(j, c, None), tuple(_flash_init_t(tq) for _ in range(heads)))
    causal = lax.broadcasted_iota(jnp.int32, (tq, tq), 0) <= lax.broadcasted_iota(jnp.int32, (tq, tq), 1)
    carries = chunk(i, carries, causal)
    for h in range(heads):
        o_ref[h] = _flash_finish_t(carries[h][1]).astype(o_ref.dtype)


def _fox_attention(qt, kaug, vt, tq=512, heads=4):
    bsz, h, wa, t = qt.shape
    tq = min(tq, t)
    return pl.pallas_call(
        functools.partial(_fox_kernel, tq=tq, heads=heads),
        out_shape=jax.ShapeDtypeStruct((bsz, h, HEAD_DIM, t), BF16),
        grid=(bsz, h // heads, t // tq),
        in_specs=[
            pl.BlockSpec((None, heads, wa, tq), lambda b, hh, i: (b, hh, 0, i)),
            pl.BlockSpec((None, heads, t, wa), lambda b, hh, i: (b, hh, 0, 0)),
            pl.BlockSpec((None, heads, HEAD_DIM + ONES_ROWS, t), lambda b, hh, i: (b, hh, 0, 0)),
        ],
        out_specs=pl.BlockSpec((None, heads, HEAD_DIM, tq), lambda b, hh, i: (b, hh, 0, i)),
        compiler_params=_cparams(("parallel", "parallel", "arbitrary")),
        name="fox_attention",
    )(qt, kaug, vt)


def _merge_kernel(x_ref, yp_ref, oc_ref, os_ref, ow_ref, yf_ref, zs_ref, gm_ref, ex_ref,
                  wup_ref, wo_ref, g_ref, b_ref, o_ref, *, alpha):
    mix_w = yp_ref.shape[-1]
    d = x_ref.shape[-1]
    sg = jax.nn.sigmoid(zs_ref[...])
    hi, mid, lo = _split3(sg)
    ex = ex_ref[...]
    ge = (jnp.dot(hi, ex, preferred_element_type=F32) + jnp.dot(mid, ex, preferred_element_type=F32)
          + jnp.dot(lo, ex, preferred_element_type=F32))
    yn = (ge[:, 0:mix_w] * oc_ref[...].astype(F32) + ge[:, mix_w:2 * mix_w] * os_ref[...].astype(F32)
          + ge[:, 2 * mix_w:3 * mix_w] * ow_ref[...].astype(F32))
    ys = (yp_ref[...], yn.astype(BF16), yf_ref[...])
    mix = None
    for n in range(N_BRANCH):
        up = jnp.dot(ys[n], wup_ref[n], preferred_element_type=F32)
        term = gm_ref[:, n * d:(n + 1) * d].astype(F32) * up
        mix = term if mix is None else mix + term
    h = jnp.dot(mix.astype(BF16), wo_ref[...], preferred_element_type=F32)
    _store_rows_tiled(o_ref, _layer_norm(alpha * x_ref[...] + h, g_ref[...], b_ref[...]))


def _merge(x, yp, oc, osl, ow, yf, zs, gm, expand, wup, wo, g, b, alpha, tm=512):
    n, d = x.shape
    mix_w = yp.shape[1]
    tok = lambda width: pl.BlockSpec((tm, width), lambda i: (i, 0))
    return pl.pallas_call(
        functools.partial(_merge_kernel, alpha=alpha),
        out_shape=jax.ShapeDtypeStruct((n * SUBLANES, d // SUBLANES), F32),
        grid=(n // tm,),
        in_specs=[
            tok(d), tok(mix_w), tok(mix_w), tok(mix_w), tok(mix_w), tok(mix_w),
            pl.BlockSpec((tm, LANES), lambda i: (i, 0)),
            tok(N_BRANCH * d),
            pl.BlockSpec((LANES, N_BRANCH * mix_w), lambda i: (0, 0)),
            pl.BlockSpec((N_BRANCH, mix_w, d), lambda i: (0, 0, 0)),
            pl.BlockSpec((d, d), lambda i: (0, 0)),
            pl.BlockSpec((1, d), lambda i: (0, 0)),
            pl.BlockSpec((1, d), lambda i: (0, 0)),
        ],
        out_specs=pl.BlockSpec((tm * SUBLANES, d // SUBLANES), lambda i: (i, 0)),
        compiler_params=_cparams(("parallel",)),
        name="merge_outproj_ln",
    )(x, yp, oc, osl, ow, yf, zs, gm, expand, wup, wo, g.reshape(1, d), b.reshape(1, d))


def _router_kernel(x_ref, w_ref, b_ref, idx_ref, gate_ref):
    x = _load_rows_tiled(x_ref, idx_ref.shape[0])
    logits = jnp.dot(x, w_ref[...], preferred_element_type=F32,
                     precision=lax.Precision.HIGHEST) + b_ref[...]
    tm, lanes = logits.shape
    lane = lax.broadcasted_iota(jnp.int32, (tm, lanes), 1)
    work = jnp.where(lane < N_EXPERTS, logits, -jnp.inf)
    idx_out = jnp.zeros((tm, lanes), jnp.int32)
    val_out = jnp.zeros((tm, lanes), F32)
    top = None
    denom = jnp.zeros((tm, 1), F32)
    for k in range(TOP_K):
        mx = jnp.max(work, axis=-1, keepdims=True)
        first = jnp.min(jnp.where(work == mx, lane, lanes), axis=-1, keepdims=True)
        if top is None:
            top = mx
        e = jnp.exp(mx - top)
        denom = denom + e
        idx_out = jnp.where(lane == k, first, idx_out)
        val_out = jnp.where(lane == k, e, val_out)
        work = jnp.where(lane == first, -jnp.inf, work)
    idx_ref[...] = idx_out
    gate_ref[...] = val_out / denom


def _router(x, w, b, tm=512):
    n, d = x.shape[0] // SUBLANES, x.shape[1] * SUBLANES
    wp = jnp.zeros((d, LANES), F32).at[:, :N_EXPERTS].set(w)
    bp = jnp.zeros((1, LANES), F32).at[0, :N_EXPERTS].set(b)
    return pl.pallas_call(
        _router_kernel,
        out_shape=(jax.ShapeDtypeStruct((n, LANES), jnp.int32), jax.ShapeDtypeStruct((n, LANES), F32)),
        grid=(n // tm,),
        in_specs=[
            pl.BlockSpec((tm * SUBLANES, d // SUBLANES), lambda i: (i, 0)),
            pl.BlockSpec((d, LANES), lambda i: (0, 0)),
            pl.BlockSpec((1, LANES), lambda i: (0, 0)),
        ],
        out_specs=(pl.BlockSpec((tm, LANES), lambda i: (i, 0)), pl.BlockSpec((tm, LANES), lambda i: (i, 0))),
        compiler_params=_cparams(("parallel",)),
        name="moe_router",
    )(x, wp, bp)


def _start_row_tiles(src_hbm, idx_ref, n, dst, sem, *, per_row=1):
    def body(r, c):
        for k in range(per_row):
            _row_tile_copy(src_hbm, idx_ref[0, r * per_row + k], dst.at[k] if per_row > 1 else dst, r, sem).start()
        return c
    lax.fori_loop(0, n, body, 0, unroll=DMA_ISSUE_UNROLL)


def _wait_buffer(buf, sem):
    pltpu.make_async_copy(buf, buf, sem).wait()


def _expert_kernel(be_ref, nu_ref, tok_cur_ref, tok_next_ref, x_hbm, w1_ref, b1_ref, w2_ref, b2_ref, o_ref,
                   xbuf, w1_bf, w2_bf, sems):
    i = pl.program_id(0)
    n_used = nu_ref[0]
    slot = lax.rem(i, 2)

    @pl.when((i == 0) & (n_used > 0))
    def _():
        _start_row_tiles(x_hbm, tok_cur_ref, MOE_BLOCK, xbuf.at[0], sems.at[0])

    @pl.when(i + 1 < n_used)
    def _():
        _start_row_tiles(x_hbm, tok_next_ref, MOE_BLOCK, xbuf.at[1 - slot], sems.at[1 - slot])

    @pl.when((i < n_used) & ((i == 0) | (be_ref[i] != be_ref[jnp.maximum(i - 1, 0)])))
    def _():
        w1_bf[...] = w1_ref[...].astype(BF16)
        w2_bf[...] = w2_ref[...].astype(BF16)

    @pl.when(i < n_used)
    def _():
        _wait_buffer(xbuf.at[slot], sems.at[slot])
        dff = w2_ref.shape[0]
        x = _load_rows_tiled(xbuf.at[slot], MOE_BLOCK).astype(BF16)
        h = jnp.dot(x, w1_bf[...], preferred_element_type=F32) + b1_ref[...]
        gate = jnp.minimum(h[:, 0:dff], SWIGLU_LIMIT)
        upv = jnp.clip(h[:, dff:2 * dff], -SWIGLU_LIMIT, SWIGLU_LIMIT)
        act = (upv + 1.0) * (gate * jax.nn.sigmoid(SWIGLU_ALPHA * gate))
        _store_rows_tiled(o_ref, jnp.dot(act.astype(BF16), w2_bf[...], preferred_element_type=F32) + b2_ref[...])

    @pl.when(i >= n_used)
    def _():
        o_ref[...] = jnp.zeros_like(o_ref)


def _expert_ffn(x, buf_tok, blk_e, n_used, w1, b1, w2, b2):
    n_blk = buf_tok.shape[0]
    dl = x.shape[1]
    d = dl * SUBLANES
    dff = w2.shape[1]
    grid_spec = pltpu.PrefetchScalarGridSpec(
        num_scalar_prefetch=2,
        grid=(n_blk,),
        in_specs=[
            pl.BlockSpec((None, 1, MOE_BLOCK), lambda i, be, nu: (i, 0, 0), memory_space=pltpu.SMEM),
            pl.BlockSpec((None, 1, MOE_BLOCK), lambda i, be, nu: (jnp.minimum(i + 1, n_blk - 1), 0, 0),
                         memory_space=pltpu.SMEM),
            pl.BlockSpec(memory_space=pl.ANY),
            pl.BlockSpec((None, d, 2 * dff), lambda i, be, nu: (be[i], 0, 0)),
            pl.BlockSpec((None, 1, 2 * dff), lambda i, be, nu: (be[i], 0, 0)),
            pl.BlockSpec((None, dff, d), lambda i, be, nu: (be[i], 0, 0)),
            pl.BlockSpec((None, 1, d), lambda i, be, nu: (be[i], 0, 0)),
        ],
        out_specs=pl.BlockSpec((MOE_BLOCK * SUBLANES, dl), lambda i, be, nu: (i, 0)),
        scratch_shapes=[pltpu.VMEM((2, MOE_BLOCK * SUBLANES, dl), F32), pltpu.VMEM((d, 2 * dff), BF16),
                        pltpu.VMEM((dff, d), BF16), pltpu.SemaphoreType.DMA((2,))],
    )
    return pl.pallas_call(
        _expert_kernel,
        out_shape=jax.ShapeDtypeStruct((n_blk * MOE_BLOCK * SUBLANES, dl), F32),
        grid_spec=grid_spec,
        compiler_params=pltpu.CompilerParams(dimension_semantics=("arbitrary",), vmem_limit_bytes=EXPERT_VMEM_LIMIT,
                                             disable_bounds_checks=True),
        name="moe_experts",
    )(blk_e, n_used, buf_tok, buf_tok, x, w1, b1.reshape(N_EXPERTS, 1, 2 * dff), w2, b2.reshape(N_EXPERTS, 1, d))


def _combine_kernel(row_cur_ref, row_next_ref, x_ref, gate_ref, yb_hbm, g_ref, b_ref, o_ref, buf, sems,
                    *, tm, alpha):
    i = pl.program_id(0)
    slot = lax.rem(i, 2)

    @pl.when(i == 0)
    def _():
        _start_row_tiles(yb_hbm, row_cur_ref, tm, buf.at[0], sems.at[0], per_row=TOP_K)

    @pl.when(i + 1 < pl.num_programs(0))
    def _():
        _start_row_tiles(yb_hbm, row_next_ref, tm, buf.at[1 - slot], sems.at[1 - slot], per_row=TOP_K)

    _wait_buffer(buf.at[slot], sems.at[slot])
    gates = gate_ref[...]
    y = gates[:, 0:1] * _load_rows_tiled(buf.at[slot, 0], tm)
    for k in range(1, TOP_K):
        y = y + gates[:, k:k + 1] * _load_rows_tiled(buf.at[slot, k], tm)
    o_ref[...] = _layer_norm(alpha * _load_rows_tiled(x_ref, tm) + y, g_ref[...], b_ref[...])


def _combine(x, gates, row_of, yb, g, b, alpha, tm=256):
    dl = x.shape[1]
    n, d = x.shape[0] // SUBLANES, dl * SUBLANES
    n_tiles = n // tm
    rows = row_of.reshape(n_tiles, 1, tm * TOP_K)
    return pl.pallas_call(
        functools.partial(_combine_kernel, tm=tm, alpha=alpha),
        out_shape=jax.ShapeDtypeStruct((n, d), F32),
        grid=(n_tiles,),
        in_specs=[
            pl.BlockSpec((None, 1, tm * TOP_K), lambda i: (i, 0, 0), memory_space=pltpu.SMEM),
            pl.BlockSpec((None, 1, tm * TOP_K), lambda i: (jnp.minimum(i + 1, n_tiles - 1), 0, 0),
                         memory_space=pltpu.SMEM),
            pl.BlockSpec((tm * SUBLANES, dl), lambda i: (i, 0)),
            pl.BlockSpec((tm, LANES), lambda i: (i, 0)),
            pl.BlockSpec(memory_space=pl.ANY),
            pl.BlockSpec((1, d), lambda i: (0, 0)),
            pl.BlockSpec((1, d), lambda i: (0, 0)),
        ],
        out_specs=pl.BlockSpec((tm, d), lambda i: (i, 0)),
        scratch_shapes=[pltpu.VMEM((2, TOP_K, tm * SUBLANES, dl), F32), pltpu.SemaphoreType.DMA((2,))],
        compiler_params=pltpu.CompilerParams(dimension_semantics=("arbitrary",), vmem_limit_bytes=VMEM_LIMIT,
                                             disable_bounds_checks=True),
        name="moe_combine_ln",
    )(rows, rows, x, gates, yb, g.reshape(1, d), b.reshape(1, d))


def _head_major(z, n_heads):
    bsz, t, _ = z.shape
    return z.reshape(bsz, t, n_heads, HEAD_DIM).transpose(0, 2, 1, 3)


def _mixer_layer(x, w_in, b_in, pool_w, pool_b, pool_scale, cmp_pos, cmp_w1, cmp_b1, cmp_w2, cmp_b2,
                 w_up, w_o, ln_g, ln_b, alpha):
    bsz, t, d = x.shape
    n = bsz * t
    mix_w = d // 2
    g, hpg = NSA_KV_GROUPS, NSA_HPG
    nq_w = NSA_HEADS * HEAD_DIM
    nkv_w = N_BRANCH * 2 * g * HEAD_DIM
    nfox_w = 3 * FOX_HEADS * HEAD_DIM
    sizes = (mix_w, nq_w, nkv_w, N_BRANCH * NSA_HEADS, nfox_w, FOX_HEADS, N_BRANCH * d)
    o_pool, o_q, o_kv, o_g, o_fox, o_f, o_gm = (int(v) for v in np.cumsum((0,) + sizes[:-1]))
    n_gate = N_BRANCH * NSA_HEADS

    def cols(a, lo, width):
        return lax.slice_in_dim(a, lo, lo + width, axis=-1)

    qs = SCALE * LOG2E
    pad_small = 2 * LANES - n_gate - FOX_HEADS
    w_a = jnp.concatenate([cols(w_in, o_pool, mix_w), cols(w_in, o_g, n_gate), cols(w_in, o_f, FOX_HEADS),
                           jnp.zeros((d, pad_small), F32)], axis=1).astype(BF16)
    b_a = jnp.concatenate([cols(b_in, o_pool, mix_w), cols(b_in, o_g, n_gate), cols(b_in, o_f, FOX_HEADS),
                           jnp.zeros((pad_small,), F32)])
    w_b = jnp.concatenate([cols(w_in, o_q, nq_w) * qs, cols(w_in, o_kv, nkv_w),
                           cols(w_in, o_fox, FOX_HEADS * HEAD_DIM) * qs,
                           cols(w_in, o_fox + FOX_HEADS * HEAD_DIM, 2 * FOX_HEADS * HEAD_DIM)], axis=1).astype(BF16)
    b_b = jnp.concatenate([cols(b_in, o_q, nq_w) * qs, cols(b_in, o_kv, nkv_w),
                           cols(b_in, o_fox, FOX_HEADS * HEAD_DIM) * qs,
                           cols(b_in, o_fox + FOX_HEADS * HEAD_DIM, 2 * FOX_HEADS * HEAD_DIM)])
    w_c = cols(w_in, o_gm, N_BRANCH * d).astype(BF16)
    b_c = cols(b_in, o_gm, N_BRANCH * d)

    xf = x.reshape(n, d)
    z_a = _matmul_bias(xf, w_a, b_a, F32, tn=mix_w + 2 * LANES)
    z_b = _matmul_bias(xf, w_b, b_b, BF16, tn=256)
    gm = _matmul_bias(xf, w_c, b_c, BF16, act="sigmoid")

    z_a = z_a.reshape(bsz, t, -1)
    z_b = z_b.reshape(bsz, t, -1)
    u_pool = cols(z_a, 0, mix_w)
    z_small = cols(z_a, mix_w, 2 * LANES)

    y_pool = _pool_mixer(u_pool, pool_w, pool_b, pool_scale).reshape(n, mix_w)

    def feature_major(z, n_heads):
        return z.reshape(bsz, t, n_heads, HEAD_DIM).transpose(0, 2, 3, 1)

    def with_ones(vt):
        return jnp.concatenate([vt, jnp.ones(vt.shape[:-2] + (ONES_ROWS, t), BF16)], axis=-2)

    def token_major(ot):
        return ot.reshape(bsz, -1, t).transpose(0, 2, 1).reshape(n, -1)

    qt_nsa = feature_major(cols(z_b, 0, nq_w), NSA_HEADS).reshape(bsz, g, hpg, HEAD_DIM, t)
    kv = _head_major(cols(z_b, nq_w, nkv_w), N_BRANCH * 2 * g).reshape(bsz, N_BRANCH, 2, g, t, HEAD_DIM)
    n_chunk = t // CMP_STRIDE
    a_cmp = kv[:, 0].transpose(1, 0, 2, 3, 4).reshape(2, bsz * g, n_chunk, CMP_STRIDE * HEAD_DIM)
    kvc = _compress(a_cmp, cmp_pos.reshape(2, CMP_LEN * HEAD_DIM), cmp_w1, cmp_b1, cmp_w2, cmp_b2)
    kvc = kvc.reshape(2, bsz, g, n_chunk, HEAD_DIM)
    n_s = t // SEL_LEN
    c_start = np.arange(n_chunk) * CMP_STRIDE
    s_start = np.arange(n_s) * SEL_LEN
    ov = (c_start[:, None] < s_start[None, :] + SEL_LEN) & (c_start[:, None] + CMP_LEN > s_start[None, :])
    ov[n_chunk - (CMP_LEN // CMP_STRIDE) + 1:] = False
    ovt = jnp.asarray(ov.T.astype(np.float32), BF16)
    o_cmp, qaug = _nsa_cmp(qt_nsa, kvc[0], kvc[1].transpose(0, 1, 3, 2), ovt)
    onehot = jnp.asarray((np.arange(t)[:, None] // SEL_LEN == np.arange(n_s)[None, :]).astype(np.float32), BF16)
    kaug = jnp.concatenate([jnp.broadcast_to(onehot, (bsz, g, t, n_s)), kv[:, 1, 0]], axis=-1)
    o_slc = _nsa_slc(qaug, kaug, with_ones(kv[:, 1, 1].transpose(0, 1, 3, 2)))
    o_win = _nsa_win(qt_nsa, kv[:, 2, 0], with_ones(kv[:, 2, 1].transpose(0, 1, 3, 2)))

    cum = _fox_prep(z_small)
    cq = cols(cum, n_gate, FOX_HEADS).transpose(0, 3, 2, 1)
    fox = _head_major(cols(z_b, nq_w + nkv_w, nfox_w), 3 * FOX_HEADS).reshape(bsz, 3, FOX_HEADS, t, HEAD_DIM)
    ones = jnp.ones((bsz, FOX_HEADS, t, 3), BF16)
    zpad = jnp.zeros((bsz, FOX_HEADS, t, LANES - HEAD_DIM - 6), BF16)
    q_fox = jnp.concatenate([fox[:, 0], cq, ones, zpad], axis=-1)
    k_fox = jnp.concatenate([fox[:, 1], ones, -cq, zpad], axis=-1)
    y_fox = _fox_attention(q_fox.transpose(0, 1, 3, 2), k_fox, with_ones(fox[:, 2].transpose(0, 1, 3, 2)))

    hsel = np.arange(NSA_HEADS * HEAD_DIM) // HEAD_DIM
    expand = np.zeros((LANES, N_BRANCH * mix_w), np.float32)
    for br in range(N_BRANCH):
        expand[hsel * N_BRANCH + br, br * mix_w + np.arange(mix_w)] = 1.0
    return _merge(xf, y_pool, token_major(o_cmp), token_major(o_slc), token_major(o_win),
                  token_major(y_fox), cols(z_small, 0, LANES).reshape(n, LANES), gm,
                  jnp.asarray(expand, BF16), w_up.astype(BF16), w_o.astype(BF16), ln_g, ln_b, alpha)


def _moe_layer(x, router_w, router_b, w1, b1, w2, b2, ln_g, ln_b, alpha):
    n = x.shape[0] // SUBLANES
    nk = n * TOP_K
    idx_p, gate_p = _router(x, router_w, router_b)
    e_flat = idx_p[:, :TOP_K].reshape(nk)
    onehot = (e_flat[:, None] == jnp.arange(N_EXPERTS)[None, :]).astype(jnp.int32)
    incl = jnp.cumsum(onehot, axis=0)
    counts = incl[-1]
    rank = jnp.sum((incl - onehot) * onehot, axis=1)
    padded = (counts + MOE_BLOCK - 1) // MOE_BLOCK * MOE_BLOCK
    pend = jnp.cumsum(padded)
    pstart = pend - padded
    dest = pstart[e_flat] + rank
    n_blk = (nk + MOE_BLOCK - 1) // MOE_BLOCK + N_EXPERTS
    p = n_blk * MOE_BLOCK
    buf_tok = jnp.zeros((p,), jnp.int32).at[dest].set(jnp.arange(nk, dtype=jnp.int32) // TOP_K)
    blk_e = jnp.minimum(jnp.sum(jnp.arange(n_blk)[:, None] * MOE_BLOCK >= pend[None, :], axis=1),
                        N_EXPERTS - 1).astype(jnp.int32)
    n_used = (pend[-1] // MOE_BLOCK).astype(jnp.int32).reshape(1)

    yb = _expert_ffn(x, buf_tok.reshape(n_blk, 1, MOE_BLOCK), blk_e, n_used, w1, b1, w2, b2)
    return _combine(x, gate_p, dest.astype(jnp.int32), yb, ln_g, ln_b, alpha)


def kernel(x, w_in, b_in, pool_w, pool_b, pool_scale, cmp_pos, cmp_w1, cmp_b1, cmp_w2, cmp_b2,
           w_up, w_o, ln1_g, ln1_b, router_w, router_b, moe_w1, moe_b1, moe_w2, moe_b2, ln2_g, ln2_b):
    depth = w_in.shape[0]
    alpha = (2 * depth) ** 0.25
    bsz, t, d = x.shape
    for l in range(depth):
        x1 = _mixer_layer(x, w_in[l], b_in[l], pool_w[l], pool_b[l], pool_scale[l], cmp_pos[l], cmp_w1[l],
                          cmp_b1[l], cmp_w2[l], cmp_b2[l], w_up[l], w_o[l], ln1_g[l], ln1_b[l], alpha)
        x2 = _moe_layer(x1, router_w[l], router_b[l], moe_w1[l], moe_b1[l], moe_w2[l], moe_b2[l],
                        ln2_g[l], ln2_b[l], alpha)
        x = x2.reshape(bsz, t, d)
    return x
```

```python
import functools

import numpy as np
import jax
import jax.numpy as jnp
from jax import lax
from jax.experimental import pallas as pl
from jax.experimental.pallas import tpu as pltpu

F32 = jnp.float32
BF16 = jnp.bfloat16

HEAD_DIM = 64
POOL_WINDOWS = (2, 4, 8, 16)
POOL_GC = 128
POOL_HALO = 16
NSA_HEADS = 8
NSA_KV_GROUPS = 2
NSA_HPG = NSA_HEADS // NSA_KV_GROUPS
N_BRANCH = 3
CMP_LEN = 32
CMP_STRIDE = 16
SEL_LEN = 64
N_SEL = 16
WINDOW = 512
FOX_HEADS = 8
Q_BLOCK = 128
N_EXPERTS = 32
TOP_K = 4
SWIGLU_LIMIT = 7.0
SWIGLU_ALPHA = 1.702
MOE_BLOCK = 256
LN_EPS = 1e-5
NEG_INF = -1e30
SCALE = HEAD_DIM ** -0.5
MASK_BIG = 2.0 ** 100
LOG2E = 1.4426950408889634
ONES_ROWS = 8
LANES = 128
SUBLANES = 8
VMEM_LIMIT = 48 * 1024 * 1024
EXPERT_VMEM_LIMIT = 58 * 1024 * 1024
DMA_ISSUE_UNROLL = 8


def _cparams(sem):
    return pltpu.CompilerParams(dimension_semantics=sem, vmem_limit_bytes=VMEM_LIMIT)


def _split3(x):
    hi = x.astype(BF16)
    r1 = x - hi.astype(F32)
    mid = r1.astype(BF16)
    lo = (r1 - mid.astype(F32)).astype(BF16)
    return hi, mid, lo


def _layer_norm(r, g, b):
    mu = jnp.mean(r, axis=-1, keepdims=True)
    var = jnp.mean(jnp.square(r - mu), axis=-1, keepdims=True)
    return (r - mu) * lax.rsqrt(var + LN_EPS) * g + b


def _load_rows_tiled(ref, rows):
    return jnp.concatenate([ref[pl.ds(s, rows, stride=SUBLANES), :] for s in range(SUBLANES)], axis=1)


def _store_rows_tiled(ref, val):
    rows = val.shape[0]
    for s in range(SUBLANES):
        ref[pl.ds(s, rows, stride=SUBLANES), :] = val[:, s * LANES:(s + 1) * LANES]


def _row_tile_copy(src, src_row, dst, dst_row, sem):
    return pltpu.make_async_copy(src.at[pl.ds(pl.multiple_of(src_row * SUBLANES, SUBLANES), SUBLANES), :],
                                 dst.at[pl.ds(pl.multiple_of(dst_row * SUBLANES, SUBLANES), SUBLANES), :], sem)


def _mm_kernel(x_ref, w_ref, b_ref, o_ref, *, act):
    acc = jnp.dot(x_ref[...].astype(BF16), w_ref[...], preferred_element_type=F32) + b_ref[...]
    if act == "sigmoid":
        acc = jax.nn.sigmoid(acc)
    o_ref[...] = acc.astype(o_ref.dtype)


def _matmul_bias(x, w, b, out_dtype, act=None, tm=1024, tn=512):
    m, k = x.shape
    n = w.shape[1]
    tn = min(tn, n)
    assert m % tm == 0 and n % tn == 0
    return pl.pallas_call(
        functools.partial(_mm_kernel, act=act),
        out_shape=jax.ShapeDtypeStruct((m, n), out_dtype),
        grid=(m // tm, n // tn),
        in_specs=[
            pl.BlockSpec((tm, k), lambda i, j: (i, 0)),
            pl.BlockSpec((k, tn), lambda i, j: (0, j)),
            pl.BlockSpec((1, tn), lambda i, j: (0, j)),
        ],
        out_specs=pl.BlockSpec((tm, tn), lambda i, j: (i, j)),
        compiler_params=_cparams(("parallel", "arbitrary")),
        name="in_proj",
    )(x, w, b.reshape(1, n))


def _pool_kernel(prev_ref, cur_ref, w_ref, b_ref, sc_ref, o_ref, ext_ref, *, tile):
    i = pl.program_id(1)
    ext_ref[0:POOL_HALO, :] = jnp.where(i > 0, prev_ref[...], 0.0)
    ext_ref[POOL_HALO:POOL_HALO + tile, :] = cur_ref[...]
    t_idx = i * tile + lax.broadcasted_iota(jnp.int32, (tile, 1), 0)
    for gi, win in enumerate(POOL_WINDOWS):
        cols = slice(gi * POOL_GC, (gi + 1) * POOL_GC)
        u = ext_ref[POOL_HALO:POOL_HALO + tile, cols]
        wsum = u
        for j in range(1, win):
            wsum = wsum + ext_ref[POOL_HALO - j:POOL_HALO - j + tile, cols]
        cnt = jnp.minimum(t_idx + 1, win).astype(F32)
        d = wsum / cnt - u
        y = jnp.dot(d.astype(BF16), w_ref[gi], preferred_element_type=F32) + b_ref[gi]
        o_ref[:, cols] = (y * sc_ref[:, cols]).astype(o_ref.dtype)


def _pool_mixer(u, w, b, scale, tile=512):
    bsz, t, c = u.shape
    return pl.pallas_call(
        functools.partial(_pool_kernel, tile=tile),
        out_shape=jax.ShapeDtypeStruct((bsz, t, c), BF16),
        grid=(bsz, t // tile),
        in_specs=[
            pl.BlockSpec((None, POOL_HALO, c),
                         lambda bb, i: (bb, jnp.maximum(i * (tile // POOL_HALO) - 1, 0), 0)),
            pl.BlockSpec((None, tile, c), lambda bb, i: (bb, i, 0)),
            pl.BlockSpec((len(POOL_WINDOWS), POOL_GC, POOL_GC), lambda bb, i: (0, 0, 0)),
            pl.BlockSpec((len(POOL_WINDOWS), 1, POOL_GC), lambda bb, i: (0, 0, 0)),
            pl.BlockSpec((1, c), lambda bb, i: (0, 0)),
        ],
        out_specs=pl.BlockSpec((None, tile, c), lambda bb, i: (bb, i, 0)),
        scratch_shapes=[pltpu.VMEM((POOL_HALO + tile, c), F32)],
        compiler_params=_cparams(("parallel", "arbitrary")),
        name="pool_mixer",
    )(u, u, w.astype(BF16), b.reshape(len(POOL_WINDOWS), 1, POOL_GC), scale.reshape(1, c))


def _compress_kernel(a_ref, pos_ref, w1_ref, b1_ref, w2_ref, b2_ref, o_ref):
    half = CMP_STRIDE * HEAD_DIM
    a = a_ref[...].astype(F32)
    n_chunk = a.shape[0]
    top = (a + pos_ref[:, 0:half]).astype(BF16)
    bot = (a + pos_ref[:, half:2 * half]).astype(BF16)
    p1 = jnp.dot(top, w1_ref[0:half, :], preferred_element_type=F32)
    p2 = jnp.dot(bot, w1_ref[half:2 * half, :], preferred_element_type=F32)
    h = p1 + pltpu.roll(p2, n_chunk - 1, axis=0) + b1_ref[...]
    h = jax.nn.gelu(h)
    o = jnp.dot(h.astype(BF16), w2_ref[...], preferred_element_type=F32) + b2_ref[...]
    o_ref[...] = o.astype(o_ref.dtype)


def _compress(a, pos, w1, b1, w2, b2):
    _, bg, n_chunk, width = a.shape
    hid = w1.shape[-1]
    return pl.pallas_call(
        _compress_kernel,
        out_shape=jax.ShapeDtypeStruct((2, bg, n_chunk, HEAD_DIM), BF16),
        grid=(2, bg),
        in_specs=[
            pl.BlockSpec((None, None, n_chunk, width), lambda s, j: (s, j, 0, 0)),
            pl.BlockSpec((None, 1, 2 * width), lambda s, j: (s, 0, 0)),
            pl.BlockSpec((None, 2 * width, hid), lambda s, j: (s, 0, 0)),
            pl.BlockSpec((None, 1, hid), lambda s, j: (s, 0, 0)),
            pl.BlockSpec((None, hid, HEAD_DIM), lambda s, j: (s, 0, 0)),
            pl.BlockSpec((None, 1, HEAD_DIM), lambda s, j: (s, 0, 0)),
        ],
        out_specs=pl.BlockSpec((None, None, n_chunk, HEAD_DIM), lambda s, j: (s, j, 0, 0)),
        compiler_params=_cparams(("arbitrary", "arbitrary")),
        name="nsa_compress",
    )(a, pos.reshape(2, 1, 2 * width), w1.astype(BF16), b1.reshape(2, 1, hid),
      w2.astype(BF16), b2.reshape(2, 1, HEAD_DIM))


def _flash_steps_t(qts, ks, vts, carries, mask):
    scores = [jnp.dot(k, qt, preferred_element_type=F32) for k, qt in zip(ks, qts)]
    if mask is not None:
        masks = mask if isinstance(mask, (list, tuple)) else [mask] * len(scores)
        scores = [jnp.where(mk, s, NEG_INF) for mk, s in zip(masks, scores)]
    probs, stats = [], []
    for s, (m, _) in zip(scores, carries):
        m_new = jnp.maximum(m, jnp.max(s, axis=0, keepdims=True))
        probs.append(jnp.exp2(s - m_new).astype(BF16))
        stats.append((m_new, jnp.exp2(m - m_new)))
    return tuple((m_new, alpha * acc + jnp.dot(vt, p, preferred_element_type=F32))
                 for vt, p, (m_new, alpha), (_, acc) in zip(vts, probs, stats, carries))


def _flash_init_t(cols):
    return (jnp.full((1, cols), NEG_INF, F32), jnp.zeros((HEAD_DIM + ONES_ROWS, cols), F32))


def _flash_finish_t(acc):
    return acc[0:HEAD_DIM] / acc[HEAD_DIM:HEAD_DIM + 1]


def _store_heads(o_ref, o, lead=()):
    for h in range(NSA_HPG):
        o_ref[lead + (h,)] = o[:, h * Q_BLOCK:(h + 1) * Q_BLOCK].astype(o_ref.dtype)


def _nsa_cmp_kernel(qt_ref, kc_ref, vct_ref, ovt_ref, ocmp_ref, qaug_ref):
    groups = range(qt_ref.shape[0])
    cols = NSA_HPG * Q_BLOCK
    t0 = pl.program_id(1) * Q_BLOCK
    n_c = kc_ref.shape[1]
    n_s = ovt_ref.shape[0]
    qts = [jnp.concatenate([qt_ref[g, h] for h in range(NSA_HPG)], axis=1) for g in groups]
    tq = t0 + (lax.broadcasted_iota(jnp.int32, (n_c, cols), 1) & (Q_BLOCK - 1))
    c_end = lax.broadcasted_iota(jnp.int32, (n_c, cols), 0) * CMP_STRIDE + (CMP_LEN - 1)
    mask = c_end <= tq
    ps = []
    for g in groups:
        sm = jnp.where(mask, jnp.dot(kc_ref[g], qts[g], preferred_element_type=F32), NEG_INF)
        e = jnp.where(mask, jnp.exp2(sm - jnp.max(sm, axis=0, keepdims=True)), 0.0)
        l = jnp.sum(e, axis=0, keepdims=True)
        ps.append(e / jnp.where(l > 0.0, l, 1.0))
    for g in groups:
        _store_heads(ocmp_ref, jnp.dot(vct_ref[g], ps[g].astype(BF16), preferred_element_type=F32), (g,))

    ovt = ovt_ref[...]
    blk = lax.broadcasted_iota(jnp.int32, (n_s, Q_BLOCK), 0)
    tqq = t0 + lax.broadcasted_iota(jnp.int32, (n_s, Q_BLOCK), 1)
    future = blk * SEL_LEN > tqq
    cur = lax.shift_right_logical(tqq, 6)
    forced = (blk == 0) | (blk == cur) | (blk == cur - 1)
    scores = []
    for g in groups:
        psum = ps[g][:, 0:Q_BLOCK]
        for h in range(1, NSA_HPG):
            psum = psum + ps[g][:, h * Q_BLOCK:(h + 1) * Q_BLOCK]
        imp = None
        for part in _split3(psum):
            term = jnp.dot(ovt, part, preferred_element_type=F32)
            imp = term if imp is None else imp + term
        scores.append(jnp.where(future, -1.0, jnp.where(forced, 1e6, imp)))
    sels = [jnp.zeros((n_s, Q_BLOCK), F32) for _ in groups]
    for _ in range(min(N_SEL, n_s)):
        for g in groups:
            mx = jnp.max(scores[g], axis=0, keepdims=True)
            first = jnp.min(jnp.where(scores[g] == mx, blk, n_s), axis=0, keepdims=True)
            pick = blk == first
            sels[g] = jnp.where(pick, 1.0, sels[g])
            scores[g] = jnp.where(pick, -2.0, scores[g])
    for g in groups:
        sel = jnp.where(future, 0.0, sels[g])
        bias = ((sel - 1.0) * MASK_BIG).astype(BF16)
        for h in range(NSA_HPG):
            qaug_ref[g, 0:n_s, h * Q_BLOCK:(h + 1) * Q_BLOCK] = bias
        qaug_ref[g, n_s:n_s + HEAD_DIM, :] = qts[g]


def _nsa_cmp(qt, kc, vct, ovt):
    bsz, g, hpg, hd, t = qt.shape
    n_c = kc.shape[2]
    n_s = ovt.shape[0]
    nq = t // Q_BLOCK
    cols = hpg * Q_BLOCK
    return pl.pallas_call(
        _nsa_cmp_kernel,
        out_shape=(jax.ShapeDtypeStruct((bsz, g, hpg, hd, t), BF16),
                   jax.ShapeDtypeStruct((bsz, g, nq, n_s + hd, cols), BF16)),
        grid=(bsz, nq),
        in_specs=[
            pl.BlockSpec((None, g, hpg, hd, Q_BLOCK), lambda b, i: (b, 0, 0, 0, i)),
            pl.BlockSpec((None, g, n_c, hd), lambda b, i: (b, 0, 0, 0)),
            pl.BlockSpec((None, g, hd, n_c), lambda b, i: (b, 0, 0, 0)),
            pl.BlockSpec((n_s, n_c), lambda b, i: (0, 0)),
        ],
        out_specs=(
            pl.BlockSpec((None, g, hpg, hd, Q_BLOCK), lambda b, i: (b, 0, 0, 0, i)),
            pl.BlockSpec((None, g, None, n_s + hd, cols), lambda b, i: (b, 0, i, 0, 0)),
        ),
        compiler_params=_cparams(("parallel", "arbitrary")),
        name="nsa_cmp_select",
    )(qt, kc, vct, ovt)


def _nsa_slc_kernel(qaug_ref, kaug_ref, vt_ref, o_ref, *, tk, qb):
    groups = qaug_ref.shape[0]
    cols = NSA_HPG * Q_BLOCK
    t0 = pl.program_id(1) * (qb * Q_BLOCK)
    jd = t0 // tk
    probs = [(g, b) for g in range(groups) for b in range(qb)]

    def chunk(j, carries, mask):
        start = pl.multiple_of(j * tk, tk)
        return _flash_steps_t([qaug_ref[g, b] for g, b in probs],
                              [kaug_ref[g, pl.ds(start, tk), :] for g, _ in probs],
                              [vt_ref[g, :, pl.ds(start, tk)] for g, _ in probs], carries, mask)

    carries = lax.fori_loop(0, jd, lambda j, c: chunk(j, c, None), tuple(_flash_init_t(cols) for _ in probs))
    kpos = jd * tk + lax.broadcasted_iota(jnp.int32, (tk, cols), 0)
    tq = t0 + (lax.broadcasted_iota(jnp.int32, (tk, cols), 1) & (Q_BLOCK - 1))
    causal = [kpos <= tq + b * Q_BLOCK for b in range(qb)]
    carries = chunk(jd, carries, [causal[b] for _, b in probs])
    for (g, b), (_, acc) in zip(probs, carries):
        o = _flash_finish_t(acc)
        for h in range(NSA_HPG):
            o_ref[g, h, :, b * Q_BLOCK:(b + 1) * Q_BLOCK] = o[:, h * Q_BLOCK:(h + 1) * Q_BLOCK].astype(o_ref.dtype)


def _nsa_slc(qaug, kaug, vt, tk=512, qb=2):
    bsz, g, nq, wa, cols = qaug.shape
    t = kaug.shape[2]
    tk = min(tk, t)
    assert tk % (qb * Q_BLOCK) == 0 and nq % qb == 0
    return pl.pallas_call(
        functools.partial(_nsa_slc_kernel, tk=tk, qb=qb),
        out_shape=jax.ShapeDtypeStruct((bsz, g, NSA_HPG, HEAD_DIM, t), BF16),
        grid=(bsz, nq // qb),
        in_specs=[
            pl.BlockSpec((None, g, qb, wa, cols), lambda b, i: (b, 0, i, 0, 0)),
            pl.BlockSpec((None, g, t, wa), lambda b, i: (b, 0, 0, 0)),
            pl.BlockSpec((None, g, HEAD_DIM + ONES_ROWS, t), lambda b, i: (b, 0, 0, 0)),
        ],
        out_specs=pl.BlockSpec((None, g, NSA_HPG, HEAD_DIM, qb * Q_BLOCK), lambda b, i: (b, 0, 0, 0, i)),
        compiler_params=_cparams(("parallel", "arbitrary")),
        name="nsa_selected",
    )(qaug, kaug, vt)


def _nsa_win_kernel(qt_ref, k_ref, vt_ref, o_ref):
    groups = qt_ref.shape[0]
    cols = NSA_HPG * Q_BLOCK
    span = WINDOW + Q_BLOCK
    t0 = pl.program_id(1) * Q_BLOCK
    start = pl.multiple_of(jnp.maximum(t0 - WINDOW, 0), Q_BLOCK)
    dist = (t0 - start) + (lax.broadcasted_iota(jnp.int32, (span, cols), 1) & (Q_BLOCK - 1)) \
        - lax.broadcasted_iota(jnp.int32, (span, cols), 0)
    mask = (dist >= 0) & (dist < WINDOW)
    scores = []
    for g in range(groups):
        qt = jnp.concatenate([qt_ref[g, h] for h in range(NSA_HPG)], axis=1)
        s = jnp.dot(k_ref[g, pl.ds(start, span), :], qt, preferred_element_type=F32)
        scores.append(jnp.where(mask, s, NEG_INF))
    probs = [jnp.exp2(s - jnp.max(s, axis=0, keepdims=True)).astype(BF16) for s in scores]
    for g in range(groups):
        acc = jnp.dot(vt_ref[g, :, pl.ds(start, span)], probs[g], preferred_element_type=F32)
        _store_heads(o_ref, _flash_finish_t(acc), (g,))


def _nsa_win(qt, k, vt):
    bsz, g, hpg, hd, t = qt.shape
    assert t >= WINDOW + Q_BLOCK
    return pl.pallas_call(
        _nsa_win_kernel,
        out_shape=jax.ShapeDtypeStruct((bsz, g, hpg, hd, t), BF16),
        grid=(bsz, t // Q_BLOCK),
        in_specs=[
            pl.BlockSpec((None, g, hpg, hd, Q_BLOCK), lambda b, i: (b, 0, 0, 0, i)),
            pl.BlockSpec((None, g, t, hd), lambda b, i: (b, 0, 0, 0)),
            pl.BlockSpec((None, g, hd + ONES_ROWS, t), lambda b, i: (b, 0, 0, 0)),
        ],
        out_specs=pl.BlockSpec((None, g, hpg, hd, Q_BLOCK), lambda b, i: (b, 0, 0, 0, i)),
        compiler_params=_cparams(("parallel", "arbitrary")),
        name="nsa_window",
    )(qt, k, vt)


def _fox_prep_kernel(z_ref, tri_ref, o_ref, carry_ref):
    @pl.when(pl.program_id(1) == 0)
    def _():
        carry_ref[...] = jnp.zeros_like(carry_ref)

    lf = jax.nn.log_sigmoid(z_ref[...])
    hi, mid, lo = _split3(lf)
    tri = tri_ref[...]
    cum = (jnp.dot(tri, hi, preferred_element_type=F32) + jnp.dot(tri, mid, preferred_element_type=F32)
           + jnp.dot(tri, lo, preferred_element_type=F32)) + carry_ref[...]
    carry_ref[...] = cum[cum.shape[0] - 1:cum.shape[0], :]
    for part, val in enumerate(_split3(cum * LOG2E)):
        o_ref[part] = val


def _fox_prep(z_small, tile=256):
    bsz, t, _ = z_small.shape
    tri = jnp.asarray(np.tril(np.ones((tile, tile), np.float32)), BF16)
    return pl.pallas_call(
        _fox_prep_kernel,
        out_shape=jax.ShapeDtypeStruct((bsz, 3, t, LANES), BF16),
        grid=(bsz, t // tile),
        in_specs=[
            pl.BlockSpec((None, tile, LANES), lambda b, i: (b, i, 0)),
            pl.BlockSpec((tile, tile), lambda b, i: (0, 0)),
        ],
        out_specs=pl.BlockSpec((None, 3, tile, LANES), lambda b, i: (b, 0, i, 0)),
        scratch_shapes=[pltpu.VMEM((1, LANES), F32)],
        compiler_params=_cparams(("parallel", "arbitrary")),
        name="fox_decay_cumsum",
    )(z_small, tri)


def _fox_kernel(qt_ref, k_ref, vt_ref, o_ref, *, tq, heads):
    i = pl.program_id(2)

    def chunk(j, carries, mask):
        start = pl.multiple_of(j * tq, tq)
        return _flash_steps_t([qt_ref[h] for h in range(heads)],
                              [k_ref[h, pl.ds(start, tq), :] for h in range(heads)],
                              [vt_ref[h, :, pl.ds(start, tq)] for h in range(heads)], carries, mask)

    carries = lax.fori_loop(0, i, lambda j, c: chunk(j, c, None), tuple(_flash_init_t(tq) for _ in range(heads)))
    causal = lax.broadcasted_iota(jnp.int32, (tq, tq), 0) <= lax.broadcasted_iota(jnp.int32, (tq, tq), 1)
    carries = chunk(i, carries, causal)
    for h in range(heads):
        o_ref[h] = _flash_finish_t(carries[h][1]).astype(o_ref.dtype)


def _fox_attention(qt, kaug, vt, tq=512, heads=4):
    bsz, h, wa, t = qt.shape
    tq = min(tq, t)
    return pl.pallas_call(
        functools.partial(_fox_kernel, tq=tq, heads=heads),
        out_shape=jax.ShapeDtypeStruct((bsz, h, HEAD_DIM, t), BF16),
        grid=(bsz, h // heads, t // tq),
        in_specs=[
            pl.BlockSpec((None, heads, wa, tq), lambda b, hh, i: (b, hh, 0, i)),
            pl.BlockSpec((None, heads, t, wa), lambda b, hh, i: (b, hh, 0, 0)),
            pl.BlockSpec((None, heads, HEAD_DIM + ONES_ROWS, t), lambda b, hh, i: (b, hh, 0, 0)),
        ],
        out_specs=pl.BlockSpec((None, heads, HEAD_DIM, tq), lambda b, hh, i: (b, hh, 0, i)),
        compiler_params=_cparams(("parallel", "parallel", "arbitrary")),
        name="fox_attention",
    )(qt, kaug, vt)


def _merge_kernel(x_ref, yp_ref, oc_ref, os_ref, ow_ref, yf_ref, zs_ref, gm_ref, ex_ref,
                  wup_ref, wo_ref, g_ref, b_ref, o_ref, *, alpha):
    mix_w = yp_ref.shape[-1]
    d = x_ref.shape[-1]
    sg = jax.nn.sigmoid(zs_ref[...])
    hi, mid, lo = _split3(sg)
    ex = ex_ref[...]
    ge = (jnp.dot(hi, ex, preferred_element_type=F32) + jnp.dot(mid, ex, preferred_element_type=F32)
          + jnp.dot(lo, ex, preferred_element_type=F32))
    yn = (ge[:, 0:mix_w] * oc_ref[...].astype(F32) + ge[:, mix_w:2 * mix_w] * os_ref[...].astype(F32)
          + ge[:, 2 * mix_w:3 * mix_w] * ow_ref[...].astype(F32))
    ys = (yp_ref[...], yn.astype(BF16), yf_ref[...])
    mix = None
    for n in range(N_BRANCH):
        up = jnp.dot(ys[n], wup_ref[n], preferred_element_type=F32)
        term = gm_ref[:, n * d:(n + 1) * d].astype(F32) * up
        mix = term if mix is None else mix + term
    h = jnp.dot(mix.astype(BF16), wo_ref[...], preferred_element_type=F32)
    _store_rows_tiled(o_ref, _layer_norm(alpha * x_ref[...] + h, g_ref[...], b_ref[...]))


def _merge(x, yp, oc, osl, ow, yf, zs, gm, expand, wup, wo, g, b, alpha, tm=512):
    n, d = x.shape
    mix_w = yp.shape[1]
    tok = lambda width: pl.BlockSpec((tm, width), lambda i: (i, 0))
    return pl.pallas_call(
        functools.partial(_merge_kernel, alpha=alpha),
        out_shape=jax.ShapeDtypeStruct((n * SUBLANES, d // SUBLANES), F32),
        grid=(n // tm,),
        in_specs=[
            tok(d), tok(mix_w), tok(mix_w), tok(mix_w), tok(mix_w), tok(mix_w),
            pl.BlockSpec((tm, LANES), lambda i: (i, 0)),
            tok(N_BRANCH * d),
            pl.BlockSpec((LANES, N_BRANCH * mix_w), lambda i: (0, 0)),
            pl.BlockSpec((N_BRANCH, mix_w, d), lambda i: (0, 0, 0)),
            pl.BlockSpec((d, d), lambda i: (0, 0)),
            pl.BlockSpec((1, d), lambda i: (0, 0)),
            pl.BlockSpec((1, d), lambda i: (0, 0)),
        ],
        out_specs=pl.BlockSpec((tm * SUBLANES, d // SUBLANES), lambda i: (i, 0)),
        compiler_params=_cparams(("parallel",)),
        name="merge_outproj_ln",
    )(x, yp, oc, osl, ow, yf, zs, gm, expand, wup, wo, g.reshape(1, d), b.reshape(1, d))


def _router_kernel(x_ref, w_ref, b_ref, idx_ref, gate_ref):
    x = _load_rows_tiled(x_ref, idx_ref.shape[0])
    logits = jnp.dot(x, w_ref[...], preferred_element_type=F32,
                     precision=lax.Precision.HIGHEST) + b_ref[...]
    tm, lanes = logits.shape
    lane = lax.broadcasted_iota(jnp.int32, (tm, lanes), 1)
    work = jnp.where(lane < N_EXPERTS, logits, -jnp.inf)
    idx_out = jnp.zeros((tm, lanes), jnp.int32)
    val_out = jnp.zeros((tm, lanes), F32)
    top = None
    denom = jnp.zeros((tm, 1), F32)
    for k in range(TOP_K):
        mx = jnp.max(work, axis=-1, keepdims=True)
        first = jnp.min(jnp.where(work == mx, lane, lanes), axis=-1, keepdims=True)
        if top is None:
            top = mx
        e = jnp.exp(mx - top)
        denom = denom + e
        idx_out = jnp.where(lane == k, first, idx_out)
        val_out = jnp.where(lane == k, e, val_out)
        work = jnp.where(lane == first, -jnp.inf, work)
    idx_ref[...] = idx_out
    gate_ref[...] = val_out / denom


def _router(x, w, b, tm=512):
    n, d = x.shape[0] // SUBLANES, x.shape[1] * SUBLANES
    wp = jnp.zeros((d, LANES), F32).at[:, :N_EXPERTS].set(w)
    bp = jnp.zeros((1, LANES), F32).at[0, :N_EXPERTS].set(b)
    return pl.pallas_call(
        _router_kernel,
        out_shape=(jax.ShapeDtypeStruct((n, LANES), jnp.int32), jax.ShapeDtypeStruct((n, LANES), F32)),
        grid=(n // tm,),
        in_specs=[
            pl.BlockSpec((tm * SUBLANES, d // SUBLANES), lambda i: (i, 0)),
            pl.BlockSpec((d, LANES), lambda i: (0, 0)),
            pl.BlockSpec((1, LANES), lambda i: (0, 0)),
        ],
        out_specs=(pl.BlockSpec((tm, LANES), lambda i: (i, 0)), pl.BlockSpec((tm, LANES), lambda i: (i, 0))),
        compiler_params=_cparams(("parallel",)),
        name="moe_router",
    )(x, wp, bp)


def _start_row_tiles(src_hbm, idx_ref, n, dst, sem, *, per_row=1):
    def body(r, c):
        for k in range(per_row):
            _row_tile_copy(src_hbm, idx_ref[0, r * per_row + k], dst.at[k] if per_row > 1 else dst, r, sem).start()
        return c
    lax.fori_loop(0, n, body, 0, unroll=DMA_ISSUE_UNROLL)


def _wait_buffer(buf, sem):
    pltpu.make_async_copy(buf, buf, sem).wait()


def _expert_kernel(be_ref, nu_ref, tok_cur_ref, tok_next_ref, x_hbm, w1_ref, b1_ref, w2_ref, b2_ref, o_ref,
                   xbuf, w1_bf, w2_bf, sems):
    i = pl.program_id(0)
    n_used = nu_ref[0]
    slot = lax.rem(i, 2)

    @pl.when((i == 0) & (n_used > 0))
    def _():
        _start_row_tiles(x_hbm, tok_cur_ref, MOE_BLOCK, xbuf.at[0], sems.at[0])

    @pl.when(i + 1 < n_used)
    def _():
        _start_row_tiles(x_hbm, tok_next_ref, MOE_BLOCK, xbuf.at[1 - slot], sems.at[1 - slot])

    @pl.when((i < n_used) & ((i == 0) | (be_ref[i] != be_ref[jnp.maximum(i - 1, 0)])))
    def _():
        w1_bf[...] = w1_ref[...].astype(BF16)
        w2_bf[...] = w2_ref[...].astype(BF16)

    @pl.when(i < n_used)
    def _():
        _wait_buffer(xbuf.at[slot], sems.at[slot])
        dff = w2_ref.shape[0]
        x = _load_rows_tiled(xbuf.at[slot], MOE_BLOCK).astype(BF16)
        h = jnp.dot(x, w1_bf[...], preferred_element_type=F32) + b1_ref[...]
        gate = jnp.minimum(h[:, 0:dff], SWIGLU_LIMIT)
        upv = jnp.clip(h[:, dff:2 * dff], -SWIGLU_LIMIT, SWIGLU_LIMIT)
        act = (upv + 1.0) * (gate * jax.nn.sigmoid(SWIGLU_ALPHA * gate))
        _store_rows_tiled(o_ref, jnp.dot(act.astype(BF16), w2_bf[...], preferred_element_type=F32) + b2_ref[...])

    @pl.when(i >= n_used)
    def _():
        o_ref[...] = jnp.zeros_like(o_ref)


def _expert_ffn(x, buf_tok, blk_e, n_used, w1, b1, w2, b2, layer):
    n_blk = buf_tok.shape[0]
    dl = x.shape[1]
    d = dl * SUBLANES
    dff = w2.shape[2]
    depth = w1.shape[0]
    grid_spec = pltpu.PrefetchScalarGridSpec(
        num_scalar_prefetch=2,
        grid=(n_blk,),
        in_specs=[
            pl.BlockSpec((None, 1, MOE_BLOCK), lambda i, be, nu: (i, 0, 0), memory_space=pltpu.SMEM),
            pl.BlockSpec((None, 1, MOE_BLOCK), lambda i, be, nu: (jnp.minimum(i + 1, n_blk - 1), 0, 0),
                         memory_space=pltpu.SMEM),
            pl.BlockSpec(memory_space=pl.ANY),
            pl.BlockSpec((None, None, d, 2 * dff), lambda i, be, nu: (layer, be[i], 0, 0)),
            pl.BlockSpec((None, None, 1, 2 * dff), lambda i, be, nu: (layer, be[i], 0, 0)),
            pl.BlockSpec((None, None, dff, d), lambda i, be, nu: (layer, be[i], 0, 0)),
            pl.BlockSpec((None, None, 1, d), lambda i, be, nu: (layer, be[i], 0, 0)),
        ],
        out_specs=pl.BlockSpec((MOE_BLOCK * SUBLANES, dl), lambda i, be, nu: (i, 0)),
        scratch_shapes=[pltpu.VMEM((2, MOE_BLOCK * SUBLANES, dl), F32), pltpu.VMEM((d, 2 * dff), BF16),
                        pltpu.VMEM((dff, d), BF16), pltpu.SemaphoreType.DMA((2,))],
    )
    return pl.pallas_call(
        _expert_kernel,
        out_shape=jax.ShapeDtypeStruct((n_blk * MOE_BLOCK * SUBLANES, dl), F32),
        grid_spec=grid_spec,
        compiler_params=pltpu.CompilerParams(dimension_semantics=("arbitrary",), vmem_limit_bytes=EXPERT_VMEM_LIMIT,
                                             disable_bounds_checks=True),
        name="moe_experts",
    )(blk_e, n_used, buf_tok, buf_tok, x, w1, b1.reshape(depth, N_EXPERTS, 1, 2 * dff), w2,
      b2.reshape(depth, N_EXPERTS, 1, d))


def _combine_kernel(row_cur_ref, row_next_ref, x_ref, gate_ref, yb_hbm, g_ref, b_ref, o_ref, buf, sems,
                    *, tm, alpha):
    i = pl.program_id(0)
    slot = lax.rem(i, 2)

    @pl.when(i == 0)
    def _():
        _start_row_tiles(yb_hbm, row_cur_ref, tm, buf.at[0], sems.at[0], per_row=TOP_K)

    @pl.when(i + 1 < pl.num_programs(0))
    def _():
        _start_row_tiles(yb_hbm, row_next_ref, tm, buf.at[1 - slot], sems.at[1 - slot], per_row=TOP_K)

    _wait_buffer(buf.at[slot], sems.at[slot])
    gates = gate_ref[...]
    y = gates[:, 0:1] * _load_rows_tiled(buf.at[slot, 0], tm)
    for k in range(1, TOP_K):
        y = y + gates[:, k:k + 1] * _load_rows_tiled(buf.at[slot, k], tm)
    o_ref[...] = _layer_norm(alpha * _load_rows_tiled(x_ref, tm) + y, g_ref[...], b_ref[...])


def _combine(x, gates, row_of, yb, g, b, alpha, tm=256):
    dl = x.shape[1]
    n, d = x.shape[0] // SUBLANES, dl * SUBLANES
    n_tiles = n // tm
    rows = row_of.reshape(n_tiles, 1, tm * TOP_K)
    return pl.pallas_call(
        functools.partial(_combine_kernel, tm=tm, alpha=alpha),
        out_shape=jax.ShapeDtypeStruct((n, d), F32),
        grid=(n_tiles,),
        in_specs=[
            pl.BlockSpec((None, 1, tm * TOP_K), lambda i: (i, 0, 0), memory_space=pltpu.SMEM),
            pl.BlockSpec((None, 1, tm * TOP_K), lambda i: (jnp.minimum(i + 1, n_tiles - 1), 0, 0),
                         memory_space=pltpu.SMEM),
            pl.BlockSpec((tm * SUBLANES, dl), lambda i: (i, 0)),
            pl.BlockSpec((tm, LANES), lambda i: (i, 0)),
            pl.BlockSpec(memory_space=pl.ANY),
            pl.BlockSpec((1, d), lambda i: (0, 0)),
            pl.BlockSpec((1, d), lambda i: (0, 0)),
        ],
        out_specs=pl.BlockSpec((tm, d), lambda i: (i, 0)),
        scratch_shapes=[pltpu.VMEM((2, TOP_K, tm * SUBLANES, dl), F32), pltpu.SemaphoreType.DMA((2,))],
        compiler_params=pltpu.CompilerParams(dimension_semantics=("arbitrary",), vmem_limit_bytes=VMEM_LIMIT,
                                             disable_bounds_checks=True),
        name="moe_combine_ln",
    )(rows, rows, x, gates, yb, g.reshape(1, d), b.reshape(1, d))


def _head_major(z, n_heads):
    bsz, t, _ = z.shape
    return z.reshape(bsz, t, n_heads, HEAD_DIM).transpose(0, 2, 1, 3)


def _mixer_layer(x, w_in, b_in, pool_w, pool_b, pool_scale, cmp_pos, cmp_w1, cmp_b1, cmp_w2, cmp_b2,
                 w_up, w_o, ln_g, ln_b, alpha):
    bsz, t, d = x.shape
    n = bsz * t
    mix_w = d // 2
    g, hpg = NSA_KV_GROUPS, NSA_HPG
    nq_w = NSA_HEADS * HEAD_DIM
    nkv_w = N_BRANCH * 2 * g * HEAD_DIM
    nfox_w = 3 * FOX_HEADS * HEAD_DIM
    sizes = (mix_w, nq_w, nkv_w, N_BRANCH * NSA_HEADS, nfox_w, FOX_HEADS, N_BRANCH * d)
    o_pool, o_q, o_kv, o_g, o_fox, o_f, o_gm = (int(v) for v in np.cumsum((0,) + sizes[:-1]))
    n_gate = N_BRANCH * NSA_HEADS

    def cols(a, lo, width):
        return lax.slice_in_dim(a, lo, lo + width, axis=-1)

    qs = SCALE * LOG2E
    pad_small = 2 * LANES - n_gate - FOX_HEADS
    w_a = jnp.concatenate([cols(w_in, o_pool, mix_w), cols(w_in, o_g, n_gate), cols(w_in, o_f, FOX_HEADS),
                           jnp.zeros((d, pad_small), F32)], axis=1).astype(BF16)
    b_a = jnp.concatenate([cols(b_in, o_pool, mix_w), cols(b_in, o_g, n_gate), cols(b_in, o_f, FOX_HEADS),
                           jnp.zeros((pad_small,), F32)])
    w_b = jnp.concatenate([cols(w_in, o_q, nq_w) * qs, cols(w_in, o_kv, nkv_w),
                           cols(w_in, o_fox, FOX_HEADS * HEAD_DIM) * qs,
                           cols(w_in, o_fox + FOX_HEADS * HEAD_DIM, 2 * FOX_HEADS * HEAD_DIM)], axis=1).astype(BF16)
    b_b = jnp.concatenate([cols(b_in, o_q, nq_w) * qs, cols(b_in, o_kv, nkv_w),
                           cols(b_in, o_fox, FOX_HEADS * HEAD_DIM) * qs,
                           cols(b_in, o_fox + FOX_HEADS * HEAD_DIM, 2 * FOX_HEADS * HEAD_DIM)])
    w_c = cols(w_in, o_gm, N_BRANCH * d).astype(BF16)
    b_c = cols(b_in, o_gm, N_BRANCH * d)

    xf = x.reshape(n, d)
    z_a = _matmul_bias(xf, w_a, b_a, F32, tn=mix_w + 2 * LANES)
    z_b = _matmul_bias(xf, w_b, b_b, BF16, tn=256)
    gm = _matmul_bias(xf, w_c, b_c, BF16, act="sigmoid")

    z_a = z_a.reshape(bsz, t, -1)
    z_b = z_b.reshape(bsz, t, -1)
    u_pool = cols(z_a, 0, mix_w)
    z_small = cols(z_a, mix_w, 2 * LANES)

    y_pool = _pool_mixer(u_pool, pool_w, pool_b, pool_scale).reshape(n, mix_w)

    def feature_major(z, n_heads):
        return z.reshape(bsz, t, n_heads, HEAD_DIM).transpose(0, 2, 3, 1)

    def with_ones(vt):
        return jnp.concatenate([vt, jnp.ones(vt.shape[:-2] + (ONES_ROWS, t), BF16)], axis=-2)

    def token_major(ot):
        return ot.reshape(bsz, -1, t).transpose(0, 2, 1).reshape(n, -1)

    qt_nsa = feature_major(cols(z_b, 0, nq_w), NSA_HEADS).reshape(bsz, g, hpg, HEAD_DIM, t)
    kv = _head_major(cols(z_b, nq_w, nkv_w), N_BRANCH * 2 * g).reshape(bsz, N_BRANCH, 2, g, t, HEAD_DIM)
    n_chunk = t // CMP_STRIDE
    a_cmp = kv[:, 0].transpose(1, 0, 2, 3, 4).reshape(2, bsz * g, n_chunk, CMP_STRIDE * HEAD_DIM)
    kvc = _compress(a_cmp, cmp_pos.reshape(2, CMP_LEN * HEAD_DIM), cmp_w1, cmp_b1, cmp_w2, cmp_b2)
    kvc = kvc.reshape(2, bsz, g, n_chunk, HEAD_DIM)
    n_s = t // SEL_LEN
    c_start = np.arange(n_chunk) * CMP_STRIDE
    s_start = np.arange(n_s) * SEL_LEN
    ov = (c_start[:, None] < s_start[None, :] + SEL_LEN) & (c_start[:, None] + CMP_LEN > s_start[None, :])
    ov[n_chunk - (CMP_LEN // CMP_STRIDE) + 1:] = False
    ovt = jnp.asarray(ov.T.astype(np.float32), BF16)
    o_cmp, qaug = _nsa_cmp(qt_nsa, kvc[0], kvc[1].transpose(0, 1, 3, 2), ovt)
    onehot = jnp.asarray((np.arange(t)[:, None] // SEL_LEN == np.arange(n_s)[None, :]).astype(np.float32), BF16)
    kaug = jnp.concatenate([jnp.broadcast_to(onehot, (bsz, g, t, n_s)), kv[:, 1, 0]], axis=-1)
    o_slc = _nsa_slc(qaug, kaug, with_ones(kv[:, 1, 1].transpose(0, 1, 3, 2)))
    o_win = _nsa_win(qt_nsa, kv[:, 2, 0], with_ones(kv[:, 2, 1].transpose(0, 1, 3, 2)))

    cum = _fox_prep(z_small)
    cq = cols(cum, n_gate, FOX_HEADS).transpose(0, 3, 2, 1)
    fox = _head_major(cols(z_b, nq_w + nkv_w, nfox_w), 3 * FOX_HEADS).reshape(bsz, 3, FOX_HEADS, t, HEAD_DIM)
    ones = jnp.ones((bsz, FOX_HEADS, t, 3), BF16)
    zpad = jnp.zeros((bsz, FOX_HEADS, t, LANES - HEAD_DIM - 6), BF16)
    q_fox = jnp.concatenate([fox[:, 0], cq, ones, zpad], axis=-1)
    k_fox = jnp.concatenate([fox[:, 1], ones, -cq, zpad], axis=-1)
    y_fox = _fox_attention(q_fox.transpose(0, 1, 3, 2), k_fox, with_ones(fox[:, 2].transpose(0, 1, 3, 2)))

    hsel = np.arange(NSA_HEADS * HEAD_DIM) // HEAD_DIM
    expand = np.zeros((LANES, N_BRANCH * mix_w), np.float32)
    for br in range(N_BRANCH):
        expand[hsel * N_BRANCH + br, br * mix_w + np.arange(mix_w)] = 1.0
    return _merge(xf, y_pool, token_major(o_cmp), token_major(o_slc), token_major(o_win),
                  token_major(y_fox), cols(z_small, 0, LANES).reshape(n, LANES), gm,
                  jnp.asarray(expand, BF16), w_up.astype(BF16), w_o.astype(BF16), ln_g, ln_b, alpha)


def _moe_layer(x, router_w, router_b, w1, b1, w2, b2, layer, ln_g, ln_b, alpha):
    n = x.shape[0] // SUBLANES
    nk = n * TOP_K
    idx_p, gate_p = _router(x, router_w, router_b)
    e_flat = idx_p[:, :TOP_K].reshape(nk)
    onehot = (e_flat[:, None] == jnp.arange(N_EXPERTS)[None, :]).astype(jnp.int32)
    incl = jnp.cumsum(onehot, axis=0)
    counts = incl[-1]
    rank = jnp.sum((incl - onehot) * onehot, axis=1)
    padded = (counts + MOE_BLOCK - 1) // MOE_BLOCK * MOE_BLOCK
    pend = jnp.cumsum(padded)
    pstart = pend - padded
    dest = pstart[e_flat] + rank
    n_blk = (nk + MOE_BLOCK - 1) // MOE_BLOCK + N_EXPERTS
    p = n_blk * MOE_BLOCK
    buf_tok = jnp.zeros((p,), jnp.int32).at[dest].set(jnp.arange(nk, dtype=jnp.int32) // TOP_K)
    blk_e = jnp.minimum(jnp.sum(jnp.arange(n_blk)[:, None] * MOE_BLOCK >= pend[None, :], axis=1),
                        N_EXPERTS - 1).astype(jnp.int32)
    n_used = (pend[-1] // MOE_BLOCK).astype(jnp.int32).reshape(1)

    yb = _expert_ffn(x, buf_tok.reshape(n_blk, 1, MOE_BLOCK), blk_e, n_used, w1, b1, w2, b2, layer)
    return _combine(x, gate_p, dest.astype(jnp.int32), yb, ln_g, ln_b, alpha)


def kernel(x, w_in, b_in, pool_w, pool_b, pool_scale, cmp_pos, cmp_w1, cmp_b1, cmp_w2, cmp_b2,
           w_up, w_o, ln1_g, ln1_b, router_w, router_b, moe_w1, moe_b1, moe_w2, moe_b2, ln2_g, ln2_b):
    depth = w_in.shape[0]
    alpha = (2 * depth) ** 0.25
    bsz, t, d = x.shape
    for l in range(depth):
        x1 = _mixer_layer(x, w_in[l], b_in[l], pool_w[l], pool_b[l], pool_scale[l], cmp_pos[l], cmp_w1[l],
                          cmp_b1[l], cmp_w2[l], cmp_b2[l], w_up[l], w_o[l], ln1_g[l], ln1_b[l], alpha)
        x2 = _moe_layer(x1, router_w[l], router_b[l], moe_w1, moe_b1, moe_w2, moe_b2, l,
                        ln2_g[l], ln2_b[l], alpha)
        x = x2.reshape(bsz, t, d)
    return x
```

```python
import functools

import numpy as np
import jax
import jax.numpy as jnp
from jax import lax
from jax.experimental import pallas as pl
from jax.experimental.pallas import tpu as pltpu

F32 = jnp.float32
BF16 = jnp.bfloat16

HEAD_DIM = 64
POOL_WINDOWS = (2, 4, 8, 16)
POOL_GC = 128
POOL_HALO = 16
NSA_HEADS = 8
NSA_KV_GROUPS = 2
NSA_HPG = NSA_HEADS // NSA_KV_GROUPS
N_BRANCH = 3
CMP_LEN = 32
CMP_STRIDE = 16
SEL_LEN = 64
N_SEL = 16
WINDOW = 512
FOX_HEADS = 8
Q_BLOCK = 128
N_EXPERTS = 32
TOP_K = 4
SWIGLU_LIMIT = 7.0
SWIGLU_ALPHA = 1.702
MOE_BLOCK = 256
LN_EPS = 1e-5
NEG_INF = -1e30
SCALE = HEAD_DIM ** -0.5
MASK_BIG = 2.0 ** 100
LOG2E = 1.4426950408889634
ONES_ROWS = 16
LANES = 128
SUBLANES = 8
VMEM_LIMIT = 48 * 1024 * 1024
EXPERT_VMEM_LIMIT = 58 * 1024 * 1024
DMA_ISSUE_UNROLL = 8


def _cparams(sem):
    return pltpu.CompilerParams(dimension_semantics=sem, vmem_limit_bytes=VMEM_LIMIT)


def _split3(x):
    hi = x.astype(BF16)
    r1 = x - hi.astype(F32)
    mid = r1.astype(BF16)
    lo = (r1 - mid.astype(F32)).astype(BF16)
    return hi, mid, lo


def _layer_norm(r, g, b):
    mu = jnp.mean(r, axis=-1, keepdims=True)
    var = jnp.mean(jnp.square(r - mu), axis=-1, keepdims=True)
    return (r - mu) * lax.rsqrt(var + LN_EPS) * g + b


def _load_rows_tiled(ref, rows):
    return jnp.concatenate([ref[pl.ds(s, rows, stride=SUBLANES), :] for s in range(SUBLANES)], axis=1)


def _store_rows_tiled(ref, val):
    rows = val.shape[0]
    for s in range(SUBLANES):
        ref[pl.ds(s, rows, stride=SUBLANES), :] = val[:, s * LANES:(s + 1) * LANES]


def _row_tile_copy(src, src_row, dst, dst_row, sem):
    return pltpu.make_async_copy(src.at[pl.ds(pl.multiple_of(src_row * SUBLANES, SUBLANES), SUBLANES), :],
                                 dst.at[pl.ds(pl.multiple_of(dst_row * SUBLANES, SUBLANES), SUBLANES), :], sem)


def _mm_kernel(x_ref, w_ref, b_ref, o_ref, *, act):
    acc = jnp.dot(x_ref[...].astype(BF16), w_ref[...], preferred_element_type=F32) + b_ref[...]
    if act == "sigmoid":
        acc = jax.nn.sigmoid(acc)
    o_ref[...] = acc.astype(o_ref.dtype)


def _matmul_bias(x, w, b, out_dtype, act=None, tm=1024, tn=512):
    m, k = x.shape
    n = w.shape[1]
    tn = min(tn, n)
    assert m % tm == 0 and n % tn == 0
    return pl.pallas_call(
        functools.partial(_mm_kernel, act=act),
        out_shape=jax.ShapeDtypeStruct((m, n), out_dtype),
        grid=(m // tm, n // tn),
        in_specs=[
            pl.BlockSpec((tm, k), lambda i, j: (i, 0)),
            pl.BlockSpec((k, tn), lambda i, j: (0, j)),
            pl.BlockSpec((1, tn), lambda i, j: (0, j)),
        ],
        out_specs=pl.BlockSpec((tm, tn), lambda i, j: (i, j)),
        compiler_params=_cparams(("parallel", "arbitrary")),
        name="in_proj",
    )(x, w, b.reshape(1, n))


def _mm_t_kernel(x_ref, wt_ref, b_ref, o_ref):
    acc = lax.dot_general(wt_ref[...], x_ref[...].astype(BF16), (((1,), (1,)), ((), ())),
                          preferred_element_type=F32) + b_ref[...]
    o_ref[...] = acc.astype(o_ref.dtype)


def _matmul_bias_t(x, wt, b, tm=1024, tf=896):
    bsz, t, k = x.shape
    f = wt.shape[0]
    assert t % tm == 0 and f % tf == 0
    return pl.pallas_call(
        _mm_t_kernel,
        out_shape=jax.ShapeDtypeStruct((bsz, f, t), BF16),
        grid=(bsz, t // tm, f // tf),
        in_specs=[
            pl.BlockSpec((None, tm, k), lambda bb, i, j: (bb, i, 0)),
            pl.BlockSpec((tf, k), lambda bb, i, j: (j, 0)),
            pl.BlockSpec((tf, 1), lambda bb, i, j: (j, 0)),
        ],
        out_specs=pl.BlockSpec((None, tf, tm), lambda bb, i, j: (bb, j, i)),
        compiler_params=_cparams(("parallel", "parallel", "arbitrary")),
        name="in_proj_t",
    )(x, wt, b.reshape(f, 1))


def _pool_kernel(prev_ref, cur_ref, w_ref, b_ref, sc_ref, o_ref, ext_ref, *, tile):
    i = pl.program_id(1)
    ext_ref[0:POOL_HALO, :] = jnp.where(i > 0, prev_ref[...], 0.0)
    ext_ref[POOL_HALO:POOL_HALO + tile, :] = cur_ref[...]
    t_idx = i * tile + lax.broadcasted_iota(jnp.int32, (tile, 1), 0)
    for gi, win in enumerate(POOL_WINDOWS):
        cols = slice(gi * POOL_GC, (gi + 1) * POOL_GC)
        u = ext_ref[POOL_HALO:POOL_HALO + tile, cols]
        wsum = u
        for j in range(1, win):
            wsum = wsum + ext_ref[POOL_HALO - j:POOL_HALO - j + tile, cols]
        cnt = jnp.minimum(t_idx + 1, win).astype(F32)
        d = wsum / cnt - u
        y = jnp.dot(d.astype(BF16), w_ref[gi], preferred_element_type=F32) + b_ref[gi]
        o_ref[:, cols] = (y * sc_ref[:, cols]).astype(o_ref.dtype)


def _pool_mixer(u, w, b, scale, tile=512):
    bsz, t, c = u.shape
    return pl.pallas_call(
        functools.partial(_pool_kernel, tile=tile),
        out_shape=jax.ShapeDtypeStruct((bsz, t, c), BF16),
        grid=(bsz, t // tile),
        in_specs=[
            pl.BlockSpec((None, POOL_HALO, c),
                         lambda bb, i: (bb, jnp.maximum(i * (tile // POOL_HALO) - 1, 0), 0)),
            pl.BlockSpec((None, tile, c), lambda bb, i: (bb, i, 0)),
            pl.BlockSpec((len(POOL_WINDOWS), POOL_GC, POOL_GC), lambda bb, i: (0, 0, 0)),
            pl.BlockSpec((len(POOL_WINDOWS), 1, POOL_GC), lambda bb, i: (0, 0, 0)),
            pl.BlockSpec((1, c), lambda bb, i: (0, 0)),
        ],
        out_specs=pl.BlockSpec((None, tile, c), lambda bb, i: (bb, i, 0)),
        scratch_shapes=[pltpu.VMEM((POOL_HALO + tile, c), F32)],
        compiler_params=_cparams(("parallel", "arbitrary")),
        name="pool_mixer",
    )(u, u, w.astype(BF16), b.reshape(len(POOL_WINDOWS), 1, POOL_GC), scale.reshape(1, c))


def _compress_kernel(a_ref, pos_ref, w1_ref, b1_ref, w2_ref, b2_ref, o_ref):
    half = CMP_STRIDE * HEAD_DIM
    a = a_ref[...].astype(F32)
    n_chunk = a.shape[0]
    top = (a + pos_ref[:, 0:half]).astype(BF16)
    bot = (a + pos_ref[:, half:2 * half]).astype(BF16)
    p1 = jnp.dot(top, w1_ref[0:half, :], preferred_element_type=F32)
    p2 = jnp.dot(bot, w1_ref[half:2 * half, :], preferred_element_type=F32)
    h = p1 + pltpu.roll(p2, n_chunk - 1, axis=0) + b1_ref[...]
    h = jax.nn.gelu(h)
    o = jnp.dot(h.astype(BF16), w2_ref[...], preferred_element_type=F32) + b2_ref[...]
    o_ref[...] = o.astype(o_ref.dtype)


def _compress(a, pos, w1, b1, w2, b2):
    _, bg, n_chunk, width = a.shape
    hid = w1.shape[-1]
    return pl.pallas_call(
        _compress_kernel,
        out_shape=jax.ShapeDtypeStruct((2, bg, n_chunk, HEAD_DIM), BF16),
        grid=(2, bg),
        in_specs=[
            pl.BlockSpec((None, None, n_chunk, width), lambda s, j: (s, j, 0, 0)),
            pl.BlockSpec((None, 1, 2 * width), lambda s, j: (s, 0, 0)),
            pl.BlockSpec((None, 2 * width, hid), lambda s, j: (s, 0, 0)),
            pl.BlockSpec((None, 1, hid), lambda s, j: (s, 0, 0)),
            pl.BlockSpec((None, hid, HEAD_DIM), lambda s, j: (s, 0, 0)),
            pl.BlockSpec((None, 1, HEAD_DIM), lambda s, j: (s, 0, 0)),
        ],
        out_specs=pl.BlockSpec((None, None, n_chunk, HEAD_DIM), lambda s, j: (s, j, 0, 0)),
        compiler_params=_cparams(("arbitrary", "arbitrary")),
        name="nsa_compress",
    )(a, pos.reshape(2, 1, 2 * width), w1.astype(BF16), b1.reshape(2, 1, hid),
      w2.astype(BF16), b2.reshape(2, 1, HEAD_DIM))


def _flash_steps_t(qts, ks, vts, carries, mask):
    scores = [jnp.dot(k, qt, preferred_element_type=F32) for k, qt in zip(ks, qts)]
    if mask is not None:
        masks = mask if isinstance(mask, (list, tuple)) else [mask] * len(scores)
        scores = [jnp.where(mk, s, NEG_INF) for mk, s in zip(masks, scores)]
    probs, stats = [], []
    for s, (m, _) in zip(scores, carries):
        m_new = jnp.maximum(m, jnp.max(s, axis=0, keepdims=True))
        probs.append(jnp.exp2(s - m_new).astype(BF16))
        stats.append((m_new, jnp.exp2(m - m_new)))
    return tuple((m_new, alpha * acc + jnp.dot(vt, p, preferred_element_type=F32))
                 for vt, p, (m_new, alpha), (_, acc) in zip(vts, probs, stats, carries))


def _flash_init_t(cols):
    return (jnp.full((1, cols), NEG_INF, F32), jnp.zeros((HEAD_DIM + ONES_ROWS, cols), F32))


def _flash_finish_t(acc):
    return acc[0:HEAD_DIM] / acc[HEAD_DIM:HEAD_DIM + 1]


def _with_ones(vt):
    return jnp.concatenate([vt, jnp.ones((ONES_ROWS, vt.shape[1]), vt.dtype)], axis=0)


def _group_queries(qt_ref, g):
    return jnp.concatenate([qt_ref[(g * NSA_HPG + h) * HEAD_DIM:(g * NSA_HPG + h + 1) * HEAD_DIM, :]
                            for h in range(NSA_HPG)], axis=1)


def _slot_rows(x, slot, n_slots):
    z = jnp.zeros_like(x)
    return jnp.concatenate([x if j == slot else z for j in range(n_slots)], axis=0)


def _store_group(o_ref, g, o, col0=0):
    for h in range(NSA_HPG):
        r0 = (g * NSA_HPG + h) * HEAD_DIM
        o_ref[r0:r0 + HEAD_DIM, col0:col0 + Q_BLOCK] = o[:, h * Q_BLOCK:(h + 1) * Q_BLOCK].astype(o_ref.dtype)


def _nsa_cmp_kernel(qt_ref, kc_ref, vct_ref, ovt_ref, ocmp_ref, qaug_ref):
    groups = range(kc_ref.shape[0])
    cols = NSA_HPG * Q_BLOCK
    t0 = pl.program_id(1) * Q_BLOCK
    n_c = kc_ref.shape[1]
    n_s = ovt_ref.shape[0]
    qts = [_group_queries(qt_ref, g) for g in groups]
    tq = t0 + (lax.broadcasted_iota(jnp.int32, (n_c, cols), 1) & (Q_BLOCK - 1))
    c_end = lax.broadcasted_iota(jnp.int32, (n_c, cols), 0) * CMP_STRIDE + (CMP_LEN - 1)
    mask = c_end <= tq
    ps = []
    for g in groups:
        sm = jnp.where(mask, jnp.dot(kc_ref[g], qts[g], preferred_element_type=F32), NEG_INF)
        e = jnp.where(mask, jnp.exp2(sm - jnp.max(sm, axis=0, keepdims=True)), 0.0)
        l = jnp.sum(e, axis=0, keepdims=True)
        ps.append(e / jnp.where(l > 0.0, l, 1.0))
    for g in groups:
        _store_group(ocmp_ref, g, jnp.dot(vct_ref[g], ps[g].astype(BF16), preferred_element_type=F32))

    ovt = ovt_ref[...]
    blk = lax.broadcasted_iota(jnp.int32, (n_s, Q_BLOCK), 0)
    tqq = t0 + lax.broadcasted_iota(jnp.int32, (n_s, Q_BLOCK), 1)
    future = blk * SEL_LEN > tqq
    cur = lax.shift_right_logical(tqq, 6)
    forced = (blk == 0) | (blk == cur) | (blk == cur - 1)
    scores = []
    for g in groups:
        psum = ps[g][:, 0:Q_BLOCK]
        for h in range(1, NSA_HPG):
            psum = psum + ps[g][:, h * Q_BLOCK:(h + 1) * Q_BLOCK]
        imp = None
        for part in _split3(psum):
            term = jnp.dot(ovt, part, preferred_element_type=F32)
            imp = term if imp is None else imp + term
        scores.append(jnp.where(future, -1.0, jnp.where(forced, 1e6, imp)))
    sels = [jnp.zeros((n_s, Q_BLOCK), F32) for _ in groups]
    for _ in range(min(N_SEL, n_s)):
        for g in groups:
            mx = jnp.max(scores[g], axis=0, keepdims=True)
            first = jnp.min(jnp.where(scores[g] == mx, blk, n_s), axis=0, keepdims=True)
            pick = blk == first
            sels[g] = jnp.where(pick, 1.0, sels[g])
            scores[g] = jnp.where(pick, -2.0, scores[g])
    for g in groups:
        sel = jnp.where(future, 0.0, sels[g])
        bias = ((sel - 1.0) * MASK_BIG).astype(BF16)
        for h in range(NSA_HPG):
            qaug_ref[g, 0:n_s, h * Q_BLOCK:(h + 1) * Q_BLOCK] = bias
        qaug_ref[g, n_s:n_s + len(groups) * HEAD_DIM, :] = _slot_rows(qts[g], g, len(groups))


def _nsa_cmp(zt, kc, vct, ovt):
    bsz, _, t = zt.shape
    g, n_c = kc.shape[1], kc.shape[2]
    n_s = ovt.shape[0]
    nq = t // Q_BLOCK
    rows_q = NSA_HEADS * HEAD_DIM
    cols = NSA_HPG * Q_BLOCK
    return pl.pallas_call(
        _nsa_cmp_kernel,
        out_shape=(jax.ShapeDtypeStruct((bsz, rows_q, t), BF16),
                   jax.ShapeDtypeStruct((bsz, g, nq, n_s + g * HEAD_DIM, cols), BF16)),
        grid=(bsz, nq),
        in_specs=[
            pl.BlockSpec((None, rows_q, Q_BLOCK), lambda b, i: (b, 0, i)),
            pl.BlockSpec((None, g, n_c, HEAD_DIM), lambda b, i: (b, 0, 0, 0)),
            pl.BlockSpec((None, g, HEAD_DIM, n_c), lambda b, i: (b, 0, 0, 0)),
            pl.BlockSpec((n_s, n_c), lambda b, i: (0, 0)),
        ],
        out_specs=(
            pl.BlockSpec((None, rows_q, Q_BLOCK), lambda b, i: (b, 0, i)),
            pl.BlockSpec((None, g, None, n_s + g * HEAD_DIM, cols), lambda b, i: (b, 0, i, 0, 0)),
        ),
        compiler_params=_cparams(("parallel", "arbitrary")),
        name="nsa_cmp_select",
    )(zt, kc, vct, ovt)


def _nsa_slc_kernel(qaug_ref, onehot_ref, k_ref, vt_ref, o_ref, *, tk, qb):
    groups = qaug_ref.shape[0]
    cols = NSA_HPG * Q_BLOCK
    t0 = pl.program_id(1) * (qb * Q_BLOCK)
    jd = t0 // tk
    probs = [(g, b) for g in range(groups) for b in range(qb)]

    def chunk(j, carries, mask):
        start = pl.multiple_of(j * tk, tk)
        k = jnp.concatenate([onehot_ref[pl.ds(start, tk), :], k_ref[pl.ds(start, tk), :]], axis=1)
        vts = [_with_ones(vt_ref[g * HEAD_DIM:(g + 1) * HEAD_DIM, pl.ds(start, tk)]) for g in range(groups)]
        return _flash_steps_t([qaug_ref[g, b] for g, b in probs], [k for _ in probs],
                              [vts[g] for g, _ in probs], carries, mask)

    carries = lax.fori_loop(0, jd, lambda j, c: chunk(j, c, None), tuple(_flash_init_t(cols) for _ in probs))
    kpos = jd * tk + lax.broadcasted_iota(jnp.int32, (tk, cols), 0)
    tq = t0 + (lax.broadcasted_iota(jnp.int32, (tk, cols), 1) & (Q_BLOCK - 1))
    causal = [kpos <= tq + b * Q_BLOCK for b in range(qb)]
    carries = chunk(jd, carries, [causal[b] for _, b in probs])
    for (g, b), (_, acc) in zip(probs, carries):
        _store_group(o_ref, g, _flash_finish_t(acc), b * Q_BLOCK)


def _nsa_slc(qaug, onehot, zk, zt, k_blk, vt_blk, tk=512, qb=2):
    bsz, g, nq, wa, cols = qaug.shape
    t = zk.shape[1]
    tk = min(tk, t)
    assert tk % (qb * Q_BLOCK) == 0 and nq % qb == 0 and g * HEAD_DIM == LANES
    return pl.pallas_call(
        functools.partial(_nsa_slc_kernel, tk=tk, qb=qb),
        out_shape=jax.ShapeDtypeStruct((bsz, NSA_HEADS * HEAD_DIM, t), BF16),
        grid=(bsz, nq // qb),
        in_specs=[
            pl.BlockSpec((None, g, qb, wa, cols), lambda b, i: (b, 0, i, 0, 0)),
            pl.BlockSpec(onehot.shape, lambda b, i: (0, 0)),
            pl.BlockSpec((None, t, LANES), lambda b, i: (b, 0, k_blk)),
            pl.BlockSpec((None, LANES, t), lambda b, i: (b, vt_blk, 0)),
        ],
        out_specs=pl.BlockSpec((None, NSA_HEADS * HEAD_DIM, qb * Q_BLOCK), lambda b, i: (b, 0, i)),
        compiler_params=_cparams(("parallel", "arbitrary")),
        name="nsa_selected",
    )(qaug, onehot, zk, zt)


def _nsa_win_kernel(qt_ref, k_ref, vt_ref, o_ref):
    groups = NSA_KV_GROUPS
    cols = NSA_HPG * Q_BLOCK
    span = WINDOW + Q_BLOCK
    t0 = pl.program_id(1) * Q_BLOCK
    start = pl.multiple_of(jnp.maximum(t0 - WINDOW, 0), Q_BLOCK)
    dist = (t0 - start) + (lax.broadcasted_iota(jnp.int32, (span, cols), 1) & (Q_BLOCK - 1)) \
        - lax.broadcasted_iota(jnp.int32, (span, cols), 0)
    mask = (dist >= 0) & (dist < WINDOW)
    k = k_ref[pl.ds(start, span), :]
    scores = [jnp.where(mask, jnp.dot(k, _slot_rows(_group_queries(qt_ref, g), g, groups),
                                      preferred_element_type=F32), NEG_INF) for g in range(groups)]
    probs = [jnp.exp2(s - jnp.max(s, axis=0, keepdims=True)).astype(BF16) for s in scores]
    for g in range(groups):
        vt = _with_ones(vt_ref[g * HEAD_DIM:(g + 1) * HEAD_DIM, pl.ds(start, span)])
        _store_group(o_ref, g, _flash_finish_t(jnp.dot(vt, probs[g], preferred_element_type=F32)))


def _nsa_win(zt, zk, k_blk, vt_blk):
    bsz, _, t = zt.shape
    rows_q = NSA_HEADS * HEAD_DIM
    assert t >= WINDOW + Q_BLOCK and NSA_KV_GROUPS * HEAD_DIM == LANES
    return pl.pallas_call(
        _nsa_win_kernel,
        out_shape=jax.ShapeDtypeStruct((bsz, rows_q, t), BF16),
        grid=(bsz, t // Q_BLOCK),
        in_specs=[
            pl.BlockSpec((None, rows_q, Q_BLOCK), lambda b, i: (b, 0, i)),
            pl.BlockSpec((None, t, LANES), lambda b, i: (b, 0, k_blk)),
            pl.BlockSpec((None, LANES, t), lambda b, i: (b, vt_blk, 0)),
        ],
        out_specs=pl.BlockSpec((None, rows_q, Q_BLOCK), lambda b, i: (b, 0, i)),
        compiler_params=_cparams(("parallel", "arbitrary")),
        name="nsa_window",
    )(zt, zk, zt)


def _fox_prep_kernel(z_ref, tri_ref, o_ref, carry_ref):
    @pl.when(pl.program_id(1) == 0)
    def _():
        carry_ref[...] = jnp.zeros_like(carry_ref)

    lf = jax.nn.log_sigmoid(z_ref[...])
    hi, mid, lo = _split3(lf)
    tri = tri_ref[...]
    cum = (jnp.dot(tri, hi, preferred_element_type=F32) + jnp.dot(tri, mid, preferred_element_type=F32)
           + jnp.dot(tri, lo, preferred_element_type=F32)) + carry_ref[...]
    carry_ref[...] = cum[cum.shape[0] - 1:cum.shape[0], :]
    for part, val in enumerate(_split3(cum * LOG2E)):
        o_ref[part] = val


def _fox_prep(z_small, tile=256):
    bsz, t, _ = z_small.shape
    tri = jnp.asarray(np.tril(np.ones((tile, tile), np.float32)), BF16)
    return pl.pallas_call(
        _fox_prep_kernel,
        out_shape=jax.ShapeDtypeStruct((bsz, 3, t, LANES), BF16),
        grid=(bsz, t // tile),
        in_specs=[
            pl.BlockSpec((None, tile, LANES), lambda b, i: (b, i, 0)),
            pl.BlockSpec((tile, tile), lambda b, i: (0, 0)),
        ],
        out_specs=pl.BlockSpec((None, 3, tile, LANES), lambda b, i: (b, 0, i, 0)),
        scratch_shapes=[pltpu.VMEM((1, LANES), F32)],
        compiler_params=_cparams(("parallel", "arbitrary")),
        name="fox_decay_cumsum",
    )(z_small, tri)


def _fox_kernel(qt_ref, augq_ref, k_ref, augk_ref, vt_ref, o_ref, *, tq, heads):
    i = pl.program_id(2)
    pair_w = LANES // HEAD_DIM
    qts = []
    for h in range(heads):
        q = _slot_rows(qt_ref[h * HEAD_DIM:(h + 1) * HEAD_DIM, :], h % pair_w, pair_w)
        aug = augq_ref[h]
        qts.append(jnp.concatenate([q, aug, jnp.zeros((LANES - aug.shape[0], tq), aug.dtype)], axis=0))

    def chunk(j, carries, mask):
        start = pl.multiple_of(j * tq, tq)
        ks = [jnp.concatenate([k_ref[pl.ds(start, tq), p * LANES:(p + 1) * LANES], augk_ref[p, pl.ds(start, tq), :]],
                              axis=1) for p in range(heads // pair_w)]
        return _flash_steps_t(qts, [ks[h // pair_w] for h in range(heads)],
                              [_with_ones(vt_ref[h * HEAD_DIM:(h + 1) * HEAD_DIM, pl.ds(start, tq)])
                               for h in range(heads)], carries, mask)

    carries = lax.fori_loop(0, i, lambda j, c: chunk(j, c, None), tuple(_flash_init_t(tq) for _ in range(heads)))
    causal = lax.broadcasted_iota(jnp.int32, (tq, tq), 0) <= lax.broadcasted_iota(jnp.int32, (tq, tq), 1)
    carries = chunk(i, carries, causal)
    for h in range(heads):
        o_ref[h * HEAD_DIM:(h + 1) * HEAD_DIM, :] = _flash_finish_t(carries[h][1]).astype(o_ref.dtype)


def _fox_attention(zt, zk, augq, augk, q_row0, k_col0, v_row0, tq=512, heads=4):
    bsz, _, t = zt.shape
    tq = min(tq, t)
    rows = heads * HEAD_DIM
    pairs = heads * HEAD_DIM // LANES
    assert q_row0 % rows == 0 and v_row0 % rows == 0 and k_col0 % rows == 0
    return pl.pallas_call(
        functools.partial(_fox_kernel, tq=tq, heads=heads),
        out_shape=jax.ShapeDtypeStruct((bsz, FOX_HEADS * HEAD_DIM, t), BF16),
        grid=(bsz, FOX_HEADS // heads, t // tq),
        in_specs=[
            pl.BlockSpec((None, rows, tq), lambda b, hh, i: (b, q_row0 // rows + hh, i)),
            pl.BlockSpec((None, heads, augq.shape[2], tq), lambda b, hh, i: (b, hh, 0, i)),
            pl.BlockSpec((None, t, rows), lambda b, hh, i: (b, 0, k_col0 // rows + hh)),
            pl.BlockSpec((None, pairs, t, LANES), lambda b, hh, i: (b, hh, 0, 0)),
            pl.BlockSpec((None, rows, t), lambda b, hh, i: (b, v_row0 // rows + hh, 0)),
        ],
        out_specs=pl.BlockSpec((None, rows, tq), lambda b, hh, i: (b, hh, i)),
        compiler_params=_cparams(("parallel", "parallel", "arbitrary")),
        name="fox_attention",
    )(zt, augq, zk, augk, zt)


def _merge_kernel(x_ref, yp_ref, oc_ref, os_ref, ow_ref, yf_ref, zs_ref, gm_ref, ex_ref,
                  wup_ref, wo_ref, g_ref, b_ref, o_ref, *, alpha):
    mix_w = yp_ref.shape[-1]
    d = x_ref.shape[-1]
    sg = jax.nn.sigmoid(zs_ref[:, 0:LANES])
    hi, mid, lo = _split3(sg)
    ex = ex_ref[...]
    ge = (jnp.dot(hi, ex, preferred_element_type=F32) + jnp.dot(mid, ex, preferred_element_type=F32)
          + jnp.dot(lo, ex, preferred_element_type=F32))
    yn = (ge[:, 0:mix_w] * oc_ref[...].astype(F32).T + ge[:, mix_w:2 * mix_w] * os_ref[...].astype(F32).T
          + ge[:, 2 * mix_w:3 * mix_w] * ow_ref[...].astype(F32).T)
    ys = (yp_ref[...], yn.astype(BF16), yf_ref[...].astype(F32).T.astype(BF16))
    mix = None
    for n in range(N_BRANCH):
        up = jnp.dot(ys[n], wup_ref[n], preferred_element_type=F32)
        term = gm_ref[:, n * d:(n + 1) * d].astype(F32) * up
        mix = term if mix is None else mix + term
    h = jnp.dot(mix.astype(BF16), wo_ref[...], preferred_element_type=F32)
    _store_rows_tiled(o_ref, _layer_norm(alpha * x_ref[...] + h, g_ref[...], b_ref[...]))


def _merge(x, yp, oc, osl, ow, yf, zs, gm, expand, wup, wo, g, b, alpha, tm=512):
    n, d = x.shape
    mix_w = yp.shape[1]
    tiles_per_seq = oc.shape[2] // tm
    tok = lambda width: pl.BlockSpec((tm, width), lambda i: (i, 0))
    feat = pl.BlockSpec((None, mix_w, tm), lambda i: (i // tiles_per_seq, 0, i % tiles_per_seq))
    return pl.pallas_call(
        functools.partial(_merge_kernel, alpha=alpha),
        out_shape=jax.ShapeDtypeStruct((n * SUBLANES, d // SUBLANES), F32),
        grid=(n // tm,),
        in_specs=[
            tok(d), tok(mix_w), feat, feat, feat, feat,
            pl.BlockSpec((tm, 2 * LANES), lambda i: (i, 0)),
            tok(N_BRANCH * d),
            pl.BlockSpec((LANES, N_BRANCH * mix_w), lambda i: (0, 0)),
            pl.BlockSpec((N_BRANCH, mix_w, d), lambda i: (0, 0, 0)),
            pl.BlockSpec((d, d), lambda i: (0, 0)),
            pl.BlockSpec((1, d), lambda i: (0, 0)),
            pl.BlockSpec((1, d), lambda i: (0, 0)),
        ],
        out_specs=pl.BlockSpec((tm * SUBLANES, d // SUBLANES), lambda i: (i, 0)),
        compiler_params=_cparams(("parallel",)),
        name="merge_outproj_ln",
    )(x, yp, oc, osl, ow, yf, zs, gm, expand, wup, wo, g.reshape(1, d), b.reshape(1, d))


def _router_kernel(x_ref, w_ref, b_ref, idx_ref, gate_ref):
    x = _load_rows_tiled(x_ref, idx_ref.shape[0])
    logits = jnp.dot(x, w_ref[...], preferred_element_type=F32,
                     precision=lax.Precision.HIGHEST) + b_ref[...]
    tm, lanes = logits.shape
    lane = lax.broadcasted_iota(jnp.int32, (tm, lanes), 1)
    work = jnp.where(lane < N_EXPERTS, logits, -jnp.inf)
    idx_out = jnp.zeros((tm, lanes), jnp.int32)
    val_out = jnp.zeros((tm, lanes), F32)
    top = None
    denom = jnp.zeros((tm, 1), F32)
    for k in range(TOP_K):
        mx = jnp.max(work, axis=-1, keepdims=True)
        first = jnp.min(jnp.where(work == mx, lane, lanes), axis=-1, keepdims=True)
        if top is None:
            top = mx
        e = jnp.exp(mx - top)
        denom = denom + e
        idx_out = jnp.where(lane == k, first, idx_out)
        val_out = jnp.where(lane == k, e, val_out)
        work = jnp.where(lane == first, -jnp.inf, work)
    idx_ref[...] = idx_out
    gate_ref[...] = val_out / denom


def _router(x, w, b, tm=512):
    n, d = x.shape[0] // SUBLANES, x.shape[1] * SUBLANES
    wp = jnp.zeros((d, LANES), F32).at[:, :N_EXPERTS].set(w)
    bp = jnp.zeros((1, LANES), F32).at[0, :N_EXPERTS].set(b)
    return pl.pallas_call(
        _router_kernel,
        out_shape=(jax.ShapeDtypeStruct((n, LANES), jnp.int32), jax.ShapeDtypeStruct((n, LANES), F32)),
        grid=(n // tm,),
        in_specs=[
            pl.BlockSpec((tm * SUBLANES, d // SUBLANES), lambda i: (i, 0)),
            pl.BlockSpec((d, LANES), lambda i: (0, 0)),
            pl.BlockSpec((1, LANES), lambda i: (0, 0)),
        ],
        out_specs=(pl.BlockSpec((tm, LANES), lambda i: (i, 0)), pl.BlockSpec((tm, LANES), lambda i: (i, 0))),
        compiler_params=_cparams(("parallel",)),
        name="moe_router",
    )(x, wp, bp)


def _start_row_tiles(src_hbm, idx_ref, n, dst, sem, *, per_row=1):
    def body(r, c):
        for k in range(per_row):
            _row_tile_copy(src_hbm, idx_ref[0, r * per_row + k], dst.at[k] if per_row > 1 else dst, r, sem).start()
        return c
    lax.fori_loop(0, n, body, 0, unroll=DMA_ISSUE_UNROLL)


def _wait_buffer(buf, sem):
    pltpu.make_async_copy(buf, buf, sem).wait()


def _expert_kernel(be_ref, nu_ref, tok_cur_ref, tok_next_ref, x_hbm, w1_ref, b1_ref, w2_ref, b2_ref, o_ref,
                   xbuf, w1_bf, w2_bf, sems):
    i = pl.program_id(0)
    n_used = nu_ref[0]
    slot = lax.rem(i, 2)

    @pl.when((i == 0) & (n_used > 0))
    def _():
        _start_row_tiles(x_hbm, tok_cur_ref, MOE_BLOCK, xbuf.at[0], sems.at[0])

    @pl.when(i + 1 < n_used)
    def _():
        _start_row_tiles(x_hbm, tok_next_ref, MOE_BLOCK, xbuf.at[1 - slot], sems.at[1 - slot])

    @pl.when((i < n_used) & ((i == 0) | (be_ref[i] != be_ref[jnp.maximum(i - 1, 0)])))
    def _():
        w1_bf[...] = w1_ref[...].astype(BF16)
        w2_bf[...] = w2_ref[...].astype(BF16)

    @pl.when(i < n_used)
    def _():
        _wait_buffer(xbuf.at[slot], sems.at[slot])
        dff = w2_ref.shape[0]
        x = _load_rows_tiled(xbuf.at[slot], MOE_BLOCK).astype(BF16)
        h = jnp.dot(x, w1_bf[...], preferred_element_type=F32) + b1_ref[...]
        gate = jnp.minimum(h[:, 0:dff], SWIGLU_LIMIT)
        upv = jnp.clip(h[:, dff:2 * dff], -SWIGLU_LIMIT, SWIGLU_LIMIT)
        act = (upv + 1.0) * (gate * jax.nn.sigmoid(SWIGLU_ALPHA * gate))
        _store_rows_tiled(o_ref, jnp.dot(act.astype(BF16), w2_bf[...], preferred_element_type=F32) + b2_ref[...])

    @pl.when(i >= n_used)
    def _():
        o_ref[...] = jnp.zeros_like(o_ref)


def _expert_ffn(x, buf_tok, blk_e, n_used, w1, b1, w2, b2, layer):
    n_blk = buf_tok.shape[0]
    dl = x.shape[1]
    d = dl * SUBLANES
    dff = w2.shape[2]
    depth = w1.shape[0]
    grid_spec = pltpu.PrefetchScalarGridSpec(
        num_scalar_prefetch=2,
        grid=(n_blk,),
        in_specs=[
            pl.BlockSpec((None, 1, MOE_BLOCK), lambda i, be, nu: (i, 0, 0), memory_space=pltpu.SMEM),
            pl.BlockSpec((None, 1, MOE_BLOCK), lambda i, be, nu: (jnp.minimum(i + 1, n_blk - 1), 0, 0),
                         memory_space=pltpu.SMEM),
            pl.BlockSpec(memory_space=pl.ANY),
            pl.BlockSpec((None, None, d, 2 * dff), lambda i, be, nu: (layer, be[i], 0, 0)),
            pl.BlockSpec((None, None, 1, 2 * dff), lambda i, be, nu: (layer, be[i], 0, 0)),
            pl.BlockSpec((None, None, dff, d), lambda i, be, nu: (layer, be[i], 0, 0)),
            pl.BlockSpec((None, None, 1, d), lambda i, be, nu: (layer, be[i], 0, 0)),
        ],
        out_specs=pl.BlockSpec((MOE_BLOCK * SUBLANES, dl), lambda i, be, nu: (i, 0)),
        scratch_shapes=[pltpu.VMEM((2, MOE_BLOCK * SUBLANES, dl), F32), pltpu.VMEM((d, 2 * dff), BF16),
                        pltpu.VMEM((dff, d), BF16), pltpu.SemaphoreType.DMA((2,))],
    )
    return pl.pallas_call(
        _expert_kernel,
        out_shape=jax.ShapeDtypeStruct((n_blk * MOE_BLOCK * SUBLANES, dl), F32),
        grid_spec=grid_spec,
        compiler_params=pltpu.CompilerParams(dimension_semantics=("arbitrary",), vmem_limit_bytes=EXPERT_VMEM_LIMIT,
                                             disable_bounds_checks=True),
        name="moe_experts",
    )(blk_e, n_used, buf_tok, buf_tok, x, w1, b1.reshape(depth, N_EXPERTS, 1, 2 * dff), w2,
      b2.reshape(depth, N_EXPERTS, 1, d))


def _combine_kernel(row_cur_ref, row_next_ref, x_ref, gate_ref, yb_hbm, g_ref, b_ref, o_ref, buf, sems,
                    *, tm, alpha):
    i = pl.program_id(0)
    slot = lax.rem(i, 2)

    @pl.when(i == 0)
    def _():
        _start_row_tiles(yb_hbm, row_cur_ref, tm, buf.at[0], sems.at[0], per_row=TOP_K)

    @pl.when(i + 1 < pl.num_programs(0))
    def _():
        _start_row_tiles(yb_hbm, row_next_ref, tm, buf.at[1 - slot], sems.at[1 - slot], per_row=TOP_K)

    _wait_buffer(buf.at[slot], sems.at[slot])
    gates = gate_ref[...]
    y = gates[:, 0:1] * _load_rows_tiled(buf.at[slot, 0], tm)
    for k in range(1, TOP_K):
        y = y + gates[:, k:k + 1] * _load_rows_tiled(buf.at[slot, k], tm)
    o_ref[...] = _layer_norm(alpha * _load_rows_tiled(x_ref, tm) + y, g_ref[...], b_ref[...])


def _combine(x, gates, row_of, yb, g, b, alpha, tm=256):
    dl = x.shape[1]
    n, d = x.shape[0] // SUBLANES, dl * SUBLANES
    n_tiles = n // tm
    rows = row_of.reshape(n_tiles, 1, tm * TOP_K)
    return pl.pallas_call(
        functools.partial(_combine_kernel, tm=tm, alpha=alpha),
        out_shape=jax.ShapeDtypeStruct((n, d), F32),
        grid=(n_tiles,),
        in_specs=[
            pl.BlockSpec((None, 1, tm * TOP_K), lambda i: (i, 0, 0), memory_space=pltpu.SMEM),
            pl.BlockSpec((None, 1, tm * TOP_K), lambda i: (jnp.minimum(i + 1, n_tiles - 1), 0, 0),
                         memory_space=pltpu.SMEM),
            pl.BlockSpec((tm * SUBLANES, dl), lambda i: (i, 0)),
            pl.BlockSpec((tm, LANES), lambda i: (i, 0)),
            pl.BlockSpec(memory_space=pl.ANY),
            pl.BlockSpec((1, d), lambda i: (0, 0)),
            pl.BlockSpec((1, d), lambda i: (0, 0)),
        ],
        out_specs=pl.BlockSpec((tm, d), lambda i: (i, 0)),
        scratch_shapes=[pltpu.VMEM((2, TOP_K, tm * SUBLANES, dl), F32), pltpu.SemaphoreType.DMA((2,))],
        compiler_params=pltpu.CompilerParams(dimension_semantics=("arbitrary",), vmem_limit_bytes=VMEM_LIMIT,
                                             disable_bounds_checks=True),
        name="moe_combine_ln",
    )(rows, rows, x, gates, yb, g.reshape(1, d), b.reshape(1, d))


def _mixer_layer(x, w_in, b_in, pool_w, pool_b, pool_scale, cmp_pos, cmp_w1, cmp_b1, cmp_w2, cmp_b2,
                 w_up, w_o, ln_g, ln_b, alpha):
    bsz, t, d = x.shape
    n = bsz * t
    mix_w = d // 2
    g, hpg = NSA_KV_GROUPS, NSA_HPG
    nq_w = NSA_HEADS * HEAD_DIM
    nkv_w = N_BRANCH * 2 * g * HEAD_DIM
    nfox_w = 3 * FOX_HEADS * HEAD_DIM
    sizes = (mix_w, nq_w, nkv_w, N_BRANCH * NSA_HEADS, nfox_w, FOX_HEADS, N_BRANCH * d)
    o_pool, o_q, o_kv, o_g, o_fox, o_f, o_gm = (int(v) for v in np.cumsum((0,) + sizes[:-1]))
    n_gate = N_BRANCH * NSA_HEADS

    def cols(a, lo, width):
        return lax.slice_in_dim(a, lo, lo + width, axis=-1)

    qs = SCALE * LOG2E
    hw = FOX_HEADS * HEAD_DIM
    gw = g * HEAD_DIM
    pad_small = 2 * LANES - n_gate - FOX_HEADS
    kv0 = o_kv
    w_small = jnp.concatenate([cols(w_in, o_g, n_gate), cols(w_in, o_f, FOX_HEADS),
                               jnp.zeros((d, pad_small), F32)], axis=1).astype(BF16)
    b_small = jnp.concatenate([cols(b_in, o_g, n_gate), cols(b_in, o_f, FOX_HEADS), jnp.zeros((pad_small,), F32)])
    tok_cols = ((kv0, 2 * gw), (kv0 + 2 * gw, gw), (kv0 + 4 * gw, gw), (o_fox + hw, hw))
    w_k = jnp.concatenate([cols(w_in, lo, wd) for lo, wd in tok_cols], axis=1).astype(BF16)
    b_k = jnp.concatenate([cols(b_in, lo, wd) for lo, wd in tok_cols])
    feat_cols = ((o_q, nq_w, qs), (kv0 + 3 * gw, gw, 1.0), (kv0 + 5 * gw, gw, 1.0), (o_fox, hw, qs),
                 (o_fox + 2 * hw, hw, 1.0))
    w_t = jnp.concatenate([cols(w_in, lo, wd) * sc for lo, wd, sc in feat_cols], axis=1).T.astype(BF16)
    b_t = jnp.concatenate([cols(b_in, lo, wd) * sc for lo, wd, sc in feat_cols])
    r_vslc, r_vwin, r_qfox, r_vfox = nq_w, nq_w + gw, nq_w + 2 * gw, nq_w + 2 * gw + hw
    c_kslc, c_kwin, c_kfox = 2 * gw, 3 * gw, 4 * gw

    xf = x.reshape(n, d)
    u_pool = _matmul_bias(xf, cols(w_in, o_pool, mix_w).astype(BF16), cols(b_in, o_pool, mix_w), F32)
    z_small = _matmul_bias(xf, w_small, b_small, F32)
    zk = _matmul_bias(xf, w_k, b_k, BF16, tn=256).reshape(bsz, t, -1)
    gm = _matmul_bias(xf, cols(w_in, o_gm, N_BRANCH * d).astype(BF16), cols(b_in, o_gm, N_BRANCH * d), BF16,
                      act="sigmoid")
    zt = _matmul_bias_t(x, w_t, b_t)

    y_pool = _pool_mixer(u_pool.reshape(bsz, t, mix_w), pool_w, pool_b, pool_scale).reshape(n, mix_w)

    n_chunk = t // CMP_STRIDE
    a_cmp = cols(zk, 0, 2 * gw).reshape(bsz, n_chunk, CMP_STRIDE, 2, g, HEAD_DIM)
    a_cmp = a_cmp.transpose(3, 0, 4, 1, 2, 5).reshape(2, bsz * g, n_chunk, CMP_STRIDE * HEAD_DIM)
    kvc = _compress(a_cmp, cmp_pos.reshape(2, CMP_LEN * HEAD_DIM), cmp_w1, cmp_b1, cmp_w2, cmp_b2)
    kvc = kvc.reshape(2, bsz, g, n_chunk, HEAD_DIM)
    n_s = t // SEL_LEN
    c_start = np.arange(n_chunk) * CMP_STRIDE
    s_start = np.arange(n_s) * SEL_LEN
    ov = (c_start[:, None] < s_start[None, :] + SEL_LEN) & (c_start[:, None] + CMP_LEN > s_start[None, :])
    ov[n_chunk - (CMP_LEN // CMP_STRIDE) + 1:] = False
    ovt = jnp.asarray(ov.T.astype(np.float32), BF16)
    o_cmp, qaug = _nsa_cmp(zt, kvc[0], kvc[1].transpose(0, 1, 3, 2), ovt)
    onehot = jnp.asarray((np.arange(t)[:, None] // SEL_LEN == np.arange(n_s)[None, :]).astype(np.float32), BF16)
    o_slc = _nsa_slc(qaug, onehot, zk, zt, c_kslc // LANES, r_vslc // LANES)
    o_win = _nsa_win(zt, zk, c_kwin // LANES, r_vwin // LANES)

    cum = _fox_prep(z_small.reshape(bsz, t, 2 * LANES))
    cq = cols(cum, n_gate, FOX_HEADS)
    cq_t = cq.transpose(0, 3, 1, 2)
    ones_t = jnp.ones((bsz, FOX_HEADS, 3, t), BF16)
    zero6 = jnp.zeros((bsz, FOX_HEADS, 6, t), BF16)
    even = (np.arange(FOX_HEADS) % 2 == 0)[None, :, None, None]
    augq = jnp.concatenate([jnp.where(even, jnp.concatenate([cq_t, ones_t], axis=2), zero6),
                            jnp.where(even, zero6, jnp.concatenate([cq_t, ones_t], axis=2)),
                            jnp.zeros((bsz, FOX_HEADS, 4, t), BF16)], axis=2)
    ck = cq.transpose(0, 3, 2, 1).reshape(bsz, FOX_HEADS // 2, 2, t, 3)
    ones_k = jnp.ones((bsz, FOX_HEADS // 2, t, 3), BF16)
    augk = jnp.concatenate([ones_k, -ck[:, :, 0], ones_k, -ck[:, :, 1],
                            jnp.zeros((bsz, FOX_HEADS // 2, t, LANES - 12), BF16)], axis=-1)
    y_fox = _fox_attention(zt, zk, augq, augk, r_qfox, c_kfox, r_vfox)

    hsel = np.arange(NSA_HEADS * HEAD_DIM) // HEAD_DIM
    expand = np.zeros((LANES, N_BRANCH * mix_w), np.float32)
    for br in range(N_BRANCH):
        expand[hsel * N_BRANCH + br, br * mix_w + np.arange(mix_w)] = 1.0
    return _merge(xf, y_pool, o_cmp, o_slc, o_win, y_fox, z_small, gm,
                  jnp.asarray(expand, BF16), w_up.astype(BF16), w_o.astype(BF16), ln_g, ln_b, alpha)


def _moe_layer(x, router_w, router_b, w1, b1, w2, b2, layer, ln_g, ln_b, alpha):
    n = x.shape[0] // SUBLANES
    nk = n * TOP_K
    idx_p, gate_p = _router(x, router_w, router_b)
    e_flat = idx_p[:, :TOP_K].reshape(nk)
    onehot = (e_flat[:, None] == jnp.arange(N_EXPERTS)[None, :]).astype(jnp.int32)
    incl = jnp.cumsum(onehot, axis=0)
    counts = incl[-1]
    rank = jnp.sum((incl - onehot) * onehot, axis=1)
    padded = (counts + MOE_BLOCK - 1) // MOE_BLOCK * MOE_BLOCK
    pend = jnp.cumsum(padded)
    pstart = pend - padded
    dest = pstart[e_flat] + rank
    n_blk = (nk + MOE_BLOCK - 1) // MOE_BLOCK + N_EXPERTS
    p = n_blk * MOE_BLOCK
    buf_tok = jnp.zeros((p,), jnp.int32).at[dest].set(jnp.arange(nk, dtype=jnp.int32) // TOP_K)
    blk_e = jnp.minimum(jnp.sum(jnp.arange(n_blk)[:, None] * MOE_BLOCK >= pend[None, :], axis=1),
                        N_EXPERTS - 1).astype(jnp.int32)
    n_used = (pend[-1] // MOE_BLOCK).astype(jnp.int32).reshape(1)

    yb = _expert_ffn(x, buf_tok.reshape(n_blk, 1, MOE_BLOCK), blk_e, n_used, w1, b1, w2, b2, layer)
    return _combine(x, gate_p, dest.astype(jnp.int32), yb, ln_g, ln_b, alpha)


def kernel(x, w_in, b_in, pool_w, pool_b, pool_scale, cmp_pos, cmp_w1, cmp_b1, cmp_w2, cmp_b2,
           w_up, w_o, ln1_g, ln1_b, router_w, router_b, moe_w1, moe_b1, moe_w2, moe_b2, ln2_g, ln2_b):
    depth = w_in.shape[0]
    alpha = (2 * depth) ** 0.25
    bsz, t, d = x.shape
    for l in range(depth):
        x1 = _mixer_layer(x, w_in[l], b_in[l], pool_w[l], pool_b[l], pool_scale[l], cmp_pos[l], cmp_w1[l],
                          cmp_b1[l], cmp_w2[l], cmp_b2[l], w_up[l], w_o[l], ln1_g[l], ln1_b[l], alpha)
        x2 = _moe_layer(x1, router_w[l], router_b[l], moe_w1, moe_b1, moe_w2, moe_b2, l,
                        ln2_g[l], ln2_b[l], alpha)
        x = x2.reshape(bsz, t, d)
    return x
```

```python
import functools

import numpy as np
import jax
import jax.numpy as jnp
from jax import lax
from jax.experimental import pallas as pl
from jax.experimental.pallas import tpu as pltpu

F32 = jnp.float32
BF16 = jnp.bfloat16

HEAD_DIM = 64
POOL_WINDOWS = (2, 4, 8, 16)
POOL_GC = 128
POOL_HALO = 16
NSA_HEADS = 8
NSA_KV_GROUPS = 2
NSA_HPG = NSA_HEADS // NSA_KV_GROUPS
N_BRANCH = 3
CMP_LEN = 32
CMP_STRIDE = 16
SEL_LEN = 64
N_SEL = 16
WINDOW = 512
FOX_HEADS = 8
Q_BLOCK = 128
N_EXPERTS = 32
TOP_K = 4
SWIGLU_LIMIT = 7.0
SWIGLU_ALPHA = 1.702
MOE_BLOCK = 256
LN_EPS = 1e-5
NEG_INF = -1e30
SCALE = HEAD_DIM ** -0.5
MASK_BIG = 2.0 ** 100
LOG2E = 1.4426950408889634
ONES_ROWS = 16
LANES = 128
SUBLANES = 8
VMEM_LIMIT = 48 * 1024 * 1024
EXPERT_VMEM_LIMIT = 58 * 1024 * 1024
DMA_ISSUE_GROUP = 8


def _cparams(sem):
    return pltpu.CompilerParams(dimension_semantics=sem, vmem_limit_bytes=VMEM_LIMIT)


def _split3(x):
    hi = x.astype(BF16)
    r1 = x - hi.astype(F32)
    mid = r1.astype(BF16)
    lo = (r1 - mid.astype(F32)).astype(BF16)
    return hi, mid, lo


def _layer_norm(r, g, b):
    mu = jnp.mean(r, axis=-1, keepdims=True)
    var = jnp.mean(jnp.square(r - mu), axis=-1, keepdims=True)
    return (r - mu) * lax.rsqrt(var + LN_EPS) * g + b


def _load_rows_tiled(ref, rows):
    return jnp.concatenate([ref[pl.ds(s, rows, stride=SUBLANES), :] for s in range(SUBLANES)], axis=1)


def _store_rows_tiled(ref, val):
    rows = val.shape[0]
    for s in range(SUBLANES):
        ref[pl.ds(s, rows, stride=SUBLANES), :] = val[:, s * LANES:(s + 1) * LANES]


def _row_tile_copy(src, src_row, dst, dst_row, sem):
    return pltpu.make_async_copy(src.at[pl.ds(pl.multiple_of(src_row * SUBLANES, SUBLANES), SUBLANES), :],
                                 dst.at[pl.ds(pl.multiple_of(dst_row * SUBLANES, SUBLANES), SUBLANES), :], sem)


def _mm_kernel(x_ref, w_ref, b_ref, o_ref, *, act):
    acc = jnp.dot(x_ref[...].astype(BF16), w_ref[...], preferred_element_type=F32) + b_ref[...]
    if act == "sigmoid":
        acc = jax.nn.sigmoid(acc)
    o_ref[...] = acc.astype(o_ref.dtype)


def _matmul_bias(x, w, b, out_dtype, act=None, tm=1024, tn=512):
    m, k = x.shape
    n = w.shape[1]
    tn = min(tn, n)
    assert m % tm == 0 and n % tn == 0
    return pl.pallas_call(
        functools.partial(_mm_kernel, act=act),
        out_shape=jax.ShapeDtypeStruct((m, n), out_dtype),
        grid=(m // tm, n // tn),
        in_specs=[
            pl.BlockSpec((tm, k), lambda i, j: (i, 0)),
            pl.BlockSpec((k, tn), lambda i, j: (0, j)),
            pl.BlockSpec((1, tn), lambda i, j: (0, j)),
        ],
        out_specs=pl.BlockSpec((tm, tn), lambda i, j: (i, j)),
        compiler_params=_cparams(("parallel", "arbitrary")),
        name="in_proj",
    )(x, w, b.reshape(1, n))


def _mm_t_kernel(x_ref, wt_ref, b_ref, o_ref):
    acc = lax.dot_general(wt_ref[...], x_ref[...].astype(BF16), (((1,), (1,)), ((), ())),
                          preferred_element_type=F32) + b_ref[...]
    o_ref[...] = acc.astype(o_ref.dtype)


def _matmul_bias_t(x, wt, b, tm=1024, tf=896):
    bsz, t, k = x.shape
    f = wt.shape[0]
    assert t % tm == 0 and f % tf == 0
    return pl.pallas_call(
        _mm_t_kernel,
        out_shape=jax.ShapeDtypeStruct((bsz, f, t), BF16),
        grid=(bsz, t // tm, f // tf),
        in_specs=[
            pl.BlockSpec((None, tm, k), lambda bb, i, j: (bb, i, 0)),
            pl.BlockSpec((tf, k), lambda bb, i, j: (j, 0)),
            pl.BlockSpec((tf, 1), lambda bb, i, j: (j, 0)),
        ],
        out_specs=pl.BlockSpec((None, tf, tm), lambda bb, i, j: (bb, j, i)),
        compiler_params=_cparams(("parallel", "parallel", "arbitrary")),
        name="in_proj_t",
    )(x, wt, b.reshape(f, 1))


def _pool_kernel(prev_ref, cur_ref, w_ref, b_ref, sc_ref, o_ref, ext_ref, *, tile):
    i = pl.program_id(1)
    ext_ref[0:POOL_HALO, :] = jnp.where(i > 0, prev_ref[...], 0.0)
    ext_ref[POOL_HALO:POOL_HALO + tile, :] = cur_ref[...]
    t_idx = i * tile + lax.broadcasted_iota(jnp.int32, (tile, 1), 0)
    for gi, win in enumerate(POOL_WINDOWS):
        cols = slice(gi * POOL_GC, (gi + 1) * POOL_GC)
        u = ext_ref[POOL_HALO:POOL_HALO + tile, cols]
        wsum = u
        for j in range(1, win):
            wsum = wsum + ext_ref[POOL_HALO - j:POOL_HALO - j + tile, cols]
        cnt = jnp.minimum(t_idx + 1, win).astype(F32)
        d = wsum / cnt - u
        y = jnp.dot(d.astype(BF16), w_ref[gi], preferred_element_type=F32) + b_ref[gi]
        o_ref[:, cols] = (y * sc_ref[:, cols]).astype(o_ref.dtype)


def _pool_mixer(u, w, b, scale, tile=512):
    bsz, t, c = u.shape
    return pl.pallas_call(
        functools.partial(_pool_kernel, tile=tile),
        out_shape=jax.ShapeDtypeStruct((bsz, t, c), BF16),
        grid=(bsz, t // tile),
        in_specs=[
            pl.BlockSpec((None, POOL_HALO, c),
                         lambda bb, i: (bb, jnp.maximum(i * (tile // POOL_HALO) - 1, 0), 0)),
            pl.BlockSpec((None, tile, c), lambda bb, i: (bb, i, 0)),
            pl.BlockSpec((len(POOL_WINDOWS), POOL_GC, POOL_GC), lambda bb, i: (0, 0, 0)),
            pl.BlockSpec((len(POOL_WINDOWS), 1, POOL_GC), lambda bb, i: (0, 0, 0)),
            pl.BlockSpec((1, c), lambda bb, i: (0, 0)),
        ],
        out_specs=pl.BlockSpec((None, tile, c), lambda bb, i: (bb, i, 0)),
        scratch_shapes=[pltpu.VMEM((POOL_HALO + tile, c), F32)],
        compiler_params=_cparams(("parallel", "arbitrary")),
        name="pool_mixer",
    )(u, u, w.astype(BF16), b.reshape(len(POOL_WINDOWS), 1, POOL_GC), scale.reshape(1, c))


def _compress_kernel(a_ref, pos_ref, w1_ref, b1_ref, w2_ref, b2_ref, o_ref):
    half = CMP_STRIDE * HEAD_DIM
    a = a_ref[...].astype(F32)
    n_chunk = a.shape[0]
    top = (a + pos_ref[:, 0:half]).astype(BF16)
    bot = (a + pos_ref[:, half:2 * half]).astype(BF16)
    p1 = jnp.dot(top, w1_ref[0:half, :], preferred_element_type=F32)
    p2 = jnp.dot(bot, w1_ref[half:2 * half, :], preferred_element_type=F32)
    h = p1 + pltpu.roll(p2, n_chunk - 1, axis=0) + b1_ref[...]
    h = jax.nn.gelu(h)
    o = jnp.dot(h.astype(BF16), w2_ref[...], preferred_element_type=F32) + b2_ref[...]
    o_ref[...] = o.astype(o_ref.dtype)


def _compress(a, pos, w1, b1, w2, b2):
    _, bg, n_chunk, width = a.shape
    hid = w1.shape[-1]
    return pl.pallas_call(
        _compress_kernel,
        out_shape=jax.ShapeDtypeStruct((2, bg, n_chunk, HEAD_DIM), BF16),
        grid=(2, bg),
        in_specs=[
            pl.BlockSpec((None, None, n_chunk, width), lambda s, j: (s, j, 0, 0)),
            pl.BlockSpec((None, 1, 2 * width), lambda s, j: (s, 0, 0)),
            pl.BlockSpec((None, 2 * width, hid), lambda s, j: (s, 0, 0)),
            pl.BlockSpec((None, 1, hid), lambda s, j: (s, 0, 0)),
            pl.BlockSpec((None, hid, HEAD_DIM), lambda s, j: (s, 0, 0)),
            pl.BlockSpec((None, 1, HEAD_DIM), lambda s, j: (s, 0, 0)),
        ],
        out_specs=pl.BlockSpec((None, None, n_chunk, HEAD_DIM), lambda s, j: (s, j, 0, 0)),
        compiler_params=_cparams(("arbitrary", "arbitrary")),
        name="nsa_compress",
    )(a, pos.reshape(2, 1, 2 * width), w1.astype(BF16), b1.reshape(2, 1, hid),
      w2.astype(BF16), b2.reshape(2, 1, HEAD_DIM))


def _flash_steps_t(qts, ks, vts, carries, mask):
    scores = [jnp.dot(k, qt, preferred_element_type=F32) for k, qt in zip(ks, qts)]
    if mask is not None:
        masks = mask if isinstance(mask, (list, tuple)) else [mask] * len(scores)
        scores = [jnp.where(mk, s, NEG_INF) for mk, s in zip(masks, scores)]
    probs, stats = [], []
    for s, (m, _) in zip(scores, carries):
        m_new = jnp.maximum(m, jnp.max(s, axis=0, keepdims=True))
        probs.append(jnp.exp2(s - m_new).astype(BF16))
        stats.append((m_new, jnp.exp2(m - m_new)))
    return tuple((m_new, alpha * acc + jnp.dot(vt, p, preferred_element_type=F32))
                 for vt, p, (m_new, alpha), (_, acc) in zip(vts, probs, stats, carries))


def _flash_init_t(cols):
    return (jnp.full((1, cols), NEG_INF, F32), jnp.zeros((HEAD_DIM + ONES_ROWS, cols), F32))


def _flash_finish_t(acc):
    return acc[0:HEAD_DIM] / acc[HEAD_DIM:HEAD_DIM + 1]


def _with_ones(vt):
    return jnp.concatenate([vt, jnp.ones((ONES_ROWS, vt.shape[1]), vt.dtype)], axis=0)


def _group_queries(qt_ref, g):
    return jnp.concatenate([qt_ref[(g * NSA_HPG + h) * HEAD_DIM:(g * NSA_HPG + h + 1) * HEAD_DIM, :]
                            for h in range(NSA_HPG)], axis=1)


def _slot_rows(x, slot, n_slots):
    z = jnp.zeros_like(x)
    return jnp.concatenate([x if j == slot else z for j in range(n_slots)], axis=0)


def _store_group(o_ref, g, o, col0=0):
    for h in range(NSA_HPG):
        r0 = (g * NSA_HPG + h) * HEAD_DIM
        o_ref[r0:r0 + HEAD_DIM, col0:col0 + Q_BLOCK] = o[:, h * Q_BLOCK:(h + 1) * Q_BLOCK].astype(o_ref.dtype)


def _nsa_cmp_kernel(qt_ref, kc_ref, vct_ref, ovt_ref, ocmp_ref, qaug_ref):
    groups = range(kc_ref.shape[0])
    cols = NSA_HPG * Q_BLOCK
    t0 = pl.program_id(1) * Q_BLOCK
    n_c = kc_ref.shape[1]
    n_s = ovt_ref.shape[0]
    qts = [_group_queries(qt_ref, g) for g in groups]
    tq = t0 + (lax.broadcasted_iota(jnp.int32, (n_c, cols), 1) & (Q_BLOCK - 1))
    c_end = lax.broadcasted_iota(jnp.int32, (n_c, cols), 0) * CMP_STRIDE + (CMP_LEN - 1)
    mask = c_end <= tq
    ps = []
    for g in groups:
        sm = jnp.where(mask, jnp.dot(kc_ref[g], qts[g], preferred_element_type=F32), NEG_INF)
        e = jnp.where(mask, jnp.exp2(sm - jnp.max(sm, axis=0, keepdims=True)), 0.0)
        l = jnp.sum(e, axis=0, keepdims=True)
        ps.append(e / jnp.where(l > 0.0, l, 1.0))
    for g in groups:
        _store_group(ocmp_ref, g, jnp.dot(vct_ref[g], ps[g].astype(BF16), preferred_element_type=F32))

    ovt = ovt_ref[...]
    blk = lax.broadcasted_iota(jnp.int32, (n_s, Q_BLOCK), 0)
    tqq = t0 + lax.broadcasted_iota(jnp.int32, (n_s, Q_BLOCK), 1)
    future = blk * SEL_LEN > tqq
    cur = lax.shift_right_logical(tqq, 6)
    forced = (blk == 0) | (blk == cur) | (blk == cur - 1)
    scores = []
    for g in groups:
        psum = ps[g][:, 0:Q_BLOCK]
        for h in range(1, NSA_HPG):
            psum = psum + ps[g][:, h * Q_BLOCK:(h + 1) * Q_BLOCK]
        imp = None
        for part in _split3(psum):
            term = jnp.dot(ovt, part, preferred_element_type=F32)
            imp = term if imp is None else imp + term
        scores.append(jnp.where(future, -1.0, jnp.where(forced, 1e6, imp)))
    sels = [jnp.zeros((n_s, Q_BLOCK), F32) for _ in groups]
    for _ in range(min(N_SEL, n_s)):
        for g in groups:
            mx = jnp.max(scores[g], axis=0, keepdims=True)
            first = jnp.min(jnp.where(scores[g] == mx, blk, n_s), axis=0, keepdims=True)
            pick = blk == first
            sels[g] = jnp.where(pick, 1.0, sels[g])
            scores[g] = jnp.where(pick, -2.0, scores[g])
    for g in groups:
        sel = jnp.where(future, 0.0, sels[g])
        bias = ((sel - 1.0) * MASK_BIG).astype(BF16)
        for h in range(NSA_HPG):
            qaug_ref[g, 0:n_s, h * Q_BLOCK:(h + 1) * Q_BLOCK] = bias
        qaug_ref[g, n_s:n_s + len(groups) * HEAD_DIM, :] = _slot_rows(qts[g], g, len(groups))


def _nsa_cmp(zt, kc, vct, ovt):
    bsz, _, t = zt.shape
    g, n_c = kc.shape[1], kc.shape[2]
    n_s = ovt.shape[0]
    nq = t // Q_BLOCK
    rows_q = NSA_HEADS * HEAD_DIM
    cols = NSA_HPG * Q_BLOCK
    return pl.pallas_call(
        _nsa_cmp_kernel,
        out_shape=(jax.ShapeDtypeStruct((bsz, rows_q, t), BF16),
                   jax.ShapeDtypeStruct((bsz, g, nq, n_s + g * HEAD_DIM, cols), BF16)),
        grid=(bsz, nq),
        in_specs=[
            pl.BlockSpec((None, rows_q, Q_BLOCK), lambda b, i: (b, 0, i)),
            pl.BlockSpec((None, g, n_c, HEAD_DIM), lambda b, i: (b, 0, 0, 0)),
            pl.BlockSpec((None, g, HEAD_DIM, n_c), lambda b, i: (b, 0, 0, 0)),
            pl.BlockSpec((n_s, n_c), lambda b, i: (0, 0)),
        ],
        out_specs=(
            pl.BlockSpec((None, rows_q, Q_BLOCK), lambda b, i: (b, 0, i)),
            pl.BlockSpec((None, g, None, n_s + g * HEAD_DIM, cols), lambda b, i: (b, 0, i, 0, 0)),
        ),
        compiler_params=_cparams(("parallel", "arbitrary")),
        name="nsa_cmp_select",
    )(zt, kc, vct, ovt)


def _nsa_slc_kernel(qaug_ref, onehot_ref, k_ref, vt_ref, o_ref, *, tk, qb):
    groups = qaug_ref.shape[0]
    cols = NSA_HPG * Q_BLOCK
    t0 = pl.program_id(1) * (qb * Q_BLOCK)
    jd = t0 // tk
    probs = [(g, b) for g in range(groups) for b in range(qb)]

    def chunk(j, carries, mask):
        start = pl.multiple_of(j * tk, tk)
        k = jnp.concatenate([onehot_ref[pl.ds(start, tk), :], k_ref[pl.ds(start, tk), :]], axis=1)
        vts = [_with_ones(vt_ref[g * HEAD_DIM:(g + 1) * HEAD_DIM, pl.ds(start, tk)]) for g in range(groups)]
        return _flash_steps_t([qaug_ref[g, b] for g, b in probs], [k for _ in probs],
                              [vts[g] for g, _ in probs], carries, mask)

    carries = lax.fori_loop(0, jd, lambda j, c: chunk(j, c, None), tuple(_flash_init_t(cols) for _ in probs))
    kpos = jd * tk + lax.broadcasted_iota(jnp.int32, (tk, cols), 0)
    tq = t0 + (lax.broadcasted_iota(jnp.int32, (tk, cols), 1) & (Q_BLOCK - 1))
    causal = [kpos <= tq + b * Q_BLOCK for b in range(qb)]
    carries = chunk(jd, carries, [causal[b] for _, b in probs])
    for (g, b), (_, acc) in zip(probs, carries):
        _store_group(o_ref, g, _flash_finish_t(acc), b * Q_BLOCK)


def _nsa_slc(qaug, onehot, zk, zt, k_blk, vt_blk, tk=512, qb=2):
    bsz, g, nq, wa, cols = qaug.shape
    t = zk.shape[1]
    tk = min(tk, t)
    assert tk % (qb * Q_BLOCK) == 0 and nq % qb == 0 and g * HEAD_DIM == LANES
    return pl.pallas_call(
        functools.partial(_nsa_slc_kernel, tk=tk, qb=qb),
        out_shape=jax.ShapeDtypeStruct((bsz, NSA_HEADS * HEAD_DIM, t), BF16),
        grid=(bsz, nq // qb),
        in_specs=[
            pl.BlockSpec((None, g, qb, wa, cols), lambda b, i: (b, 0, i, 0, 0)),
            pl.BlockSpec(onehot.shape, lambda b, i: (0, 0)),
            pl.BlockSpec((None, t, LANES), lambda b, i: (b, 0, k_blk)),
            pl.BlockSpec((None, LANES, t), lambda b, i: (b, vt_blk, 0)),
        ],
        out_specs=pl.BlockSpec((None, NSA_HEADS * HEAD_DIM, qb * Q_BLOCK), lambda b, i: (b, 0, i)),
        compiler_params=_cparams(("parallel", "arbitrary")),
        name="nsa_selected",
    )(qaug, onehot, zk, zt)


def _nsa_win_kernel(qt_ref, k_ref, vt_ref, o_ref):
    groups = NSA_KV_GROUPS
    cols = NSA_HPG * Q_BLOCK
    span = WINDOW + Q_BLOCK
    t0 = pl.program_id(1) * Q_BLOCK
    start = pl.multiple_of(jnp.maximum(t0 - WINDOW, 0), Q_BLOCK)
    dist = (t0 - start) + (lax.broadcasted_iota(jnp.int32, (span, cols), 1) & (Q_BLOCK - 1)) \
        - lax.broadcasted_iota(jnp.int32, (span, cols), 0)
    mask = (dist >= 0) & (dist < WINDOW)
    k = k_ref[pl.ds(start, span), :]
    scores = [jnp.where(mask, jnp.dot(k, _slot_rows(_group_queries(qt_ref, g), g, groups),
                                      preferred_element_type=F32), NEG_INF) for g in range(groups)]
    probs = [jnp.exp2(s - jnp.max(s, axis=0, keepdims=True)).astype(BF16) for s in scores]
    for g in range(groups):
        vt = _with_ones(vt_ref[g * HEAD_DIM:(g + 1) * HEAD_DIM, pl.ds(start, span)])
        _store_group(o_ref, g, _flash_finish_t(jnp.dot(vt, probs[g], preferred_element_type=F32)))


def _nsa_win(zt, zk, k_blk, vt_blk):
    bsz, _, t = zt.shape
    rows_q = NSA_HEADS * HEAD_DIM
    assert t >= WINDOW + Q_BLOCK and NSA_KV_GROUPS * HEAD_DIM == LANES
    return pl.pallas_call(
        _nsa_win_kernel,
        out_shape=jax.ShapeDtypeStruct((bsz, rows_q, t), BF16),
        grid=(bsz, t // Q_BLOCK),
        in_specs=[
            pl.BlockSpec((None, rows_q, Q_BLOCK), lambda b, i: (b, 0, i)),
            pl.BlockSpec((None, t, LANES), lambda b, i: (b, 0, k_blk)),
            pl.BlockSpec((None, LANES, t), lambda b, i: (b, vt_blk, 0)),
        ],
        out_specs=pl.BlockSpec((None, rows_q, Q_BLOCK), lambda b, i: (b, 0, i)),
        compiler_params=_cparams(("parallel", "arbitrary")),
        name="nsa_window",
    )(zt, zk, zt)


def _fox_prep_kernel(z_ref, tri_ref, o_ref, carry_ref):
    @pl.when(pl.program_id(1) == 0)
    def _():
        carry_ref[...] = jnp.zeros_like(carry_ref)

    lf = jax.nn.log_sigmoid(z_ref[...])
    hi, mid, lo = _split3(lf)
    tri = tri_ref[...]
    cum = (jnp.dot(tri, hi, preferred_element_type=F32) + jnp.dot(tri, mid, preferred_element_type=F32)
           + jnp.dot(tri, lo, preferred_element_type=F32)) + carry_ref[...]
    carry_ref[...] = cum[cum.shape[0] - 1:cum.shape[0], :]
    for part, val in enumerate(_split3(cum * LOG2E)):
        o_ref[part] = val


def _fox_prep(z_small, tile=256):
    bsz, t, _ = z_small.shape
    tri = jnp.asarray(np.tril(np.ones((tile, tile), np.float32)), BF16)
    return pl.pallas_call(
        _fox_prep_kernel,
        out_shape=jax.ShapeDtypeStruct((bsz, 3, t, LANES), BF16),
        grid=(bsz, t // tile),
        in_specs=[
            pl.BlockSpec((None, tile, LANES), lambda b, i: (b, i, 0)),
            pl.BlockSpec((tile, tile), lambda b, i: (0, 0)),
        ],
        out_specs=pl.BlockSpec((None, 3, tile, LANES), lambda b, i: (b, 0, i, 0)),
        scratch_shapes=[pltpu.VMEM((1, LANES), F32)],
        compiler_params=_cparams(("parallel", "arbitrary")),
        name="fox_decay_cumsum",
    )(z_small, tri)


def _fox_kernel(qt_ref, augq_ref, k_ref, augk_ref, vt_ref, o_ref, *, tq, heads):
    i = pl.program_id(2)
    pair_w = LANES // HEAD_DIM
    qts = []
    for h in range(heads):
        q = _slot_rows(qt_ref[h * HEAD_DIM:(h + 1) * HEAD_DIM, :], h % pair_w, pair_w)
        aug = augq_ref[h]
        qts.append(jnp.concatenate([q, aug, jnp.zeros((LANES - aug.shape[0], tq), aug.dtype)], axis=0))

    def chunk(j, carries, mask):
        start = pl.multiple_of(j * tq, tq)
        ks = [jnp.concatenate([k_ref[pl.ds(start, tq), p * LANES:(p + 1) * LANES], augk_ref[p, pl.ds(start, tq), :]],
                              axis=1) for p in range(heads // pair_w)]
        return _flash_steps_t(qts, [ks[h // pair_w] for h in range(heads)],
                              [_with_ones(vt_ref[h * HEAD_DIM:(h + 1) * HEAD_DIM, pl.ds(start, tq)])
                               for h in range(heads)], carries, mask)

    carries = lax.fori_loop(0, i, lambda j, c: chunk(j, c, None), tuple(_flash_init_t(tq) for _ in range(heads)))
    causal = lax.broadcasted_iota(jnp.int32, (tq, tq), 0) <= lax.broadcasted_iota(jnp.int32, (tq, tq), 1)
    carries = chunk(i, carries, causal)
    for h in range(heads):
        o_ref[h * HEAD_DIM:(h + 1) * HEAD_DIM, :] = _flash_finish_t(carries[h][1]).astype(o_ref.dtype)


def _fox_attention(zt, zk, augq, augk, q_row0, k_col0, v_row0, tq=512, heads=4):
    bsz, _, t = zt.shape
    tq = min(tq, t)
    rows = heads * HEAD_DIM
    pairs = heads * HEAD_DIM // LANES
    assert q_row0 % rows == 0 and v_row0 % rows == 0 and k_col0 % rows == 0
    return pl.pallas_call(
        functools.partial(_fox_kernel, tq=tq, heads=heads),
        out_shape=jax.ShapeDtypeStruct((bsz, FOX_HEADS * HEAD_DIM, t), BF16),
        grid=(bsz, FOX_HEADS // heads, t // tq),
        in_specs=[
            pl.BlockSpec((None, rows, tq), lambda b, hh, i: (b, q_row0 // rows + hh, i)),
            pl.BlockSpec((None, heads, augq.shape[2], tq), lambda b, hh, i: (b, hh, 0, i)),
            pl.BlockSpec((None, t, rows), lambda b, hh, i: (b, 0, k_col0 // rows + hh)),
            pl.BlockSpec((None, pairs, t, LANES), lambda b, hh, i: (b, hh, 0, 0)),
            pl.BlockSpec((None, rows, t), lambda b, hh, i: (b, v_row0 // rows + hh, 0)),
        ],
        out_specs=pl.BlockSpec((None, rows, tq), lambda b, hh, i: (b, hh, i)),
        compiler_params=_cparams(("parallel", "parallel", "arbitrary")),
        name="fox_attention",
    )(zt, augq, zk, augk, zt)


def _merge_kernel(x_ref, yp_ref, oc_ref, os_ref, ow_ref, yf_ref, zs_ref, gm_ref, ex_ref,
                  wup_ref, wo_ref, g_ref, b_ref, o_ref, *, alpha):
    mix_w = yp_ref.shape[-1]
    d = x_ref.shape[-1]
    sg = jax.nn.sigmoid(zs_ref[:, 0:LANES])
    hi, mid, lo = _split3(sg)
    ex = ex_ref[...]
    ge = (jnp.dot(hi, ex, preferred_element_type=F32) + jnp.dot(mid, ex, preferred_element_type=F32)
          + jnp.dot(lo, ex, preferred_element_type=F32))
    yn = (ge[:, 0:mix_w] * oc_ref[...].astype(F32).T + ge[:, mix_w:2 * mix_w] * os_ref[...].astype(F32).T
          + ge[:, 2 * mix_w:3 * mix_w] * ow_ref[...].astype(F32).T)
    ys = (yp_ref[...], yn.astype(BF16), yf_ref[...].astype(F32).T.astype(BF16))
    mix = None
    for n in range(N_BRANCH):
        up = jnp.dot(ys[n], wup_ref[n], preferred_element_type=F32)
        term = gm_ref[:, n * d:(n + 1) * d].astype(F32) * up
        mix = term if mix is None else mix + term
    h = jnp.dot(mix.astype(BF16), wo_ref[...], preferred_element_type=F32)
    _store_rows_tiled(o_ref, _layer_norm(alpha * x_ref[...] + h, g_ref[...], b_ref[...]))


def _merge(x, yp, oc, osl, ow, yf, zs, gm, expand, wup, wo, g, b, alpha, tm=512):
    n, d = x.shape
    mix_w = yp.shape[1]
    tiles_per_seq = oc.shape[2] // tm
    tok = lambda width: pl.BlockSpec((tm, width), lambda i: (i, 0))
    feat = pl.BlockSpec((None, mix_w, tm), lambda i: (i // tiles_per_seq, 0, i % tiles_per_seq))
    return pl.pallas_call(
        functools.partial(_merge_kernel, alpha=alpha),
        out_shape=jax.ShapeDtypeStruct((n * SUBLANES, d // SUBLANES), F32),
        grid=(n // tm,),
        in_specs=[
            tok(d), tok(mix_w), feat, feat, feat, feat,
            pl.BlockSpec((tm, 2 * LANES), lambda i: (i, 0)),
            tok(N_BRANCH * d),
            pl.BlockSpec((LANES, N_BRANCH * mix_w), lambda i: (0, 0)),
            pl.BlockSpec((N_BRANCH, mix_w, d), lambda i: (0, 0, 0)),
            pl.BlockSpec((d, d), lambda i: (0, 0)),
            pl.BlockSpec((1, d), lambda i: (0, 0)),
            pl.BlockSpec((1, d), lambda i: (0, 0)),
        ],
        out_specs=pl.BlockSpec((tm * SUBLANES, d // SUBLANES), lambda i: (i, 0)),
        compiler_params=_cparams(("parallel",)),
        name="merge_outproj_ln",
    )(x, yp, oc, osl, ow, yf, zs, gm, expand, wup, wo, g.reshape(1, d), b.reshape(1, d))


def _router_kernel(x_ref, w_ref, b_ref, idx_ref, gate_ref):
    x = _load_rows_tiled(x_ref, idx_ref.shape[0])
    logits = jnp.dot(x, w_ref[...], preferred_element_type=F32,
                     precision=lax.Precision.HIGHEST) + b_ref[...]
    tm, lanes = logits.shape
    lane = lax.broadcasted_iota(jnp.int32, (tm, lanes), 1)
    work = jnp.where(lane < N_EXPERTS, logits, -jnp.inf)
    idx_out = jnp.zeros((tm, lanes), jnp.int32)
    val_out = jnp.zeros((tm, lanes), F32)
    top = None
    denom = jnp.zeros((tm, 1), F32)
    for k in range(TOP_K):
        mx = jnp.max(work, axis=-1, keepdims=True)
        first = jnp.min(jnp.where(work == mx, lane, lanes), axis=-1, keepdims=True)
        if top is None:
            top = mx
        e = jnp.exp(mx - top)
        denom = denom + e
        idx_out = jnp.where(lane == k, first, idx_out)
        val_out = jnp.where(lane == k, e, val_out)
        work = jnp.where(lane == first, -jnp.inf, work)
    idx_ref[...] = idx_out
    gate_ref[...] = val_out / denom


def _router(x, w, b, tm=512):
    n, d = x.shape[0] // SUBLANES, x.shape[1] * SUBLANES
    wp = jnp.zeros((d, LANES), F32).at[:, :N_EXPERTS].set(w)
    bp = jnp.zeros((1, LANES), F32).at[0, :N_EXPERTS].set(b)
    return pl.pallas_call(
        _router_kernel,
        out_shape=(jax.ShapeDtypeStruct((n, LANES), jnp.int32), jax.ShapeDtypeStruct((n, LANES), F32)),
        grid=(n // tm,),
        in_specs=[
            pl.BlockSpec((tm * SUBLANES, d // SUBLANES), lambda i: (i, 0)),
            pl.BlockSpec((d, LANES), lambda i: (0, 0)),
            pl.BlockSpec((1, LANES), lambda i: (0, 0)),
        ],
        out_specs=(pl.BlockSpec((tm, LANES), lambda i: (i, 0)), pl.BlockSpec((tm, LANES), lambda i: (i, 0))),
        compiler_params=_cparams(("parallel",)),
        name="moe_router",
    )(x, wp, bp)


def _start_row_tiles(src_hbm, idx_ref, n, dst, sem, *, per_row=1):
    def body(gi, c):
        base = gi * DMA_ISSUE_GROUP
        idx = [idx_ref[0, (base + j) * per_row + k] for j in range(DMA_ISSUE_GROUP) for k in range(per_row)]
        for j in range(DMA_ISSUE_GROUP):
            for k in range(per_row):
                _row_tile_copy(src_hbm, idx[j * per_row + k], dst.at[k] if per_row > 1 else dst, base + j, sem).start()
        return c
    lax.fori_loop(0, n // DMA_ISSUE_GROUP, body, 0)


def _wait_buffer(buf, sem):
    pltpu.make_async_copy(buf, buf, sem).wait()


def _dispatch_kernel(pend_ref, nu_ref, dest_ref, x_hbm, xs_hbm, zeros, sems, *, tm, n_blk):
    i = pl.program_id(0)
    slot = lax.rem(i, 2)
    blk_rows = MOE_BLOCK * SUBLANES

    def zero_block(blk):
        return pltpu.make_async_copy(zeros, xs_hbm.at[pl.ds(pl.multiple_of(blk * blk_rows, blk_rows), blk_rows), :],
                                     sems.at[2])

    def last_block(e):
        prev = jnp.where(e > 0, pend_ref[jnp.maximum(e - 1, 0)], 0)
        return pend_ref[e] > prev, pend_ref[e] // MOE_BLOCK - 1

    @pl.when(i == 0)
    def _():
        zeros[...] = jnp.zeros_like(zeros)
        for wait in (False, True):
            def per_expert(e, c):
                used, blk = last_block(e)

                @pl.when(used)
                def _():
                    zero_block(blk).wait() if wait else zero_block(blk).start()
                return c

            def per_tail(blk, c):
                zero_block(blk).wait() if wait else zero_block(blk).start()
                return c
            lax.fori_loop(0, N_EXPERTS, per_expert, 0)
            lax.fori_loop(nu_ref[0], n_blk, per_tail, 0)

    def body(gi, c):
        base = gi * DMA_ISSUE_GROUP
        idx = [dest_ref[0, (base + j) * TOP_K + k] for j in range(DMA_ISSUE_GROUP) for k in range(TOP_K)]
        for j in range(DMA_ISSUE_GROUP):
            for k in range(TOP_K):
                _row_tile_copy(x_hbm, i * tm + base + j, xs_hbm, idx[j * TOP_K + k], sems.at[slot]).start()
        return c
    lax.fori_loop(0, tm // DMA_ISSUE_GROUP, body, 0)

    def wait_tile(s):
        done = xs_hbm.at[pl.ds(0, tm * TOP_K * SUBLANES), :]
        pltpu.make_async_copy(done, done, sems.at[s]).wait()

    @pl.when(i > 0)
    def _():
        wait_tile(1 - slot)

    @pl.when(i == pl.num_programs(0) - 1)
    def _():
        wait_tile(slot)


def _dispatch(x, dest, pend, n_used, n_blk, tm=256):
    dl = x.shape[1]
    n = x.shape[0] // SUBLANES
    n_tiles = n // tm
    grid_spec = pltpu.PrefetchScalarGridSpec(
        num_scalar_prefetch=2,
        grid=(n_tiles,),
        in_specs=[
            pl.BlockSpec((None, 1, tm * TOP_K), lambda i, pe, nu: (i, 0, 0), memory_space=pltpu.SMEM),
            pl.BlockSpec(memory_space=pl.ANY),
        ],
        out_specs=pl.BlockSpec(memory_space=pl.ANY),
        scratch_shapes=[pltpu.VMEM((MOE_BLOCK * SUBLANES, dl), F32), pltpu.SemaphoreType.DMA((3,))],
    )
    return pl.pallas_call(
        functools.partial(_dispatch_kernel, tm=tm, n_blk=n_blk),
        out_shape=jax.ShapeDtypeStruct((n_blk * MOE_BLOCK * SUBLANES, dl), F32),
        grid_spec=grid_spec,
        compiler_params=pltpu.CompilerParams(dimension_semantics=("arbitrary",), vmem_limit_bytes=VMEM_LIMIT,
                                             disable_bounds_checks=True),
        name="moe_dispatch",
    )(pend, n_used, dest.reshape(n_tiles, 1, tm * TOP_K), x)


def _expert_kernel(be_ref, nu_ref, x_ref, w1_ref, b1_ref, w2_ref, b2_ref, o_ref, w1_bf, w2_bf):
    i = pl.program_id(0)
    n_used = nu_ref[0]

    @pl.when((i < n_used) & ((i == 0) | (be_ref[i] != be_ref[jnp.maximum(i - 1, 0)])))
    def _():
        w1_bf[...] = w1_ref[...].astype(BF16)
        w2_bf[...] = w2_ref[...].astype(BF16)

    @pl.when(i < n_used)
    def _():
        dff = w2_ref.shape[0]
        x = _load_rows_tiled(x_ref, MOE_BLOCK).astype(BF16)
        h = jnp.dot(x, w1_bf[...], preferred_element_type=F32) + b1_ref[...]
        gate = jnp.minimum(h[:, 0:dff], SWIGLU_LIMIT)
        upv = jnp.clip(h[:, dff:2 * dff], -SWIGLU_LIMIT, SWIGLU_LIMIT)
        act = (upv + 1.0) * (gate * jax.nn.sigmoid(SWIGLU_ALPHA * gate))
        _store_rows_tiled(o_ref, jnp.dot(act.astype(BF16), w2_bf[...], preferred_element_type=F32) + b2_ref[...])

    @pl.when(i >= n_used)
    def _():
        o_ref[...] = jnp.zeros_like(o_ref)


def _expert_ffn(xs, blk_e, n_used, w1, b1, w2, b2, layer):
    dl = xs.shape[1]
    n_blk = xs.shape[0] // (MOE_BLOCK * SUBLANES)
    d = dl * SUBLANES
    dff = w2.shape[2]
    depth = w1.shape[0]
    grid_spec = pltpu.PrefetchScalarGridSpec(
        num_scalar_prefetch=2,
        grid=(n_blk,),
        in_specs=[
            pl.BlockSpec((MOE_BLOCK * SUBLANES, dl), lambda i, be, nu: (jnp.minimum(i, nu[0] - 1), 0)),
            pl.BlockSpec((None, None, d, 2 * dff), lambda i, be, nu: (layer, be[i], 0, 0)),
            pl.BlockSpec((None, None, 1, 2 * dff), lambda i, be, nu: (layer, be[i], 0, 0)),
            pl.BlockSpec((None, None, dff, d), lambda i, be, nu: (layer, be[i], 0, 0)),
            pl.BlockSpec((None, None, 1, d), lambda i, be, nu: (layer, be[i], 0, 0)),
        ],
        out_specs=pl.BlockSpec((MOE_BLOCK * SUBLANES, dl), lambda i, be, nu: (i, 0)),
        scratch_shapes=[pltpu.VMEM((d, 2 * dff), BF16), pltpu.VMEM((dff, d), BF16)],
    )
    return pl.pallas_call(
        _expert_kernel,
        out_shape=jax.ShapeDtypeStruct(xs.shape, F32),
        grid_spec=grid_spec,
        compiler_params=pltpu.CompilerParams(dimension_semantics=("arbitrary",), vmem_limit_bytes=EXPERT_VMEM_LIMIT),
        name="moe_experts",
    )(blk_e, n_used, xs, w1, b1.reshape(depth, N_EXPERTS, 1, 2 * dff), w2, b2.reshape(depth, N_EXPERTS, 1, d))


def _combine_kernel(row_cur_ref, row_next_ref, x_ref, gate_ref, yb_hbm, g_ref, b_ref, o_ref, buf, sems,
                    *, tm, alpha):
    i = pl.program_id(0)
    slot = lax.rem(i, 2)

    @pl.when(i == 0)
    def _():
        _start_row_tiles(yb_hbm, row_cur_ref, tm, buf.at[0], sems.at[0], per_row=TOP_K)

    @pl.when(i + 1 < pl.num_programs(0))
    def _():
        _start_row_tiles(yb_hbm, row_next_ref, tm, buf.at[1 - slot], sems.at[1 - slot], per_row=TOP_K)

    _wait_buffer(buf.at[slot], sems.at[slot])
    gates = gate_ref[...]
    y = gates[:, 0:1] * _load_rows_tiled(buf.at[slot, 0], tm)
    for k in range(1, TOP_K):
        y = y + gates[:, k:k + 1] * _load_rows_tiled(buf.at[slot, k], tm)
    o_ref[...] = _layer_norm(alpha * _load_rows_tiled(x_ref, tm) + y, g_ref[...], b_ref[...])


def _combine(x, gates, row_of, yb, g, b, alpha, tm=256):
    dl = x.shape[1]
    n, d = x.shape[0] // SUBLANES, dl * SUBLANES
    n_tiles = n // tm
    rows = row_of.reshape(n_tiles, 1, tm * TOP_K)
    return pl.pallas_call(
        functools.partial(_combine_kernel, tm=tm, alpha=alpha),
        out_shape=jax.ShapeDtypeStruct((n, d), F32),
        grid=(n_tiles,),
        in_specs=[
            pl.BlockSpec((None, 1, tm * TOP_K), lambda i: (i, 0, 0), memory_space=pltpu.SMEM),
            pl.BlockSpec((None, 1, tm * TOP_K), lambda i: (jnp.minimum(i + 1, n_tiles - 1), 0, 0),
                         memory_space=pltpu.SMEM),
            pl.BlockSpec((tm * SUBLANES, dl), lambda i: (i, 0)),
            pl.BlockSpec((tm, LANES), lambda i: (i, 0)),
            pl.BlockSpec(memory_space=pl.ANY),
            pl.BlockSpec((1, d), lambda i: (0, 0)),
            pl.BlockSpec((1, d), lambda i: (0, 0)),
        ],
        out_specs=pl.BlockSpec((tm, d), lambda i: (i, 0)),
        scratch_shapes=[pltpu.VMEM((2, TOP_K, tm * SUBLANES, dl), F32), pltpu.SemaphoreType.DMA((2,))],
        compiler_params=pltpu.CompilerParams(dimension_semantics=("arbitrary",), vmem_limit_bytes=VMEM_LIMIT,
                                             disable_bounds_checks=True),
        name="moe_combine_ln",
    )(rows, rows, x, gates, yb, g.reshape(1, d), b.reshape(1, d))


def _mixer_layer(x, w_in, b_in, pool_w, pool_b, pool_scale, cmp_pos, cmp_w1, cmp_b1, cmp_w2, cmp_b2,
                 w_up, w_o, ln_g, ln_b, alpha):
    bsz, t, d = x.shape
    n = bsz * t
    mix_w = d // 2
    g, hpg = NSA_KV_GROUPS, NSA_HPG
    nq_w = NSA_HEADS * HEAD_DIM
    nkv_w = N_BRANCH * 2 * g * HEAD_DIM
    nfox_w = 3 * FOX_HEADS * HEAD_DIM
    sizes = (mix_w, nq_w, nkv_w, N_BRANCH * NSA_HEADS, nfox_w, FOX_HEADS, N_BRANCH * d)
    o_pool, o_q, o_kv, o_g, o_fox, o_f, o_gm = (int(v) for v in np.cumsum((0,) + sizes[:-1]))
    n_gate = N_BRANCH * NSA_HEADS

    def cols(a, lo, width):
        return lax.slice_in_dim(a, lo, lo + width, axis=-1)

    qs = SCALE * LOG2E
    hw = FOX_HEADS * HEAD_DIM
    gw = g * HEAD_DIM
    pad_small = 2 * LANES - n_gate - FOX_HEADS
    kv0 = o_kv
    w_small = jnp.concatenate([cols(w_in, o_g, n_gate), cols(w_in, o_f, FOX_HEADS),
                               jnp.zeros((d, pad_small), F32)], axis=1).astype(BF16)
    b_small = jnp.concatenate([cols(b_in, o_g, n_gate), cols(b_in, o_f, FOX_HEADS), jnp.zeros((pad_small,), F32)])
    tok_cols = ((kv0, 2 * gw), (kv0 + 2 * gw, gw), (kv0 + 4 * gw, gw), (o_fox + hw, hw))
    w_k = jnp.concatenate([cols(w_in, lo, wd) for lo, wd in tok_cols], axis=1).astype(BF16)
    b_k = jnp.concatenate([cols(b_in, lo, wd) for lo, wd in tok_cols])
    feat_cols = ((o_q, nq_w, qs), (kv0 + 3 * gw, gw, 1.0), (kv0 + 5 * gw, gw, 1.0), (o_fox, hw, qs),
                 (o_fox + 2 * hw, hw, 1.0))
    w_t = jnp.concatenate([cols(w_in, lo, wd) * sc for lo, wd, sc in feat_cols], axis=1).T.astype(BF16)
    b_t = jnp.concatenate([cols(b_in, lo, wd) * sc for lo, wd, sc in feat_cols])
    r_vslc, r_vwin, r_qfox, r_vfox = nq_w, nq_w + gw, nq_w + 2 * gw, nq_w + 2 * gw + hw
    c_kslc, c_kwin, c_kfox = 2 * gw, 3 * gw, 4 * gw

    xf = x.reshape(n, d)
    u_pool = _matmul_bias(xf, cols(w_in, o_pool, mix_w).astype(BF16), cols(b_in, o_pool, mix_w), F32)
    z_small = _matmul_bias(xf, w_small, b_small, F32)
    zk = _matmul_bias(xf, w_k, b_k, BF16, tn=256).reshape(bsz, t, -1)
    gm = _matmul_bias(xf, cols(w_in, o_gm, N_BRANCH * d).astype(BF16), cols(b_in, o_gm, N_BRANCH * d), BF16,
                      act="sigmoid")
    zt = _matmul_bias_t(x, w_t, b_t)

    y_pool = _pool_mixer(u_pool.reshape(bsz, t, mix_w), pool_w, pool_b, pool_scale).reshape(n, mix_w)

    n_chunk = t // CMP_STRIDE
    a_cmp = cols(zk, 0, 2 * gw).reshape(bsz, n_chunk, CMP_STRIDE, 2, g, HEAD_DIM)
    a_cmp = a_cmp.transpose(3, 0, 4, 1, 2, 5).reshape(2, bsz * g, n_chunk, CMP_STRIDE * HEAD_DIM)
    kvc = _compress(a_cmp, cmp_pos.reshape(2, CMP_LEN * HEAD_DIM), cmp_w1, cmp_b1, cmp_w2, cmp_b2)
    kvc = kvc.reshape(2, bsz, g, n_chunk, HEAD_DIM)
    n_s = t // SEL_LEN
    c_start = np.arange(n_chunk) * CMP_STRIDE
    s_start = np.arange(n_s) * SEL_LEN
    ov = (c_start[:, None] < s_start[None, :] + SEL_LEN) & (c_start[:, None] + CMP_LEN > s_start[None, :])
    ov[n_chunk - (CMP_LEN // CMP_STRIDE) + 1:] = False
    ovt = jnp.asarray(ov.T.astype(np.float32), BF16)
    o_cmp, qaug = _nsa_cmp(zt, kvc[0], kvc[1].transpose(0, 1, 3, 2), ovt)
    onehot = jnp.asarray((np.arange(t)[:, None] // SEL_LEN == np.arange(n_s)[None, :]).astype(np.float32), BF16)
    o_slc = _nsa_slc(qaug, onehot, zk, zt, c_kslc // LANES, r_vslc // LANES)
    o_win = _nsa_win(zt, zk, c_kwin // LANES, r_vwin // LANES)

    cum = _fox_prep(z_small.reshape(bsz, t, 2 * LANES))
    cq = cols(cum, n_gate, FOX_HEADS)
    cq_t = cq.transpose(0, 3, 1, 2)
    ones_t = jnp.ones((bsz, FOX_HEADS, 3, t), BF16)
    zero6 = jnp.zeros((bsz, FOX_HEADS, 6, t), BF16)
    even = (np.arange(FOX_HEADS) % 2 == 0)[None, :, None, None]
    augq = jnp.concatenate([jnp.where(even, jnp.concatenate([cq_t, ones_t], axis=2), zero6),
                            jnp.where(even, zero6, jnp.concatenate([cq_t, ones_t], axis=2)),
                            jnp.zeros((bsz, FOX_HEADS, 4, t), BF16)], axis=2)
    ck = cq.transpose(0, 3, 2, 1).reshape(bsz, FOX_HEADS // 2, 2, t, 3)
    ones_k = jnp.ones((bsz, FOX_HEADS // 2, t, 3), BF16)
    augk = jnp.concatenate([ones_k, -ck[:, :, 0], ones_k, -ck[:, :, 1],
                            jnp.zeros((bsz, FOX_HEADS // 2, t, LANES - 12), BF16)], axis=-1)
    y_fox = _fox_attention(zt, zk, augq, augk, r_qfox, c_kfox, r_vfox)

    hsel = np.arange(NSA_HEADS * HEAD_DIM) // HEAD_DIM
    expand = np.zeros((LANES, N_BRANCH * mix_w), np.float32)
    for br in range(N_BRANCH):
        expand[hsel * N_BRANCH + br, br * mix_w + np.arange(mix_w)] = 1.0
    return _merge(xf, y_pool, o_cmp, o_slc, o_win, y_fox, z_small, gm,
                  jnp.asarray(expand, BF16), w_up.astype(BF16), w_o.astype(BF16), ln_g, ln_b, alpha)


def _moe_layer(x, router_w, router_b, w1, b1, w2, b2, layer, ln_g, ln_b, alpha):
    n = x.shape[0] // SUBLANES
    nk = n * TOP_K
    idx_p, gate_p = _router(x, router_w, router_b)
    e_flat = idx_p[:, :TOP_K].reshape(nk)
    onehot = (e_flat[:, None] == jnp.arange(N_EXPERTS)[None, :]).astype(jnp.int32)
    incl = jnp.cumsum(onehot, axis=0)
    counts = incl[-1]
    rank = jnp.sum((incl - onehot) * onehot, axis=1)
    padded = (counts + MOE_BLOCK - 1) // MOE_BLOCK * MOE_BLOCK
    pend = jnp.cumsum(padded)
    pstart = pend - padded
    dest = pstart[e_flat] + rank
    n_blk = (nk + MOE_BLOCK - 1) // MOE_BLOCK + N_EXPERTS
    blk_e = jnp.minimum(jnp.sum(jnp.arange(n_blk)[:, None] * MOE_BLOCK >= pend[None, :], axis=1),
                        N_EXPERTS - 1).astype(jnp.int32)
    n_used = (pend[-1] // MOE_BLOCK).astype(jnp.int32).reshape(1)
    dest = dest.astype(jnp.int32)

    xs = _dispatch(x, dest, pend.astype(jnp.int32), n_used, n_blk)
    yb = _expert_ffn(xs, blk_e, n_used, w1, b1, w2, b2, layer)
    return _combine(x, gate_p, dest, yb, ln_g, ln_b, alpha)


def kernel(x, w_in, b_in, pool_w, pool_b, pool_scale, cmp_pos, cmp_w1, cmp_b1, cmp_w2, cmp_b2,
           w_up, w_o, ln1_g, ln1_b, router_w, router_b, moe_w1, moe_b1, moe_w2, moe_b2, ln2_g, ln2_b):
    depth = w_in.shape[0]
    alpha = (2 * depth) ** 0.25
    bsz, t, d = x.shape
    for l in range(depth):
        x1 = _mixer_layer(x, w_in[l], b_in[l], pool_w[l], pool_b[l], pool_scale[l], cmp_pos[l], cmp_w1[l],
                          cmp_b1[l], cmp_w2[l], cmp_b2[l], w_up[l], w_o[l], ln1_g[l], ln1_b[l], alpha)
        x2 = _moe_layer(x1, router_w[l], router_b[l], moe_w1, moe_b1, moe_w2, moe_b2, l,
                        ln2_g[l], ln2_b[l], alpha)
        x = x2.reshape(bsz, t, d)
    return x
```

```python
import functools

import numpy as np
import jax
import jax.numpy as jnp
from jax import lax
from jax.experimental import pallas as pl
from jax.experimental.pallas import tpu as pltpu

F32 = jnp.float32
BF16 = jnp.bfloat16

HEAD_DIM = 64
POOL_WINDOWS = (2, 4, 8, 16)
POOL_GC = 128
POOL_HALO = 16
NSA_HEADS = 8
NSA_KV_GROUPS = 2
NSA_HPG = NSA_HEADS // NSA_KV_GROUPS
N_BRANCH = 3
CMP_LEN = 32
CMP_STRIDE = 16
SEL_LEN = 64
N_SEL = 16
WINDOW = 512
FOX_HEADS = 8
Q_BLOCK = 128
N_EXPERTS = 32
TOP_K = 4
SWIGLU_LIMIT = 7.0
SWIGLU_ALPHA = 1.702
MOE_BLOCK = 256
LN_EPS = 1e-5
NEG_INF = -1e30
SCALE = HEAD_DIM ** -0.5
MASK_BIG = 2.0 ** 100
LOG2E = 1.4426950408889634
ONES_ROWS = 16
LANES = 128
SUBLANES = 8
VMEM_LIMIT = 48 * 1024 * 1024
EXPERT_VMEM_LIMIT = 58 * 1024 * 1024
DMA_ISSUE_GROUP = 8


def _cparams(sem):
    return pltpu.CompilerParams(dimension_semantics=sem, vmem_limit_bytes=VMEM_LIMIT)


def _split3(x):
    hi = x.astype(BF16)
    r1 = x - hi.astype(F32)
    mid = r1.astype(BF16)
    lo = (r1 - mid.astype(F32)).astype(BF16)
    return hi, mid, lo


def _layer_norm(r, g, b):
    mu = jnp.mean(r, axis=-1, keepdims=True)
    var = jnp.mean(jnp.square(r - mu), axis=-1, keepdims=True)
    return (r - mu) * lax.rsqrt(var + LN_EPS) * g + b


def _load_rows_tiled(ref, rows):
    return jnp.concatenate([ref[pl.ds(s, rows, stride=SUBLANES), :] for s in range(SUBLANES)], axis=1)


def _store_rows_tiled(ref, val):
    rows = val.shape[0]
    for s in range(SUBLANES):
        ref[pl.ds(s, rows, stride=SUBLANES), :] = val[:, s * LANES:(s + 1) * LANES]


def _row_tile_copy(src, src_row, dst, dst_row, sem):
    return pltpu.make_async_copy(src.at[pl.ds(pl.multiple_of(src_row * SUBLANES, SUBLANES), SUBLANES), :],
                                 dst.at[pl.ds(pl.multiple_of(dst_row * SUBLANES, SUBLANES), SUBLANES), :], sem)


def _mm_kernel(x_ref, w_ref, b_ref, o_ref, *, act):
    acc = jnp.dot(x_ref[...].astype(BF16), w_ref[...], preferred_element_type=F32) + b_ref[...]
    if act == "sigmoid":
        acc = jax.nn.sigmoid(acc)
    o_ref[...] = acc.astype(o_ref.dtype)


def _matmul_bias(x, w, b, out_dtype, act=None, tm=1024, tn=512):
    m, k = x.shape
    n = w.shape[1]
    tn = min(tn, n)
    assert m % tm == 0 and n % tn == 0
    return pl.pallas_call(
        functools.partial(_mm_kernel, act=act),
        out_shape=jax.ShapeDtypeStruct((m, n), out_dtype),
        grid=(m // tm, n // tn),
        in_specs=[
            pl.BlockSpec((tm, k), lambda i, j: (i, 0)),
            pl.BlockSpec((k, tn), lambda i, j: (0, j)),
            pl.BlockSpec((1, tn), lambda i, j: (0, j)),
        ],
        out_specs=pl.BlockSpec((tm, tn), lambda i, j: (i, j)),
        compiler_params=_cparams(("parallel", "arbitrary")),
        name="in_proj",
    )(x, w, b.reshape(1, n))


def _mm_t_kernel(x_ref, wt_ref, b_ref, o_ref):
    acc = lax.dot_general(wt_ref[...], x_ref[...].astype(BF16), (((1,), (1,)), ((), ())),
                          preferred_element_type=F32) + b_ref[...]
    o_ref[...] = acc.astype(o_ref.dtype)


def _matmul_bias_t(x, wt, b, tm=1024, tf=896):
    bsz, t, k = x.shape
    f = wt.shape[0]
    assert t % tm == 0 and f % tf == 0
    return pl.pallas_call(
        _mm_t_kernel,
        out_shape=jax.ShapeDtypeStruct((bsz, f, t), BF16),
        grid=(bsz, t // tm, f // tf),
        in_specs=[
            pl.BlockSpec((None, tm, k), lambda bb, i, j: (bb, i, 0)),
            pl.BlockSpec((tf, k), lambda bb, i, j: (j, 0)),
            pl.BlockSpec((tf, 1), lambda bb, i, j: (j, 0)),
        ],
        out_specs=pl.BlockSpec((None, tf, tm), lambda bb, i, j: (bb, j, i)),
        compiler_params=_cparams(("parallel", "parallel", "arbitrary")),
        name="in_proj_t",
    )(x, wt, b.reshape(f, 1))


def _pool_kernel(prev_ref, cur_ref, w_ref, b_ref, sc_ref, o_ref, ext_ref, *, tile):
    i = pl.program_id(1)
    ext_ref[0:POOL_HALO, :] = jnp.where(i > 0, prev_ref[...], 0.0)
    ext_ref[POOL_HALO:POOL_HALO + tile, :] = cur_ref[...]
    t_idx = i * tile + lax.broadcasted_iota(jnp.int32, (tile, 1), 0)
    for gi, win in enumerate(POOL_WINDOWS):
        cols = slice(gi * POOL_GC, (gi + 1) * POOL_GC)
        u = ext_ref[POOL_HALO:POOL_HALO + tile, cols]
        wsum = u
        for j in range(1, win):
            wsum = wsum + ext_ref[POOL_HALO - j:POOL_HALO - j + tile, cols]
        cnt = jnp.minimum(t_idx + 1, win).astype(F32)
        d = wsum / cnt - u
        y = jnp.dot(d.astype(BF16), w_ref[gi], preferred_element_type=F32) + b_ref[gi]
        o_ref[:, cols] = (y * sc_ref[:, cols]).astype(o_ref.dtype)


def _pool_mixer(u, w, b, scale, tile=512):
    bsz, t, c = u.shape
    return pl.pallas_call(
        functools.partial(_pool_kernel, tile=tile),
        out_shape=jax.ShapeDtypeStruct((bsz, t, c), BF16),
        grid=(bsz, t // tile),
        in_specs=[
            pl.BlockSpec((None, POOL_HALO, c),
                         lambda bb, i: (bb, jnp.maximum(i * (tile // POOL_HALO) - 1, 0), 0)),
            pl.BlockSpec((None, tile, c), lambda bb, i: (bb, i, 0)),
            pl.BlockSpec((len(POOL_WINDOWS), POOL_GC, POOL_GC), lambda bb, i: (0, 0, 0)),
            pl.BlockSpec((len(POOL_WINDOWS), 1, POOL_GC), lambda bb, i: (0, 0, 0)),
            pl.BlockSpec((1, c), lambda bb, i: (0, 0)),
        ],
        out_specs=pl.BlockSpec((None, tile, c), lambda bb, i: (bb, i, 0)),
        scratch_shapes=[pltpu.VMEM((POOL_HALO + tile, c), F32)],
        compiler_params=_cparams(("parallel", "arbitrary")),
        name="pool_mixer",
    )(u, u, w.astype(BF16), b.reshape(len(POOL_WINDOWS), 1, POOL_GC), scale.reshape(1, c))


def _compress_kernel(a_ref, pos_ref, w1_ref, b1_ref, w2_ref, b2_ref, o_ref):
    half = CMP_STRIDE * HEAD_DIM
    a = a_ref[...].astype(F32)
    n_chunk = a.shape[0]
    top = (a + pos_ref[:, 0:half]).astype(BF16)
    bot = (a + pos_ref[:, half:2 * half]).astype(BF16)
    p1 = jnp.dot(top, w1_ref[0:half, :], preferred_element_type=F32)
    p2 = jnp.dot(bot, w1_ref[half:2 * half, :], preferred_element_type=F32)
    h = p1 + pltpu.roll(p2, n_chunk - 1, axis=0) + b1_ref[...]
    h = jax.nn.gelu(h)
    o = jnp.dot(h.astype(BF16), w2_ref[...], preferred_element_type=F32) + b2_ref[...]
    o_ref[...] = o.astype(o_ref.dtype)


def _compress(a, pos, w1, b1, w2, b2):
    _, bg, n_chunk, width = a.shape
    hid = w1.shape[-1]
    return pl.pallas_call(
        _compress_kernel,
        out_shape=jax.ShapeDtypeStruct((2, bg, n_chunk, HEAD_DIM), BF16),
        grid=(2, bg),
        in_specs=[
            pl.BlockSpec((None, None, n_chunk, width), lambda s, j: (s, j, 0, 0)),
            pl.BlockSpec((None, 1, 2 * width), lambda s, j: (s, 0, 0)),
            pl.BlockSpec((None, 2 * width, hid), lambda s, j: (s, 0, 0)),
            pl.BlockSpec((None, 1, hid), lambda s, j: (s, 0, 0)),
            pl.BlockSpec((None, hid, HEAD_DIM), lambda s, j: (s, 0, 0)),
            pl.BlockSpec((None, 1, HEAD_DIM), lambda s, j: (s, 0, 0)),
        ],
        out_specs=pl.BlockSpec((None, None, n_chunk, HEAD_DIM), lambda s, j: (s, j, 0, 0)),
        compiler_params=_cparams(("arbitrary", "arbitrary")),
        name="nsa_compress",
    )(a, pos.reshape(2, 1, 2 * width), w1.astype(BF16), b1.reshape(2, 1, hid),
      w2.astype(BF16), b2.reshape(2, 1, HEAD_DIM))


def _flash_steps_t(qts, ks, vts, carries, mask):
    scores = [jnp.dot(k, qt, preferred_element_type=F32) for k, qt in zip(ks, qts)]
    if mask is not None:
        masks = mask if isinstance(mask, (list, tuple)) else [mask] * len(scores)
        scores = [jnp.where(mk, s, NEG_INF) for mk, s in zip(masks, scores)]
    probs, stats = [], []
    for s, (m, _) in zip(scores, carries):
        m_new = jnp.maximum(m, jnp.max(s, axis=0, keepdims=True))
        probs.append(jnp.exp2(s - m_new).astype(BF16))
        stats.append((m_new, jnp.exp2(m - m_new)))
    return tuple((m_new, alpha * acc + jnp.dot(vt, p, preferred_element_type=F32))
                 for vt, p, (m_new, alpha), (_, acc) in zip(vts, probs, stats, carries))


def _flash_init_t(cols):
    return (jnp.full((1, cols), NEG_INF, F32), jnp.zeros((HEAD_DIM + ONES_ROWS, cols), F32))


def _flash_finish_t(acc):
    return acc[0:HEAD_DIM] / acc[HEAD_DIM:HEAD_DIM + 1]


def _with_ones(vt):
    return jnp.concatenate([vt, jnp.ones((ONES_ROWS, vt.shape[1]), vt.dtype)], axis=0)


def _group_queries(qt_ref, g):
    return jnp.concatenate([qt_ref[(g * NSA_HPG + h) * HEAD_DIM:(g * NSA_HPG + h + 1) * HEAD_DIM, :]
                            for h in range(NSA_HPG)], axis=1)


def _slot_rows(x, slot, n_slots):
    z = jnp.zeros_like(x)
    return jnp.concatenate([x if j == slot else z for j in range(n_slots)], axis=0)


def _store_group(o_ref, g, o, col0=0):
    for h in range(NSA_HPG):
        r0 = (g * NSA_HPG + h) * HEAD_DIM
        o_ref[r0:r0 + HEAD_DIM, col0:col0 + Q_BLOCK] = o[:, h * Q_BLOCK:(h + 1) * Q_BLOCK].astype(o_ref.dtype)


def _nsa_cmp_kernel(qt_ref, kc_ref, vct_ref, ovt_ref, ocmp_ref, qaug_ref):
    groups = range(kc_ref.shape[0])
    cols = NSA_HPG * Q_BLOCK
    t0 = pl.program_id(1) * Q_BLOCK
    n_c = kc_ref.shape[1]
    n_s = ovt_ref.shape[0]
    qts = [_group_queries(qt_ref, g) for g in groups]
    tq = t0 + (lax.broadcasted_iota(jnp.int32, (n_c, cols), 1) & (Q_BLOCK - 1))
    c_end = lax.broadcasted_iota(jnp.int32, (n_c, cols), 0) * CMP_STRIDE + (CMP_LEN - 1)
    mask = c_end <= tq
    ps = []
    for g in groups:
        sm = jnp.where(mask, jnp.dot(kc_ref[g], qts[g], preferred_element_type=F32), NEG_INF)
        e = jnp.where(mask, jnp.exp2(sm - jnp.max(sm, axis=0, keepdims=True)), 0.0)
        l = jnp.sum(e, axis=0, keepdims=True)
        ps.append(e / jnp.where(l > 0.0, l, 1.0))
    for g in groups:
        _store_group(ocmp_ref, g, jnp.dot(vct_ref[g], ps[g].astype(BF16), preferred_element_type=F32))

    ovt = ovt_ref[...]
    blk = lax.broadcasted_iota(jnp.int32, (n_s, Q_BLOCK), 0)
    tqq = t0 + lax.broadcasted_iota(jnp.int32, (n_s, Q_BLOCK), 1)
    future = blk * SEL_LEN > tqq
    cur = lax.shift_right_logical(tqq, 6)
    forced = (blk == 0) | (blk == cur) | (blk == cur - 1)
    scores = []
    for g in groups:
        psum = ps[g][:, 0:Q_BLOCK]
        for h in range(1, NSA_HPG):
            psum = psum + ps[g][:, h * Q_BLOCK:(h + 1) * Q_BLOCK]
        imp = None
        for part in _split3(psum):
            term = jnp.dot(ovt, part, preferred_element_type=F32)
            imp = term if imp is None else imp + term
        scores.append(jnp.where(future, -1.0, jnp.where(forced, 1e6, imp)))
    sels = [jnp.zeros((n_s, Q_BLOCK), F32) for _ in groups]
    for _ in range(min(N_SEL, n_s)):
        for g in groups:
            mx = jnp.max(scores[g], axis=0, keepdims=True)
            first = jnp.min(jnp.where(scores[g] == mx, blk, n_s), axis=0, keepdims=True)
            pick = blk == first
            sels[g] = jnp.where(pick, 1.0, sels[g])
            scores[g] = jnp.where(pick, -2.0, scores[g])
    for g in groups:
        sel = jnp.where(future, 0.0, sels[g])
        bias = ((sel - 1.0) * MASK_BIG).astype(BF16)
        for h in range(NSA_HPG):
            qaug_ref[g, 0:n_s, h * Q_BLOCK:(h + 1) * Q_BLOCK] = bias
        qaug_ref[g, n_s:n_s + len(groups) * HEAD_DIM, :] = _slot_rows(qts[g], g, len(groups))


def _nsa_cmp(zt, kc, vct, ovt):
    bsz, _, t = zt.shape
    g, n_c = kc.shape[1], kc.shape[2]
    n_s = ovt.shape[0]
    nq = t // Q_BLOCK
    rows_q = NSA_HEADS * HEAD_DIM
    cols = NSA_HPG * Q_BLOCK
    return pl.pallas_call(
        _nsa_cmp_kernel,
        out_shape=(jax.ShapeDtypeStruct((bsz, rows_q, t), BF16),
                   jax.ShapeDtypeStruct((bsz, g, nq, n_s + g * HEAD_DIM, cols), BF16)),
        grid=(bsz, nq),
        in_specs=[
            pl.BlockSpec((None, rows_q, Q_BLOCK), lambda b, i: (b, 0, i)),
            pl.BlockSpec((None, g, n_c, HEAD_DIM), lambda b, i: (b, 0, 0, 0)),
            pl.BlockSpec((None, g, HEAD_DIM, n_c), lambda b, i: (b, 0, 0, 0)),
            pl.BlockSpec((n_s, n_c), lambda b, i: (0, 0)),
        ],
        out_specs=(
            pl.BlockSpec((None, rows_q, Q_BLOCK), lambda b, i: (b, 0, i)),
            pl.BlockSpec((None, g, None, n_s + g * HEAD_DIM, cols), lambda b, i: (b, 0, i, 0, 0)),
        ),
        compiler_params=_cparams(("parallel", "arbitrary")),
        name="nsa_cmp_select",
    )(zt, kc, vct, ovt)


def _nsa_slc_kernel(qaug_ref, onehot_ref, k_ref, vt_ref, o_ref, *, tk, qb):
    groups = qaug_ref.shape[0]
    cols = NSA_HPG * Q_BLOCK
    t0 = pl.program_id(1) * (qb * Q_BLOCK)
    jd = t0 // tk
    probs = [(g, b) for g in range(groups) for b in range(qb)]

    def chunk(j, carries, mask):
        start = pl.multiple_of(j * tk, tk)
        k = jnp.concatenate([onehot_ref[pl.ds(start, tk), :], k_ref[pl.ds(start, tk), :]], axis=1)
        vts = [_with_ones(vt_ref[g * HEAD_DIM:(g + 1) * HEAD_DIM, pl.ds(start, tk)]) for g in range(groups)]
        return _flash_steps_t([qaug_ref[g, b] for g, b in probs], [k for _ in probs],
                              [vts[g] for g, _ in probs], carries, mask)

    carries = lax.fori_loop(0, jd, lambda j, c: chunk(j, c, None), tuple(_flash_init_t(cols) for _ in probs))
    kpos = jd * tk + lax.broadcasted_iota(jnp.int32, (tk, cols), 0)
    tq = t0 + (lax.broadcasted_iota(jnp.int32, (tk, cols), 1) & (Q_BLOCK - 1))
    causal = [kpos <= tq + b * Q_BLOCK for b in range(qb)]
    carries = chunk(jd, carries, [causal[b] for _, b in probs])
    for (g, b), (_, acc) in zip(probs, carries):
        _store_group(o_ref, g, _flash_finish_t(acc), b * Q_BLOCK)


def _nsa_slc(qaug, onehot, zk, zt, k_blk, vt_blk, tk=512, qb=2):
    bsz, g, nq, wa, cols = qaug.shape
    t = zk.shape[1]
    tk = min(tk, t)
    assert tk % (qb * Q_BLOCK) == 0 and nq % qb == 0 and g * HEAD_DIM == LANES
    return pl.pallas_call(
        functools.partial(_nsa_slc_kernel, tk=tk, qb=qb),
        out_shape=jax.ShapeDtypeStruct((bsz, NSA_HEADS * HEAD_DIM, t), BF16),
        grid=(bsz, nq // qb),
        in_specs=[
            pl.BlockSpec((None, g, qb, wa, cols), lambda b, i: (b, 0, i, 0, 0)),
            pl.BlockSpec(onehot.shape, lambda b, i: (0, 0)),
            pl.BlockSpec((None, t, LANES), lambda b, i: (b, 0, k_blk)),
            pl.BlockSpec((None, LANES, t), lambda b, i: (b, vt_blk, 0)),
        ],
        out_specs=pl.BlockSpec((None, NSA_HEADS * HEAD_DIM, qb * Q_BLOCK), lambda b, i: (b, 0, i)),
        compiler_params=_cparams(("parallel", "arbitrary")),
        name="nsa_selected",
    )(qaug, onehot, zk, zt)


def _nsa_win_kernel(qt_ref, k_ref, vt_ref, o_ref):
    groups = NSA_KV_GROUPS
    cols = NSA_HPG * Q_BLOCK
    span = WINDOW + Q_BLOCK
    t0 = pl.program_id(1) * Q_BLOCK
    start = pl.multiple_of(jnp.maximum(t0 - WINDOW, 0), Q_BLOCK)
    dist = (t0 - start) + (lax.broadcasted_iota(jnp.int32, (span, cols), 1) & (Q_BLOCK - 1)) \
        - lax.broadcasted_iota(jnp.int32, (span, cols), 0)
    mask = (dist >= 0) & (dist < WINDOW)
    k = k_ref[pl.ds(start, span), :]
    scores = [jnp.where(mask, jnp.dot(k, _slot_rows(_group_queries(qt_ref, g), g, groups),
                                      preferred_element_type=F32), NEG_INF) for g in range(groups)]
    probs = [jnp.exp2(s - jnp.max(s, axis=0, keepdims=True)).astype(BF16) for s in scores]
    for g in range(groups):
        vt = _with_ones(vt_ref[g * HEAD_DIM:(g + 1) * HEAD_DIM, pl.ds(start, span)])
        _store_group(o_ref, g, _flash_finish_t(jnp.dot(vt, probs[g], preferred_element_type=F32)))


def _nsa_win(zt, zk, k_blk, vt_blk):
    bsz, _, t = zt.shape
    rows_q = NSA_HEADS * HEAD_DIM
    assert t >= WINDOW + Q_BLOCK and NSA_KV_GROUPS * HEAD_DIM == LANES
    return pl.pallas_call(
        _nsa_win_kernel,
        out_shape=jax.ShapeDtypeStruct((bsz, rows_q, t), BF16),
        grid=(bsz, t // Q_BLOCK),
        in_specs=[
            pl.BlockSpec((None, rows_q, Q_BLOCK), lambda b, i: (b, 0, i)),
            pl.BlockSpec((None, t, LANES), lambda b, i: (b, 0, k_blk)),
            pl.BlockSpec((None, LANES, t), lambda b, i: (b, vt_blk, 0)),
        ],
        out_specs=pl.BlockSpec((None, rows_q, Q_BLOCK), lambda b, i: (b, 0, i)),
        compiler_params=_cparams(("parallel", "arbitrary")),
        name="nsa_window",
    )(zt, zk, zt)


def _fox_prep_kernel(z_ref, tri_ref, o_ref, carry_ref):
    @pl.when(pl.program_id(1) == 0)
    def _():
        carry_ref[...] = jnp.zeros_like(carry_ref)

    lf = jax.nn.log_sigmoid(z_ref[...])
    hi, mid, lo = _split3(lf)
    tri = tri_ref[...]
    cum = (jnp.dot(tri, hi, preferred_element_type=F32) + jnp.dot(tri, mid, preferred_element_type=F32)
           + jnp.dot(tri, lo, preferred_element_type=F32)) + carry_ref[...]
    carry_ref[...] = cum[cum.shape[0] - 1:cum.shape[0], :]
    for part, val in enumerate(_split3(cum * LOG2E)):
        o_ref[part] = val


def _fox_prep(z_small, tile=256):
    bsz, t, _ = z_small.shape
    tri = jnp.asarray(np.tril(np.ones((tile, tile), np.float32)), BF16)
    return pl.pallas_call(
        _fox_prep_kernel,
        out_shape=jax.ShapeDtypeStruct((bsz, 3, t, LANES), BF16),
        grid=(bsz, t // tile),
        in_specs=[
            pl.BlockSpec((None, tile, LANES), lambda b, i: (b, i, 0)),
            pl.BlockSpec((tile, tile), lambda b, i: (0, 0)),
        ],
        out_specs=pl.BlockSpec((None, 3, tile, LANES), lambda b, i: (b, 0, i, 0)),
        scratch_shapes=[pltpu.VMEM((1, LANES), F32)],
        compiler_params=_cparams(("parallel", "arbitrary")),
        name="fox_decay_cumsum",
    )(z_small, tri)


def _fox_kernel(qt_ref, augq_ref, k_ref, augk_ref, vt_ref, o_ref, *, tq, heads):
    i = pl.program_id(2)
    pair_w = LANES // HEAD_DIM
    qts = []
    for h in range(heads):
        q = _slot_rows(qt_ref[h * HEAD_DIM:(h + 1) * HEAD_DIM, :], h % pair_w, pair_w)
        aug = augq_ref[h]
        qts.append(jnp.concatenate([q, aug, jnp.zeros((LANES - aug.shape[0], tq), aug.dtype)], axis=0))

    def chunk(j, carries, mask):
        start = pl.multiple_of(j * tq, tq)
        ks = [jnp.concatenate([k_ref[pl.ds(start, tq), p * LANES:(p + 1) * LANES], augk_ref[p, pl.ds(start, tq), :]],
                              axis=1) for p in range(heads // pair_w)]
        return _flash_steps_t(qts, [ks[h // pair_w] for h in range(heads)],
                              [_with_ones(vt_ref[h * HEAD_DIM:(h + 1) * HEAD_DIM, pl.ds(start, tq)])
                               for h in range(heads)], carries, mask)

    carries = lax.fori_loop(0, i, lambda j, c: chunk(j, c, None), tuple(_flash_init_t(tq) for _ in range(heads)))
    causal = lax.broadcasted_iota(jnp.int32, (tq, tq), 0) <= lax.broadcasted_iota(jnp.int32, (tq, tq), 1)
    carries = chunk(i, carries, causal)
    for h in range(heads):
        o_ref[h * HEAD_DIM:(h + 1) * HEAD_DIM, :] = _flash_finish_t(carries[h][1]).astype(o_ref.dtype)


def _fox_attention(zt, zk, augq, augk, q_row0, k_col0, v_row0, tq=512, heads=4):
    bsz, _, t = zt.shape
    tq = min(tq, t)
    rows = heads * HEAD_DIM
    pairs = heads * HEAD_DIM // LANES
    assert q_row0 % rows == 0 and v_row0 % rows == 0 and k_col0 % rows == 0
    return pl.pallas_call(
        functools.partial(_fox_kernel, tq=tq, heads=heads),
        out_shape=jax.ShapeDtypeStruct((bsz, FOX_HEADS * HEAD_DIM, t), BF16),
        grid=(bsz, FOX_HEADS // heads, t // tq),
        in_specs=[
            pl.BlockSpec((None, rows, tq), lambda b, hh, i: (b, q_row0 // rows + hh, i)),
            pl.BlockSpec((None, heads, augq.shape[2], tq), lambda b, hh, i: (b, hh, 0, i)),
            pl.BlockSpec((None, t, rows), lambda b, hh, i: (b, 0, k_col0 // rows + hh)),
            pl.BlockSpec((None, pairs, t, LANES), lambda b, hh, i: (b, hh, 0, 0)),
            pl.BlockSpec((None, rows, t), lambda b, hh, i: (b, v_row0 // rows + hh, 0)),
        ],
        out_specs=pl.BlockSpec((None, rows, tq), lambda b, hh, i: (b, hh, i)),
        compiler_params=_cparams(("parallel", "parallel", "arbitrary")),
        name="fox_attention",
    )(zt, augq, zk, augk, zt)


def _merge_kernel(x_ref, yp_ref, oc_ref, os_ref, ow_ref, yf_ref, zs_ref, gm_ref, ex_ref,
                  wup_ref, wo_ref, g_ref, b_ref, o_ref, *, alpha):
    mix_w = yp_ref.shape[-1]
    d = x_ref.shape[-1]
    sg = jax.nn.sigmoid(zs_ref[:, 0:LANES])
    hi, mid, lo = _split3(sg)
    ex = ex_ref[...]
    ge = (jnp.dot(hi, ex, preferred_element_type=F32) + jnp.dot(mid, ex, preferred_element_type=F32)
          + jnp.dot(lo, ex, preferred_element_type=F32))
    yn = (ge[:, 0:mix_w] * oc_ref[...].astype(F32).T + ge[:, mix_w:2 * mix_w] * os_ref[...].astype(F32).T
          + ge[:, 2 * mix_w:3 * mix_w] * ow_ref[...].astype(F32).T)
    ys = (yp_ref[...], yn.astype(BF16), yf_ref[...].astype(F32).T.astype(BF16))
    mix = None
    for n in range(N_BRANCH):
        up = jnp.dot(ys[n], wup_ref[n], preferred_element_type=F32)
        term = gm_ref[:, n * d:(n + 1) * d].astype(F32) * up
        mix = term if mix is None else mix + term
    h = jnp.dot(mix.astype(BF16), wo_ref[...], preferred_element_type=F32)
    _store_rows_tiled(o_ref, _layer_norm(alpha * x_ref[...] + h, g_ref[...], b_ref[...]))


def _merge(x, yp, oc, osl, ow, yf, zs, gm, expand, wup, wo, g, b, alpha, tm=512):
    n, d = x.shape
    mix_w = yp.shape[1]
    tiles_per_seq = oc.shape[2] // tm
    tok = lambda width: pl.BlockSpec((tm, width), lambda i: (i, 0))
    feat = pl.BlockSpec((None, mix_w, tm), lambda i: (i // tiles_per_seq, 0, i % tiles_per_seq))
    return pl.pallas_call(
        functools.partial(_merge_kernel, alpha=alpha),
        out_shape=jax.ShapeDtypeStruct((n * SUBLANES, d // SUBLANES), F32),
        grid=(n // tm,),
        in_specs=[
            tok(d), tok(mix_w), feat, feat, feat, feat,
            pl.BlockSpec((tm, 2 * LANES), lambda i: (i, 0)),
            tok(N_BRANCH * d),
            pl.BlockSpec((LANES, N_BRANCH * mix_w), lambda i: (0, 0)),
            pl.BlockSpec((N_BRANCH, mix_w, d), lambda i: (0, 0, 0)),
            pl.BlockSpec((d, d), lambda i: (0, 0)),
            pl.BlockSpec((1, d), lambda i: (0, 0)),
            pl.BlockSpec((1, d), lambda i: (0, 0)),
        ],
        out_specs=pl.BlockSpec((tm * SUBLANES, d // SUBLANES), lambda i: (i, 0)),
        compiler_params=_cparams(("parallel",)),
        name="merge_outproj_ln",
    )(x, yp, oc, osl, ow, yf, zs, gm, expand, wup, wo, g.reshape(1, d), b.reshape(1, d))


def _router_kernel(x_ref, w_ref, b_ref, idx_ref, gate_ref):
    x = _load_rows_tiled(x_ref, idx_ref.shape[0])
    logits = jnp.dot(x, w_ref[...], preferred_element_type=F32,
                     precision=lax.Precision.HIGHEST) + b_ref[...]
    tm, lanes = logits.shape
    lane = lax.broadcasted_iota(jnp.int32, (tm, lanes), 1)
    work = jnp.where(lane < N_EXPERTS, logits, -jnp.inf)
    idx_out = jnp.zeros((tm, lanes), jnp.int32)
    val_out = jnp.zeros((tm, lanes), F32)
    top = None
    denom = jnp.zeros((tm, 1), F32)
    for k in range(TOP_K):
        mx = jnp.max(work, axis=-1, keepdims=True)
        first = jnp.min(jnp.where(work == mx, lane, lanes), axis=-1, keepdims=True)
        if top is None:
            top = mx
        e = jnp.exp(mx - top)
        denom = denom + e
        idx_out = jnp.where(lane == k, first, idx_out)
        val_out = jnp.where(lane == k, e, val_out)
        work = jnp.where(lane == first, -jnp.inf, work)
    idx_ref[...] = idx_out
    gate_ref[...] = val_out / denom


def _router(x, w, b, tm=512):
    n, d = x.shape[0] // SUBLANES, x.shape[1] * SUBLANES
    wp = jnp.zeros((d, LANES), F32).at[:, :N_EXPERTS].set(w)
    bp = jnp.zeros((1, LANES), F32).at[0, :N_EXPERTS].set(b)
    return pl.pallas_call(
        _router_kernel,
        out_shape=(jax.ShapeDtypeStruct((n, LANES), jnp.int32), jax.ShapeDtypeStruct((n, LANES), F32)),
        grid=(n // tm,),
        in_specs=[
            pl.BlockSpec((tm * SUBLANES, d // SUBLANES), lambda i: (i, 0)),
            pl.BlockSpec((d, LANES), lambda i: (0, 0)),
            pl.BlockSpec((1, LANES), lambda i: (0, 0)),
        ],
        out_specs=(pl.BlockSpec((tm, LANES), lambda i: (i, 0)), pl.BlockSpec((tm, LANES), lambda i: (i, 0))),
        compiler_params=_cparams(("parallel",)),
        name="moe_router",
    )(x, wp, bp)


def _start_row_tiles(src_hbm, idx_ref, n, dst, sem, *, per_row=1):
    def body(gi, c):
        base = gi * DMA_ISSUE_GROUP
        idx = [idx_ref[0, (base + j) * per_row + k] for j in range(DMA_ISSUE_GROUP) for k in range(per_row)]
        for j in range(DMA_ISSUE_GROUP):
            for k in range(per_row):
                _row_tile_copy(src_hbm, idx[j * per_row + k], dst.at[k] if per_row > 1 else dst, base + j, sem).start()
        return c
    lax.fori_loop(0, n // DMA_ISSUE_GROUP, body, 0)


def _wait_buffer(buf, sem):
    pltpu.make_async_copy(buf, buf, sem).wait()


def _dispatch_kernel(pend_ref, nu_ref, dest_ref, x_ref, xs_hbm, zeros, sems, *, tm, n_blk):
    i = pl.program_id(0)
    blk_rows = MOE_BLOCK * SUBLANES

    def zero_block(blk):
        return pltpu.make_async_copy(zeros, xs_hbm.at[pl.ds(pl.multiple_of(blk * blk_rows, blk_rows), blk_rows), :],
                                     sems.at[1])

    def last_block(e):
        prev = jnp.where(e > 0, pend_ref[jnp.maximum(e - 1, 0)], 0)
        return pend_ref[e] > prev, pend_ref[e] // MOE_BLOCK - 1

    @pl.when(i == 0)
    def _():
        zeros[...] = jnp.zeros_like(zeros)
        for wait in (False, True):
            def per_expert(e, c):
                used, blk = last_block(e)

                @pl.when(used)
                def _():
                    zero_block(blk).wait() if wait else zero_block(blk).start()
                return c

            def per_tail(blk, c):
                zero_block(blk).wait() if wait else zero_block(blk).start()
                return c
            lax.fori_loop(0, N_EXPERTS, per_expert, 0)
            lax.fori_loop(nu_ref[0], n_blk, per_tail, 0)

    def body(gi, c):
        base = gi * DMA_ISSUE_GROUP
        idx = [dest_ref[0, (base + j) * TOP_K + k] for j in range(DMA_ISSUE_GROUP) for k in range(TOP_K)]
        for j in range(DMA_ISSUE_GROUP):
            for k in range(TOP_K):
                _row_tile_copy(x_ref, base + j, xs_hbm, idx[j * TOP_K + k], sems.at[0]).start(priority=k % 2)
        return c
    lax.fori_loop(0, tm // DMA_ISSUE_GROUP, body, 0)
    for _ in range(TOP_K):
        _wait_buffer(x_ref, sems.at[0])


def _dispatch(x, dest, pend, n_used, n_blk, tm=256):
    dl = x.shape[1]
    n = x.shape[0] // SUBLANES
    n_tiles = n // tm
    grid_spec = pltpu.PrefetchScalarGridSpec(
        num_scalar_prefetch=2,
        grid=(n_tiles,),
        in_specs=[
            pl.BlockSpec((None, 1, tm * TOP_K), lambda i, pe, nu: (i, 0, 0), memory_space=pltpu.SMEM),
            pl.BlockSpec((tm * SUBLANES, dl), lambda i, pe, nu: (i, 0)),
        ],
        out_specs=pl.BlockSpec(memory_space=pl.ANY),
        scratch_shapes=[pltpu.VMEM((MOE_BLOCK * SUBLANES, dl), F32), pltpu.SemaphoreType.DMA((2,))],
    )
    return pl.pallas_call(
        functools.partial(_dispatch_kernel, tm=tm, n_blk=n_blk),
        out_shape=jax.ShapeDtypeStruct((n_blk * MOE_BLOCK * SUBLANES, dl), F32),
        grid_spec=grid_spec,
        compiler_params=pltpu.CompilerParams(dimension_semantics=("arbitrary",), vmem_limit_bytes=VMEM_LIMIT,
                                             disable_bounds_checks=True),
        name="moe_dispatch",
    )(pend, n_used, dest.reshape(n_tiles, 1, tm * TOP_K), x)


def _expert_kernel(be_ref, nu_ref, x_ref, w1_ref, b1_ref, w2_ref, b2_ref, o_ref, w1_bf, w2_bf):
    i = pl.program_id(0)
    n_used = nu_ref[0]

    @pl.when((i < n_used) & ((i == 0) | (be_ref[i] != be_ref[jnp.maximum(i - 1, 0)])))
    def _():
        w1_bf[...] = w1_ref[...].astype(BF16)
        w2_bf[...] = w2_ref[...].astype(BF16)

    @pl.when(i < n_used)
    def _():
        dff = w2_ref.shape[0]
        x = _load_rows_tiled(x_ref, MOE_BLOCK).astype(BF16)
        h = jnp.dot(x, w1_bf[...], preferred_element_type=F32) + b1_ref[...]
        gate = jnp.minimum(h[:, 0:dff], SWIGLU_LIMIT)
        upv = jnp.clip(h[:, dff:2 * dff], -SWIGLU_LIMIT, SWIGLU_LIMIT)
        act = (upv + 1.0) * (gate * jax.nn.sigmoid(SWIGLU_ALPHA * gate))
        _store_rows_tiled(o_ref, jnp.dot(act.astype(BF16), w2_bf[...], preferred_element_type=F32) + b2_ref[...])

    @pl.when(i >= n_used)
    def _():
        o_ref[...] = jnp.zeros_like(o_ref)


def _expert_ffn(xs, blk_e, n_used, w1, b1, w2, b2, layer):
    dl = xs.shape[1]
    n_blk = xs.shape[0] // (MOE_BLOCK * SUBLANES)
    d = dl * SUBLANES
    dff = w2.shape[2]
    depth = w1.shape[0]
    grid_spec = pltpu.PrefetchScalarGridSpec(
        num_scalar_prefetch=2,
        grid=(n_blk,),
        in_specs=[
            pl.BlockSpec((MOE_BLOCK * SUBLANES, dl), lambda i, be, nu: (jnp.minimum(i, nu[0] - 1), 0)),
            pl.BlockSpec((None, None, d, 2 * dff), lambda i, be, nu: (layer, be[i], 0, 0)),
            pl.BlockSpec((None, None, 1, 2 * dff), lambda i, be, nu: (layer, be[i], 0, 0)),
            pl.BlockSpec((None, None, dff, d), lambda i, be, nu: (layer, be[i], 0, 0)),
            pl.BlockSpec((None, None, 1, d), lambda i, be, nu: (layer, be[i], 0, 0)),
        ],
        out_specs=pl.BlockSpec((MOE_BLOCK * SUBLANES, dl), lambda i, be, nu: (i, 0)),
        scratch_shapes=[pltpu.VMEM((d, 2 * dff), BF16), pltpu.VMEM((dff, d), BF16)],
    )
    return pl.pallas_call(
        _expert_kernel,
        out_shape=jax.ShapeDtypeStruct(xs.shape, F32),
        grid_spec=grid_spec,
        compiler_params=pltpu.CompilerParams(dimension_semantics=("arbitrary",), vmem_limit_bytes=EXPERT_VMEM_LIMIT),
        name="moe_experts",
    )(blk_e, n_used, xs, w1, b1.reshape(depth, N_EXPERTS, 1, 2 * dff), w2, b2.reshape(depth, N_EXPERTS, 1, d))


def _combine_kernel(row_cur_ref, row_next_ref, x_ref, gate_ref, yb_hbm, g_ref, b_ref, o_ref, buf, sems,
                    *, tm, alpha):
    i = pl.program_id(0)
    slot = lax.rem(i, 2)

    @pl.when(i == 0)
    def _():
        _start_row_tiles(yb_hbm, row_cur_ref, tm, buf.at[0], sems.at[0], per_row=TOP_K)

    @pl.when(i + 1 < pl.num_programs(0))
    def _():
        _start_row_tiles(yb_hbm, row_next_ref, tm, buf.at[1 - slot], sems.at[1 - slot], per_row=TOP_K)

    _wait_buffer(buf.at[slot], sems.at[slot])
    gates = gate_ref[...]
    y = gates[:, 0:1] * _load_rows_tiled(buf.at[slot, 0], tm)
    for k in range(1, TOP_K):
        y = y + gates[:, k:k + 1] * _load_rows_tiled(buf.at[slot, k], tm)
    o_ref[...] = _layer_norm(alpha * _load_rows_tiled(x_ref, tm) + y, g_ref[...], b_ref[...])


def _combine(x, gates, row_of, yb, g, b, alpha, tm=256):
    dl = x.shape[1]
    n, d = x.shape[0] // SUBLANES, dl * SUBLANES
    n_tiles = n // tm
    rows = row_of.reshape(n_tiles, 1, tm * TOP_K)
    return pl.pallas_call(
        functools.partial(_combine_kernel, tm=tm, alpha=alpha),
        out_shape=jax.ShapeDtypeStruct((n, d), F32),
        grid=(n_tiles,),
        in_specs=[
            pl.BlockSpec((None, 1, tm * TOP_K), lambda i: (i, 0, 0), memory_space=pltpu.SMEM),
            pl.BlockSpec((None, 1, tm * TOP_K), lambda i: (jnp.minimum(i + 1, n_tiles - 1), 0, 0),
                         memory_space=pltpu.SMEM),
            pl.BlockSpec((tm * SUBLANES, dl), lambda i: (i, 0)),
            pl.BlockSpec((tm, LANES), lambda i: (i, 0)),
            pl.BlockSpec(memory_space=pl.ANY),
            pl.BlockSpec((1, d), lambda i: (0, 0)),
            pl.BlockSpec((1, d), lambda i: (0, 0)),
        ],
        out_specs=pl.BlockSpec((tm, d), lambda i: (i, 0)),
        scratch_shapes=[pltpu.VMEM((2, TOP_K, tm * SUBLANES, dl), F32), pltpu.SemaphoreType.DMA((2,))],
        compiler_params=pltpu.CompilerParams(dimension_semantics=("arbitrary",), vmem_limit_bytes=VMEM_LIMIT,
                                             disable_bounds_checks=True),
        name="moe_combine_ln",
    )(rows, rows, x, gates, yb, g.reshape(1, d), b.reshape(1, d))


def _mixer_layer(x, w_in, b_in, pool_w, pool_b, pool_scale, cmp_pos, cmp_w1, cmp_b1, cmp_w2, cmp_b2,
                 w_up, w_o, ln_g, ln_b, alpha):
    bsz, t, d = x.shape
    n = bsz * t
    mix_w = d // 2
    g, hpg = NSA_KV_GROUPS, NSA_HPG
    nq_w = NSA_HEADS * HEAD_DIM
    nkv_w = N_BRANCH * 2 * g * HEAD_DIM
    nfox_w = 3 * FOX_HEADS * HEAD_DIM
    sizes = (mix_w, nq_w, nkv_w, N_BRANCH * NSA_HEADS, nfox_w, FOX_HEADS, N_BRANCH * d)
    o_pool, o_q, o_kv, o_g, o_fox, o_f, o_gm = (int(v) for v in np.cumsum((0,) + sizes[:-1]))
    n_gate = N_BRANCH * NSA_HEADS

    def cols(a, lo, width):
        return lax.slice_in_dim(a, lo, lo + width, axis=-1)

    qs = SCALE * LOG2E
    hw = FOX_HEADS * HEAD_DIM
    gw = g * HEAD_DIM
    pad_small = 2 * LANES - n_gate - FOX_HEADS
    kv0 = o_kv
    w_small = jnp.concatenate([cols(w_in, o_g, n_gate), cols(w_in, o_f, FOX_HEADS),
                               jnp.zeros((d, pad_small), F32)], axis=1).astype(BF16)
    b_small = jnp.concatenate([cols(b_in, o_g, n_gate), cols(b_in, o_f, FOX_HEADS), jnp.zeros((pad_small,), F32)])
    tok_cols = ((kv0, 2 * gw), (kv0 + 2 * gw, gw), (kv0 + 4 * gw, gw), (o_fox + hw, hw))
    w_k = jnp.concatenate([cols(w_in, lo, wd) for lo, wd in tok_cols], axis=1).astype(BF16)
    b_k = jnp.concatenate([cols(b_in, lo, wd) for lo, wd in tok_cols])
    feat_cols = ((o_q, nq_w, qs), (kv0 + 3 * gw, gw, 1.0), (kv0 + 5 * gw, gw, 1.0), (o_fox, hw, qs),
                 (o_fox + 2 * hw, hw, 1.0))
    w_t = jnp.concatenate([cols(w_in, lo, wd) * sc for lo, wd, sc in feat_cols], axis=1).T.astype(BF16)
    b_t = jnp.concatenate([cols(b_in, lo, wd) * sc for lo, wd, sc in feat_cols])
    r_vslc, r_vwin, r_qfox, r_vfox = nq_w, nq_w + gw, nq_w + 2 * gw, nq_w + 2 * gw + hw
    c_kslc, c_kwin, c_kfox = 2 * gw, 3 * gw, 4 * gw

    xf = x.reshape(n, d)
    u_pool = _matmul_bias(xf, cols(w_in, o_pool, mix_w).astype(BF16), cols(b_in, o_pool, mix_w), F32)
    z_small = _matmul_bias(xf, w_small, b_small, F32)
    zk = _matmul_bias(xf, w_k, b_k, BF16, tn=256).reshape(bsz, t, -1)
    gm = _matmul_bias(xf, cols(w_in, o_gm, N_BRANCH * d).astype(BF16), cols(b_in, o_gm, N_BRANCH * d), BF16,
                      act="sigmoid")
    zt = _matmul_bias_t(x, w_t, b_t)

    y_pool = _pool_mixer(u_pool.reshape(bsz, t, mix_w), pool_w, pool_b, pool_scale).reshape(n, mix_w)

    n_chunk = t // CMP_STRIDE
    a_cmp = cols(zk, 0, 2 * gw).reshape(bsz, n_chunk, CMP_STRIDE, 2, g, HEAD_DIM)
    a_cmp = a_cmp.transpose(3, 0, 4, 1, 2, 5).reshape(2, bsz * g, n_chunk, CMP_STRIDE * HEAD_DIM)
    kvc = _compress(a_cmp, cmp_pos.reshape(2, CMP_LEN * HEAD_DIM), cmp_w1, cmp_b1, cmp_w2, cmp_b2)
    kvc = kvc.reshape(2, bsz, g, n_chunk, HEAD_DIM)
    n_s = t // SEL_LEN
    c_start = np.arange(n_chunk) * CMP_STRIDE
    s_start = np.arange(n_s) * SEL_LEN
    ov = (c_start[:, None] < s_start[None, :] + SEL_LEN) & (c_start[:, None] + CMP_LEN > s_start[None, :])
    ov[n_chunk - (CMP_LEN // CMP_STRIDE) + 1:] = False
    ovt = jnp.asarray(ov.T.astype(np.float32), BF16)
    o_cmp, qaug = _nsa_cmp(zt, kvc[0], kvc[1].transpose(0, 1, 3, 2), ovt)
    onehot = jnp.asarray((np.arange(t)[:, None] // SEL_LEN == np.arange(n_s)[None, :]).astype(np.float32), BF16)
    o_slc = _nsa_slc(qaug, onehot, zk, zt, c_kslc // LANES, r_vslc // LANES)
    o_win = _nsa_win(zt, zk, c_kwin // LANES, r_vwin // LANES)

    cum = _fox_prep(z_small.reshape(bsz, t, 2 * LANES))
    cq = cols(cum, n_gate, FOX_HEADS)
    cq_t = cq.transpose(0, 3, 1, 2)
    ones_t = jnp.ones((bsz, FOX_HEADS, 3, t), BF16)
    zero6 = jnp.zeros((bsz, FOX_HEADS, 6, t), BF16)
    even = (np.arange(FOX_HEADS) % 2 == 0)[None, :, None, None]
    augq = jnp.concatenate([jnp.where(even, jnp.concatenate([cq_t, ones_t], axis=2), zero6),
                            jnp.where(even, zero6, jnp.concatenate([cq_t, ones_t], axis=2)),
                            jnp.zeros((bsz, FOX_HEADS, 4, t), BF16)], axis=2)
    ck = cq.transpose(0, 3, 2, 1).reshape(bsz, FOX_HEADS // 2, 2, t, 3)
    ones_k = jnp.ones((bsz, FOX_HEADS // 2, t, 3), BF16)
    augk = jnp.concatenate([ones_k, -ck[:, :, 0], ones_k, -ck[:, :, 1],
                            jnp.zeros((bsz, FOX_HEADS // 2, t, LANES - 12), BF16)], axis=-1)
    y_fox = _fox_attention(zt, zk, augq, augk, r_qfox, c_kfox, r_vfox)

    hsel = np.arange(NSA_HEADS * HEAD_DIM) // HEAD_DIM
    expand = np.zeros((LANES, N_BRANCH * mix_w), np.float32)
    for br in range(N_BRANCH):
        expand[hsel * N_BRANCH + br, br * mix_w + np.arange(mix_w)] = 1.0
    return _merge(xf, y_pool, o_cmp, o_slc, o_win, y_fox, z_small, gm,
                  jnp.asarray(expand, BF16), w_up.astype(BF16), w_o.astype(BF16), ln_g, ln_b, alpha)


def _moe_layer(x, router_w, router_b, w1, b1, w2, b2, layer, ln_g, ln_b, alpha):
    n = x.shape[0] // SUBLANES
    nk = n * TOP_K
    idx_p, gate_p = _router(x, router_w, router_b)
    e_flat = idx_p[:, :TOP_K].reshape(nk)
    onehot = (e_flat[:, None] == jnp.arange(N_EXPERTS)[None, :]).astype(jnp.int32)
    incl = jnp.cumsum(onehot, axis=0)
    counts = incl[-1]
    rank = jnp.sum((incl - onehot) * onehot, axis=1)
    padded = (counts + MOE_BLOCK - 1) // MOE_BLOCK * MOE_BLOCK
    pend = jnp.cumsum(padded)
    pstart = pend - padded
    dest = pstart[e_flat] + rank
    n_blk = (nk + MOE_BLOCK - 1) // MOE_BLOCK + N_EXPERTS
    blk_e = jnp.minimum(jnp.sum(jnp.arange(n_blk)[:, None] * MOE_BLOCK >= pend[None, :], axis=1),
                        N_EXPERTS - 1).astype(jnp.int32)
    n_used = (pend[-1] // MOE_BLOCK).astype(jnp.int32).reshape(1)
    dest = dest.astype(jnp.int32)

    xs = _dispatch(x, dest, pend.astype(jnp.int32), n_used, n_blk)
    yb = _expert_ffn(xs, blk_e, n_used, w1, b1, w2, b2, layer)
    return _combine(x, gate_p, dest, yb, ln_g, ln_b, alpha)


def kernel(x, w_in, b_in, pool_w, pool_b, pool_scale, cmp_pos, cmp_w1, cmp_b1, cmp_w2, cmp_b2,
           w_up, w_o, ln1_g, ln1_b, router_w, router_b, moe_w1, moe_b1, moe_w2, moe_b2, ln2_g, ln2_b):
    depth = w_in.shape[0]
    alpha = (2 * depth) ** 0.25
    bsz, t, d = x.shape
    for l in range(depth):
        x1 = _mixer_layer(x, w_in[l], b_in[l], pool_w[l], pool_b[l], pool_scale[l], cmp_pos[l], cmp_w1[l],
                          cmp_b1[l], cmp_w2[l], cmp_b2[l], w_up[l], w_o[l], ln1_g[l], ln1_b[l], alpha)
        x2 = _moe_layer(x1, router_w[l], router_b[l], moe_w1, moe_b1, moe_w2, moe_b2, l,
                        ln2_g[l], ln2_b[l], alpha)
        x = x2.reshape(bsz, t, d)
    return x
```

```python
import functools

import numpy as np
import jax
import jax.numpy as jnp
from jax import lax
from jax.experimental import pallas as pl
from jax.experimental.pallas import tpu as pltpu

F32 = jnp.float32
BF16 = jnp.bfloat16

HEAD_DIM = 64
POOL_WINDOWS = (2, 4, 8, 16)
POOL_GC = 128
POOL_HALO = 16
NSA_HEADS = 8
NSA_KV_GROUPS = 2
NSA_HPG = NSA_HEADS // NSA_KV_GROUPS
N_BRANCH = 3
CMP_LEN = 32
CMP_STRIDE = 16
SEL_LEN = 64
N_SEL = 16
WINDOW = 512
FOX_HEADS = 8
Q_BLOCK = 128
N_EXPERTS = 32
TOP_K = 4
SWIGLU_LIMIT = 7.0
SWIGLU_ALPHA = 1.702
MOE_BLOCK = 256
LN_EPS = 1e-5
NEG_INF = -1e30
SCALE = HEAD_DIM ** -0.5
MASK_BIG = 2.0 ** 100
LOG2E = 1.4426950408889634
ONES_ROWS = 16
LANES = 128
SUBLANES = 8
VMEM_LIMIT = 48 * 1024 * 1024
EXPERT_VMEM_LIMIT = 58 * 1024 * 1024
DMA_ISSUE_GROUP = 8


def _cparams(sem):
    return pltpu.CompilerParams(dimension_semantics=sem, vmem_limit_bytes=VMEM_LIMIT)


def _split3(x):
    hi = x.astype(BF16)
    r1 = x - hi.astype(F32)
    mid = r1.astype(BF16)
    lo = (r1 - mid.astype(F32)).astype(BF16)
    return hi, mid, lo


def _layer_norm(r, g, b):
    mu = jnp.mean(r, axis=-1, keepdims=True)
    var = jnp.mean(jnp.square(r - mu), axis=-1, keepdims=True)
    return (r - mu) * lax.rsqrt(var + LN_EPS) * g + b


def _load_rows_tiled(ref, rows):
    return jnp.concatenate([ref[pl.ds(s, rows, stride=SUBLANES), :] for s in range(SUBLANES)], axis=1)


def _store_rows_tiled(ref, val):
    rows = val.shape[0]
    for s in range(SUBLANES):
        ref[pl.ds(s, rows, stride=SUBLANES), :] = val[:, s * LANES:(s + 1) * LANES]


def _row_tile_copy(src, src_row, dst, dst_row, sem):
    return pltpu.make_async_copy(src.at[pl.ds(pl.multiple_of(src_row * SUBLANES, SUBLANES), SUBLANES), :],
                                 dst.at[pl.ds(pl.multiple_of(dst_row * SUBLANES, SUBLANES), SUBLANES), :], sem)


def _mm_kernel(x_ref, w_ref, b_ref, o_ref, *, act):
    acc = jnp.dot(x_ref[...].astype(BF16), w_ref[...], preferred_element_type=F32) + b_ref[...]
    if act == "sigmoid":
        acc = jax.nn.sigmoid(acc)
    o_ref[...] = acc.astype(o_ref.dtype)


def _matmul_bias(x, w, b, out_dtype, act=None, tm=1024, tn=512):
    m, k = x.shape
    n = w.shape[1]
    tn = min(tn, n)
    assert m % tm == 0 and n % tn == 0
    return pl.pallas_call(
        functools.partial(_mm_kernel, act=act),
        out_shape=jax.ShapeDtypeStruct((m, n), out_dtype),
        grid=(m // tm, n // tn),
        in_specs=[
            pl.BlockSpec((tm, k), lambda i, j: (i, 0)),
            pl.BlockSpec((k, tn), lambda i, j: (0, j)),
            pl.BlockSpec((1, tn), lambda i, j: (0, j)),
        ],
        out_specs=pl.BlockSpec((tm, tn), lambda i, j: (i, j)),
        compiler_params=_cparams(("parallel", "arbitrary")),
        name="in_proj",
    )(x, w, b.reshape(1, n))


def _mm_t_kernel(x_ref, wt_ref, b_ref, o_ref):
    acc = lax.dot_general(wt_ref[...], x_ref[...].astype(BF16), (((1,), (1,)), ((), ())),
                          preferred_element_type=F32) + b_ref[...]
    o_ref[...] = acc.astype(o_ref.dtype)


def _matmul_bias_t(x, wt, b, tm=1024, tf=896):
    bsz, t, k = x.shape
    f = wt.shape[0]
    assert t % tm == 0 and f % tf == 0
    return pl.pallas_call(
        _mm_t_kernel,
        out_shape=jax.ShapeDtypeStruct((bsz, f, t), BF16),
        grid=(bsz, t // tm, f // tf),
        in_specs=[
            pl.BlockSpec((None, tm, k), lambda bb, i, j: (bb, i, 0)),
            pl.BlockSpec((tf, k), lambda bb, i, j: (j, 0)),
            pl.BlockSpec((tf, 1), lambda bb, i, j: (j, 0)),
        ],
        out_specs=pl.BlockSpec((None, tf, tm), lambda bb, i, j: (bb, j, i)),
        compiler_params=_cparams(("parallel", "parallel", "arbitrary")),
        name="in_proj_t",
    )(x, wt, b.reshape(f, 1))


def _pool_kernel(prev_ref, cur_ref, w_ref, b_ref, sc_ref, o_ref, ext_ref, *, tile):
    i = pl.program_id(1)
    ext_ref[0:POOL_HALO, :] = jnp.where(i > 0, prev_ref[...], 0.0)
    ext_ref[POOL_HALO:POOL_HALO + tile, :] = cur_ref[...]
    t_idx = i * tile + lax.broadcasted_iota(jnp.int32, (tile, 1), 0)
    for gi, win in enumerate(POOL_WINDOWS):
        cols = slice(gi * POOL_GC, (gi + 1) * POOL_GC)
        u = ext_ref[POOL_HALO:POOL_HALO + tile, cols]
        wsum = u
        for j in range(1, win):
            wsum = wsum + ext_ref[POOL_HALO - j:POOL_HALO - j + tile, cols]
        cnt = jnp.minimum(t_idx + 1, win).astype(F32)
        d = wsum / cnt - u
        y = jnp.dot(d.astype(BF16), w_ref[gi], preferred_element_type=F32) + b_ref[gi]
        o_ref[:, cols] = (y * sc_ref[:, cols]).astype(o_ref.dtype)


def _pool_mixer(u, t, w, b, scale, tile=512):
    n, c = u.shape
    bsz = n // t
    tiles, halos = t // tile, t // POOL_HALO
    return pl.pallas_call(
        functools.partial(_pool_kernel, tile=tile),
        out_shape=jax.ShapeDtypeStruct((n, c), BF16),
        grid=(bsz, tiles),
        in_specs=[
            pl.BlockSpec((POOL_HALO, c),
                         lambda bb, i: (bb * halos + jnp.maximum(i * (tile // POOL_HALO) - 1, 0), 0)),
            pl.BlockSpec((tile, c), lambda bb, i: (bb * tiles + i, 0)),
            pl.BlockSpec((len(POOL_WINDOWS), POOL_GC, POOL_GC), lambda bb, i: (0, 0, 0)),
            pl.BlockSpec((len(POOL_WINDOWS), 1, POOL_GC), lambda bb, i: (0, 0, 0)),
            pl.BlockSpec((1, c), lambda bb, i: (0, 0)),
        ],
        out_specs=pl.BlockSpec((tile, c), lambda bb, i: (bb * tiles + i, 0)),
        scratch_shapes=[pltpu.VMEM((POOL_HALO + tile, c), F32)],
        compiler_params=_cparams(("parallel", "arbitrary")),
        name="pool_mixer",
    )(u, u, w.astype(BF16), b.reshape(len(POOL_WINDOWS), 1, POOL_GC), scale.reshape(1, c))


def _compress_kernel(a_ref, pos_ref, w1_ref, b1_ref, w2_ref, b2_ref, o_ref):
    half = CMP_STRIDE * HEAD_DIM
    a = a_ref[...].astype(F32)
    n_chunk = a.shape[0]
    top = (a + pos_ref[:, 0:half]).astype(BF16)
    bot = (a + pos_ref[:, half:2 * half]).astype(BF16)
    p1 = jnp.dot(top, w1_ref[0:half, :], preferred_element_type=F32)
    p2 = jnp.dot(bot, w1_ref[half:2 * half, :], preferred_element_type=F32)
    h = p1 + pltpu.roll(p2, n_chunk - 1, axis=0) + b1_ref[...]
    h = jax.nn.gelu(h)
    o = jnp.dot(h.astype(BF16), w2_ref[...], preferred_element_type=F32) + b2_ref[...]
    o_ref[...] = o.astype(o_ref.dtype)


def _compress(a, pos, w1, b1, w2, b2):
    _, bg, n_chunk, width = a.shape
    hid = w1.shape[-1]
    return pl.pallas_call(
        _compress_kernel,
        out_shape=jax.ShapeDtypeStruct((2, bg, n_chunk, HEAD_DIM), BF16),
        grid=(2, bg),
        in_specs=[
            pl.BlockSpec((None, None, n_chunk, width), lambda s, j: (s, j, 0, 0)),
            pl.BlockSpec((None, 1, 2 * width), lambda s, j: (s, 0, 0)),
            pl.BlockSpec((None, 2 * width, hid), lambda s, j: (s, 0, 0)),
            pl.BlockSpec((None, 1, hid), lambda s, j: (s, 0, 0)),
            pl.BlockSpec((None, hid, HEAD_DIM), lambda s, j: (s, 0, 0)),
            pl.BlockSpec((None, 1, HEAD_DIM), lambda s, j: (s, 0, 0)),
        ],
        out_specs=pl.BlockSpec((None, None, n_chunk, HEAD_DIM), lambda s, j: (s, j, 0, 0)),
        compiler_params=_cparams(("arbitrary", "arbitrary")),
        name="nsa_compress",
    )(a, pos.reshape(2, 1, 2 * width), w1.astype(BF16), b1.reshape(2, 1, hid),
      w2.astype(BF16), b2.reshape(2, 1, HEAD_DIM))


def _flash_steps_t(qts, ks, vts, carries, mask):
    scores = [jnp.dot(k, qt, preferred_element_type=F32) for k, qt in zip(ks, qts)]
    if mask is not None:
        masks = mask if isinstance(mask, (list, tuple)) else [mask] * len(scores)
        scores = [jnp.where(mk, s, NEG_INF) for mk, s in zip(masks, scores)]
    probs, stats = [], []
    for s, (m, _) in zip(scores, carries):
        m_new = jnp.maximum(m, jnp.max(s, axis=0, keepdims=True))
        probs.append(jnp.exp2(s - m_new).astype(BF16))
        stats.append((m_new, jnp.exp2(m - m_new)))
    return tuple((m_new, alpha * acc + jnp.dot(vt, p, preferred_element_type=F32))
                 for vt, p, (m_new, alpha), (_, acc) in zip(vts, probs, stats, carries))


def _flash_init_t(cols):
    return (jnp.full((1, cols), NEG_INF, F32), jnp.zeros((HEAD_DIM + ONES_ROWS, cols), F32))


def _flash_finish_t(acc):
    return acc[0:HEAD_DIM] / acc[HEAD_DIM:HEAD_DIM + 1]


def _with_ones(vt):
    return jnp.concatenate([vt, jnp.ones((ONES_ROWS, vt.shape[1]), vt.dtype)], axis=0)


def _group_queries(qt_ref, g):
    return jnp.concatenate([qt_ref[(g * NSA_HPG + h) * HEAD_DIM:(g * NSA_HPG + h + 1) * HEAD_DIM, :]
                            for h in range(NSA_HPG)], axis=1)


def _slot_rows(x, slot, n_slots):
    z = jnp.zeros_like(x)
    return jnp.concatenate([x if j == slot else z for j in range(n_slots)], axis=0)


def _store_group(o_ref, g, o, col0=0):
    for h in range(NSA_HPG):
        r0 = (g * NSA_HPG + h) * HEAD_DIM
        o_ref[r0:r0 + HEAD_DIM, col0:col0 + Q_BLOCK] = o[:, h * Q_BLOCK:(h + 1) * Q_BLOCK].astype(o_ref.dtype)


def _nsa_cmp_kernel(qt_ref, kc_ref, vct_ref, ovt_ref, ocmp_ref, qaug_ref):
    groups = range(kc_ref.shape[0])
    cols = NSA_HPG * Q_BLOCK
    t0 = pl.program_id(1) * Q_BLOCK
    n_c = kc_ref.shape[1]
    n_s = ovt_ref.shape[0]
    qts = [_group_queries(qt_ref, g) for g in groups]
    tq = t0 + (lax.broadcasted_iota(jnp.int32, (n_c, cols), 1) & (Q_BLOCK - 1))
    c_end = lax.broadcasted_iota(jnp.int32, (n_c, cols), 0) * CMP_STRIDE + (CMP_LEN - 1)
    mask = c_end <= tq
    ps = []
    for g in groups:
        sm = jnp.where(mask, jnp.dot(kc_ref[g], qts[g], preferred_element_type=F32), NEG_INF)
        e = jnp.where(mask, jnp.exp2(sm - jnp.max(sm, axis=0, keepdims=True)), 0.0)
        l = jnp.sum(e, axis=0, keepdims=True)
        ps.append(e / jnp.where(l > 0.0, l, 1.0))
    for g in groups:
        _store_group(ocmp_ref, g, jnp.dot(vct_ref[g], ps[g].astype(BF16), preferred_element_type=F32))

    ovt = ovt_ref[...]
    blk = lax.broadcasted_iota(jnp.int32, (n_s, Q_BLOCK), 0)
    tqq = t0 + lax.broadcasted_iota(jnp.int32, (n_s, Q_BLOCK), 1)
    future = blk * SEL_LEN > tqq
    cur = lax.shift_right_logical(tqq, 6)
    forced = (blk == 0) | (blk == cur) | (blk == cur - 1)
    scores = []
    for g in groups:
        psum = ps[g][:, 0:Q_BLOCK]
        for h in range(1, NSA_HPG):
            psum = psum + ps[g][:, h * Q_BLOCK:(h + 1) * Q_BLOCK]
        imp = None
        for part in _split3(psum):
            term = jnp.dot(ovt, part, preferred_element_type=F32)
            imp = term if imp is None else imp + term
        scores.append(jnp.where(future, -1.0, jnp.where(forced, 1e6, imp)))
    sels = [jnp.zeros((n_s, Q_BLOCK), F32) for _ in groups]
    for _ in range(min(N_SEL, n_s)):
        for g in groups:
            mx = jnp.max(scores[g], axis=0, keepdims=True)
            first = jnp.min(jnp.where(scores[g] == mx, blk, n_s), axis=0, keepdims=True)
            pick = blk == first
            sels[g] = jnp.where(pick, 1.0, sels[g])
            scores[g] = jnp.where(pick, -2.0, scores[g])
    for g in groups:
        sel = jnp.where(future, 0.0, sels[g])
        bias = ((sel - 1.0) * MASK_BIG).astype(BF16)
        for h in range(NSA_HPG):
            qaug_ref[g, 0:n_s, h * Q_BLOCK:(h + 1) * Q_BLOCK] = bias
        qaug_ref[g, n_s:n_s + len(groups) * HEAD_DIM, :] = _slot_rows(qts[g], g, len(groups))


def _nsa_cmp(zt, kc, vct, ovt):
    bsz, _, t = zt.shape
    g, n_c = kc.shape[1], kc.shape[2]
    n_s = ovt.shape[0]
    nq = t // Q_BLOCK
    rows_q = NSA_HEADS * HEAD_DIM
    cols = NSA_HPG * Q_BLOCK
    return pl.pallas_call(
        _nsa_cmp_kernel,
        out_shape=(jax.ShapeDtypeStruct((bsz, rows_q, t), BF16),
                   jax.ShapeDtypeStruct((bsz, g, nq, n_s + g * HEAD_DIM, cols), BF16)),
        grid=(bsz, nq),
        in_specs=[
            pl.BlockSpec((None, rows_q, Q_BLOCK), lambda b, i: (b, 0, i)),
            pl.BlockSpec((None, g, n_c, HEAD_DIM), lambda b, i: (b, 0, 0, 0)),
            pl.BlockSpec((None, g, HEAD_DIM, n_c), lambda b, i: (b, 0, 0, 0)),
            pl.BlockSpec((n_s, n_c), lambda b, i: (0, 0)),
        ],
        out_specs=(
            pl.BlockSpec((None, rows_q, Q_BLOCK), lambda b, i: (b, 0, i)),
            pl.BlockSpec((None, g, None, n_s + g * HEAD_DIM, cols), lambda b, i: (b, 0, i, 0, 0)),
        ),
        compiler_params=_cparams(("parallel", "arbitrary")),
        name="nsa_cmp_select",
    )(zt, kc, vct, ovt)


def _nsa_slc_kernel(qaug_ref, onehot_ref, k_ref, vt_ref, o_ref, *, tk, qb):
    groups = qaug_ref.shape[0]
    cols = NSA_HPG * Q_BLOCK
    t0 = pl.program_id(1) * (qb * Q_BLOCK)
    jd = t0 // tk
    probs = [(g, b) for g in range(groups) for b in range(qb)]

    def chunk(j, carries, mask):
        start = pl.multiple_of(j * tk, tk)
        k = jnp.concatenate([onehot_ref[pl.ds(start, tk), :], k_ref[pl.ds(start, tk), :]], axis=1)
        vts = [_with_ones(vt_ref[g * HEAD_DIM:(g + 1) * HEAD_DIM, pl.ds(start, tk)]) for g in range(groups)]
        return _flash_steps_t([qaug_ref[g, b] for g, b in probs], [k for _ in probs],
                              [vts[g] for g, _ in probs], carries, mask)

    carries = lax.fori_loop(0, jd, lambda j, c: chunk(j, c, None), tuple(_flash_init_t(cols) for _ in probs))
    kpos = jd * tk + lax.broadcasted_iota(jnp.int32, (tk, cols), 0)
    tq = t0 + (lax.broadcasted_iota(jnp.int32, (tk, cols), 1) & (Q_BLOCK - 1))
    causal = [kpos <= tq + b * Q_BLOCK for b in range(qb)]
    carries = chunk(jd, carries, [causal[b] for _, b in probs])
    for (g, b), (_, acc) in zip(probs, carries):
        _store_group(o_ref, g, _flash_finish_t(acc), b * Q_BLOCK)


def _nsa_slc(qaug, onehot, zk, zt, k_blk, vt_blk, tk=512, qb=2):
    bsz, g, nq, wa, cols = qaug.shape
    t = onehot.shape[0]
    tk = min(tk, t)
    assert tk % (qb * Q_BLOCK) == 0 and nq % qb == 0 and g * HEAD_DIM == LANES
    return pl.pallas_call(
        functools.partial(_nsa_slc_kernel, tk=tk, qb=qb),
        out_shape=jax.ShapeDtypeStruct((bsz, NSA_HEADS * HEAD_DIM, t), BF16),
        grid=(bsz, nq // qb),
        in_specs=[
            pl.BlockSpec((None, g, qb, wa, cols), lambda b, i: (b, 0, i, 0, 0)),
            pl.BlockSpec(onehot.shape, lambda b, i: (0, 0)),
            pl.BlockSpec((t, LANES), lambda b, i: (b, k_blk)),
            pl.BlockSpec((None, LANES, t), lambda b, i: (b, vt_blk, 0)),
        ],
        out_specs=pl.BlockSpec((None, NSA_HEADS * HEAD_DIM, qb * Q_BLOCK), lambda b, i: (b, 0, i)),
        compiler_params=_cparams(("parallel", "arbitrary")),
        name="nsa_selected",
    )(qaug, onehot, zk, zt)


def _nsa_win_kernel(qt_ref, k_ref, vt_ref, o_ref):
    groups = NSA_KV_GROUPS
    cols = NSA_HPG * Q_BLOCK
    span = WINDOW + Q_BLOCK
    t0 = pl.program_id(1) * Q_BLOCK
    start = pl.multiple_of(jnp.maximum(t0 - WINDOW, 0), Q_BLOCK)
    dist = (t0 - start) + (lax.broadcasted_iota(jnp.int32, (span, cols), 1) & (Q_BLOCK - 1)) \
        - lax.broadcasted_iota(jnp.int32, (span, cols), 0)
    mask = (dist >= 0) & (dist < WINDOW)
    k = k_ref[pl.ds(start, span), :]
    scores = [jnp.where(mask, jnp.dot(k, _slot_rows(_group_queries(qt_ref, g), g, groups),
                                      preferred_element_type=F32), NEG_INF) for g in range(groups)]
    probs = [jnp.exp2(s - jnp.max(s, axis=0, keepdims=True)).astype(BF16) for s in scores]
    for g in range(groups):
        vt = _with_ones(vt_ref[g * HEAD_DIM:(g + 1) * HEAD_DIM, pl.ds(start, span)])
        _store_group(o_ref, g, _flash_finish_t(jnp.dot(vt, probs[g], preferred_element_type=F32)))


def _nsa_win(zt, zk, k_blk, vt_blk):
    bsz, _, t = zt.shape
    rows_q = NSA_HEADS * HEAD_DIM
    assert t >= WINDOW + Q_BLOCK and NSA_KV_GROUPS * HEAD_DIM == LANES
    return pl.pallas_call(
        _nsa_win_kernel,
        out_shape=jax.ShapeDtypeStruct((bsz, rows_q, t), BF16),
        grid=(bsz, t // Q_BLOCK),
        in_specs=[
            pl.BlockSpec((None, rows_q, Q_BLOCK), lambda b, i: (b, 0, i)),
            pl.BlockSpec((t, LANES), lambda b, i: (b, k_blk)),
            pl.BlockSpec((None, LANES, t), lambda b, i: (b, vt_blk, 0)),
        ],
        out_specs=pl.BlockSpec((None, rows_q, Q_BLOCK), lambda b, i: (b, 0, i)),
        compiler_params=_cparams(("parallel", "arbitrary")),
        name="nsa_window",
    )(zt, zk, zt)


def _fox_prep_kernel(z_ref, tri_ref, o_ref, carry_ref):
    @pl.when(pl.program_id(1) == 0)
    def _():
        carry_ref[...] = jnp.zeros_like(carry_ref)

    lf = jax.nn.log_sigmoid(z_ref[...])
    hi, mid, lo = _split3(lf)
    tri = tri_ref[...]
    cum = (jnp.dot(tri, hi, preferred_element_type=F32) + jnp.dot(tri, mid, preferred_element_type=F32)
           + jnp.dot(tri, lo, preferred_element_type=F32)) + carry_ref[...]
    carry_ref[...] = cum[cum.shape[0] - 1:cum.shape[0], :]
    for part, val in enumerate(_split3(cum * LOG2E)):
        o_ref[part] = val


def _fox_prep(z_small, t, tile=256):
    bsz = z_small.shape[0] // t
    tiles = t // tile
    tri = jnp.asarray(np.tril(np.ones((tile, tile), np.float32)), BF16)
    return pl.pallas_call(
        _fox_prep_kernel,
        out_shape=jax.ShapeDtypeStruct((bsz, 3, t, LANES), BF16),
        grid=(bsz, tiles),
        in_specs=[
            pl.BlockSpec((tile, LANES), lambda b, i: (b * tiles + i, 0)),
            pl.BlockSpec((tile, tile), lambda b, i: (0, 0)),
        ],
        out_specs=pl.BlockSpec((None, 3, tile, LANES), lambda b, i: (b, 0, i, 0)),
        scratch_shapes=[pltpu.VMEM((1, LANES), F32)],
        compiler_params=_cparams(("parallel", "arbitrary")),
        name="fox_decay_cumsum",
    )(z_small, tri)


def _fox_kernel(qt_ref, augq_ref, k_ref, augk_ref, vt_ref, o_ref, *, tq, heads):
    i = pl.program_id(2)
    pair_w = LANES // HEAD_DIM
    qts = []
    for h in range(heads):
        q = _slot_rows(qt_ref[h * HEAD_DIM:(h + 1) * HEAD_DIM, :], h % pair_w, pair_w)
        aug = augq_ref[h]
        qts.append(jnp.concatenate([q, aug, jnp.zeros((LANES - aug.shape[0], tq), aug.dtype)], axis=0))

    def chunk(j, carries, mask):
        start = pl.multiple_of(j * tq, tq)
        ks = [jnp.concatenate([k_ref[pl.ds(start, tq), p * LANES:(p + 1) * LANES], augk_ref[p, pl.ds(start, tq), :]],
                              axis=1) for p in range(heads // pair_w)]
        return _flash_steps_t(qts, [ks[h // pair_w] for h in range(heads)],
                              [_with_ones(vt_ref[h * HEAD_DIM:(h + 1) * HEAD_DIM, pl.ds(start, tq)])
                               for h in range(heads)], carries, mask)

    carries = lax.fori_loop(0, i, lambda j, c: chunk(j, c, None), tuple(_flash_init_t(tq) for _ in range(heads)))
    causal = lax.broadcasted_iota(jnp.int32, (tq, tq), 0) <= lax.broadcasted_iota(jnp.int32, (tq, tq), 1)
    carries = chunk(i, carries, causal)
    for h in range(heads):
        o_ref[h * HEAD_DIM:(h + 1) * HEAD_DIM, :] = _flash_finish_t(carries[h][1]).astype(o_ref.dtype)


def _fox_attention(zt, zk, augq, augk, q_row0, k_col0, v_row0, tq=512, heads=4):
    bsz, _, t = zt.shape
    tq = min(tq, t)
    rows = heads * HEAD_DIM
    pairs = heads * HEAD_DIM // LANES
    assert q_row0 % rows == 0 and v_row0 % rows == 0 and k_col0 % rows == 0
    return pl.pallas_call(
        functools.partial(_fox_kernel, tq=tq, heads=heads),
        out_shape=jax.ShapeDtypeStruct((bsz, FOX_HEADS * HEAD_DIM, t), BF16),
        grid=(bsz, FOX_HEADS // heads, t // tq),
        in_specs=[
            pl.BlockSpec((None, rows, tq), lambda b, hh, i: (b, q_row0 // rows + hh, i)),
            pl.BlockSpec((None, heads, augq.shape[2], tq), lambda b, hh, i: (b, hh, 0, i)),
            pl.BlockSpec((t, rows), lambda b, hh, i: (b, k_col0 // rows + hh)),
            pl.BlockSpec((None, pairs, t, LANES), lambda b, hh, i: (b, hh, 0, 0)),
            pl.BlockSpec((None, rows, t), lambda b, hh, i: (b, v_row0 // rows + hh, 0)),
        ],
        out_specs=pl.BlockSpec((None, rows, tq), lambda b, hh, i: (b, hh, i)),
        compiler_params=_cparams(("parallel", "parallel", "arbitrary")),
        name="fox_attention",
    )(zt, augq, zk, augk, zt)


def _merge_kernel(x_ref, yp_ref, oc_ref, os_ref, ow_ref, yf_ref, zs_ref, gm_ref, ex_ref,
                  wup_ref, wo_ref, g_ref, b_ref, o_ref, *, alpha):
    mix_w = yp_ref.shape[-1]
    d = x_ref.shape[-1]
    sg = jax.nn.sigmoid(zs_ref[:, 0:LANES])
    hi, mid, lo = _split3(sg)
    ex = ex_ref[...]
    ge = (jnp.dot(hi, ex, preferred_element_type=F32) + jnp.dot(mid, ex, preferred_element_type=F32)
          + jnp.dot(lo, ex, preferred_element_type=F32))
    yn = (ge[:, 0:mix_w] * oc_ref[...].astype(F32).T + ge[:, mix_w:2 * mix_w] * os_ref[...].astype(F32).T
          + ge[:, 2 * mix_w:3 * mix_w] * ow_ref[...].astype(F32).T)
    ys = (yp_ref[...], yn.astype(BF16), yf_ref[...].astype(F32).T.astype(BF16))
    mix = None
    for n in range(N_BRANCH):
        up = jnp.dot(ys[n], wup_ref[n], preferred_element_type=F32)
        term = gm_ref[:, n * d:(n + 1) * d].astype(F32) * up
        mix = term if mix is None else mix + term
    h = jnp.dot(mix.astype(BF16), wo_ref[...], preferred_element_type=F32)
    _store_rows_tiled(o_ref, _layer_norm(alpha * x_ref[...] + h, g_ref[...], b_ref[...]))


def _merge(x, yp, oc, osl, ow, yf, zs, gm, expand, wup, wo, g, b, alpha, tm=512):
    n, d = x.shape
    mix_w = yp.shape[1]
    tiles_per_seq = oc.shape[2] // tm
    tok = lambda width: pl.BlockSpec((tm, width), lambda i: (i, 0))
    feat = pl.BlockSpec((None, mix_w, tm), lambda i: (i // tiles_per_seq, 0, i % tiles_per_seq))
    return pl.pallas_call(
        functools.partial(_merge_kernel, alpha=alpha),
        out_shape=jax.ShapeDtypeStruct((n * SUBLANES, d // SUBLANES), F32),
        grid=(n // tm,),
        in_specs=[
            tok(d), tok(mix_w), feat, feat, feat, feat,
            pl.BlockSpec((tm, 2 * LANES), lambda i: (i, 0)),
            tok(N_BRANCH * d),
            pl.BlockSpec((LANES, N_BRANCH * mix_w), lambda i: (0, 0)),
            pl.BlockSpec((N_BRANCH, mix_w, d), lambda i: (0, 0, 0)),
            pl.BlockSpec((d, d), lambda i: (0, 0)),
            pl.BlockSpec((1, d), lambda i: (0, 0)),
            pl.BlockSpec((1, d), lambda i: (0, 0)),
        ],
        out_specs=pl.BlockSpec((tm * SUBLANES, d // SUBLANES), lambda i: (i, 0)),
        compiler_params=_cparams(("parallel",)),
        name="merge_outproj_ln",
    )(x, yp, oc, osl, ow, yf, zs, gm, expand, wup, wo, g.reshape(1, d), b.reshape(1, d))


def _router_kernel(x_ref, w_ref, b_ref, idx_ref, gate_ref):
    x = _load_rows_tiled(x_ref, idx_ref.shape[0])
    logits = jnp.dot(x, w_ref[...], preferred_element_type=F32,
                     precision=lax.Precision.HIGHEST) + b_ref[...]
    tm, lanes = logits.shape
    lane = lax.broadcasted_iota(jnp.int32, (tm, lanes), 1)
    work = jnp.where(lane < N_EXPERTS, logits, -jnp.inf)
    idx_out = jnp.zeros((tm, lanes), jnp.int32)
    val_out = jnp.zeros((tm, lanes), F32)
    top = None
    denom = jnp.zeros((tm, 1), F32)
    for k in range(TOP_K):
        mx = jnp.max(work, axis=-1, keepdims=True)
        first = jnp.min(jnp.where(work == mx, lane, lanes), axis=-1, keepdims=True)
        if top is None:
            top = mx
        e = jnp.exp(mx - top)
        denom = denom + e
        idx_out = jnp.where(lane == k, first, idx_out)
        val_out = jnp.where(lane == k, e, val_out)
        work = jnp.where(lane == first, -jnp.inf, work)
    idx_ref[...] = idx_out
    gate_ref[...] = val_out / denom


def _router(x, w, b, tm=512):
    n, d = x.shape[0] // SUBLANES, x.shape[1] * SUBLANES
    wp = jnp.zeros((d, LANES), F32).at[:, :N_EXPERTS].set(w)
    bp = jnp.zeros((1, LANES), F32).at[0, :N_EXPERTS].set(b)
    return pl.pallas_call(
        _router_kernel,
        out_shape=(jax.ShapeDtypeStruct((n, LANES), jnp.int32), jax.ShapeDtypeStruct((n, LANES), F32)),
        grid=(n // tm,),
        in_specs=[
            pl.BlockSpec((tm * SUBLANES, d // SUBLANES), lambda i: (i, 0)),
            pl.BlockSpec((d, LANES), lambda i: (0, 0)),
            pl.BlockSpec((1, LANES), lambda i: (0, 0)),
        ],
        out_specs=(pl.BlockSpec((tm, LANES), lambda i: (i, 0)), pl.BlockSpec((tm, LANES), lambda i: (i, 0))),
        compiler_params=_cparams(("parallel",)),
        name="moe_router",
    )(x, wp, bp)


def _start_row_tiles(src_hbm, idx_ref, n, dst, sem, *, per_row=1):
    def body(gi, c):
        base = gi * DMA_ISSUE_GROUP
        idx = [idx_ref[0, (base + j) * per_row + k] for j in range(DMA_ISSUE_GROUP) for k in range(per_row)]
        for j in range(DMA_ISSUE_GROUP):
            for k in range(per_row):
                _row_tile_copy(src_hbm, idx[j * per_row + k], dst.at[k] if per_row > 1 else dst, base + j,
                               sem).start(priority=k % 2)
        return c
    lax.fori_loop(0, n // DMA_ISSUE_GROUP, body, 0)


def _wait_buffer(buf, sem):
    pltpu.make_async_copy(buf, buf, sem).wait()


def _dispatch_kernel(pend_ref, nu_ref, dest_ref, x_ref, xs_hbm, zeros, sems, *, tm, n_blk):
    i = pl.program_id(0)
    blk_rows = MOE_BLOCK * SUBLANES

    def zero_block(blk):
        return pltpu.make_async_copy(zeros, xs_hbm.at[pl.ds(pl.multiple_of(blk * blk_rows, blk_rows), blk_rows), :],
                                     sems.at[1])

    def last_block(e):
        prev = jnp.where(e > 0, pend_ref[jnp.maximum(e - 1, 0)], 0)
        return pend_ref[e] > prev, pend_ref[e] // MOE_BLOCK - 1

    @pl.when(i == 0)
    def _():
        zeros[...] = jnp.zeros_like(zeros)
        for wait in (False, True):
            def per_expert(e, c):
                used, blk = last_block(e)

                @pl.when(used)
                def _():
                    zero_block(blk).wait() if wait else zero_block(blk).start()
                return c

            def per_tail(blk, c):
                zero_block(blk).wait() if wait else zero_block(blk).start()
                return c
            lax.fori_loop(0, N_EXPERTS, per_expert, 0)
            lax.fori_loop(nu_ref[0], n_blk, per_tail, 0)

    def body(gi, c):
        base = gi * DMA_ISSUE_GROUP
        idx = [dest_ref[0, (base + j) * TOP_K + k] for j in range(DMA_ISSUE_GROUP) for k in range(TOP_K)]
        for j in range(DMA_ISSUE_GROUP):
            for k in range(TOP_K):
                _row_tile_copy(x_ref, base + j, xs_hbm, idx[j * TOP_K + k], sems.at[0]).start(priority=k % 2)
        return c
    lax.fori_loop(0, tm // DMA_ISSUE_GROUP, body, 0)
    for _ in range(TOP_K):
        _wait_buffer(x_ref, sems.at[0])


def _dispatch(x, dest, pend, n_used, n_blk, tm=256):
    dl = x.shape[1]
    n = x.shape[0] // SUBLANES
    n_tiles = n // tm
    grid_spec = pltpu.PrefetchScalarGridSpec(
        num_scalar_prefetch=2,
        grid=(n_tiles,),
        in_specs=[
            pl.BlockSpec((None, 1, tm * TOP_K), lambda i, pe, nu: (i, 0, 0), memory_space=pltpu.SMEM),
            pl.BlockSpec((tm * SUBLANES, dl), lambda i, pe, nu: (i, 0)),
        ],
        out_specs=pl.BlockSpec(memory_space=pl.ANY),
        scratch_shapes=[pltpu.VMEM((MOE_BLOCK * SUBLANES, dl), F32), pltpu.SemaphoreType.DMA((2,))],
    )
    return pl.pallas_call(
        functools.partial(_dispatch_kernel, tm=tm, n_blk=n_blk),
        out_shape=jax.ShapeDtypeStruct((n_blk * MOE_BLOCK * SUBLANES, dl), F32),
        grid_spec=grid_spec,
        compiler_params=pltpu.CompilerParams(dimension_semantics=("arbitrary",), vmem_limit_bytes=VMEM_LIMIT,
                                             disable_bounds_checks=True),
        name="moe_dispatch",
    )(pend, n_used, dest.reshape(n_tiles, 1, tm * TOP_K), x)


def _expert_kernel(be_ref, nu_ref, x_ref, w1_ref, b1_ref, w2_ref, b2_ref, o_ref, w1_bf, w2_bf):
    i = pl.program_id(0)
    n_used = nu_ref[0]

    @pl.when((i < n_used) & ((i == 0) | (be_ref[i] != be_ref[jnp.maximum(i - 1, 0)])))
    def _():
        w1_bf[...] = w1_ref[...].astype(BF16)
        w2_bf[...] = w2_ref[...].astype(BF16)

    @pl.when(i < n_used)
    def _():
        dff = w2_ref.shape[0]
        x = _load_rows_tiled(x_ref, MOE_BLOCK).astype(BF16)
        h = jnp.dot(x, w1_bf[...], preferred_element_type=F32) + b1_ref[...]
        gate = jnp.minimum(h[:, 0:dff], SWIGLU_LIMIT)
        upv = jnp.clip(h[:, dff:2 * dff], -SWIGLU_LIMIT, SWIGLU_LIMIT)
        act = (upv + 1.0) * (gate * jax.nn.sigmoid(SWIGLU_ALPHA * gate))
        _store_rows_tiled(o_ref, jnp.dot(act.astype(BF16), w2_bf[...], preferred_element_type=F32) + b2_ref[...])

    @pl.when(i >= n_used)
    def _():
        o_ref[...] = jnp.zeros_like(o_ref)


def _expert_ffn(xs, blk_e, n_used, w1, b1, w2, b2, layer):
    dl = xs.shape[1]
    n_blk = xs.shape[0] // (MOE_BLOCK * SUBLANES)
    d = dl * SUBLANES
    dff = w2.shape[2]
    depth = w1.shape[0]
    grid_spec = pltpu.PrefetchScalarGridSpec(
        num_scalar_prefetch=2,
        grid=(n_blk,),
        in_specs=[
            pl.BlockSpec((MOE_BLOCK * SUBLANES, dl), lambda i, be, nu: (jnp.minimum(i, nu[0] - 1), 0)),
            pl.BlockSpec((None, None, d, 2 * dff), lambda i, be, nu: (layer, be[i], 0, 0)),
            pl.BlockSpec((None, None, 1, 2 * dff), lambda i, be, nu: (layer, be[i], 0, 0)),
            pl.BlockSpec((None, None, dff, d), lambda i, be, nu: (layer, be[i], 0, 0)),
            pl.BlockSpec((None, None, 1, d), lambda i, be, nu: (layer, be[i], 0, 0)),
        ],
        out_specs=pl.BlockSpec((MOE_BLOCK * SUBLANES, dl), lambda i, be, nu: (i, 0)),
        scratch_shapes=[pltpu.VMEM((d, 2 * dff), BF16), pltpu.VMEM((dff, d), BF16)],
    )
    return pl.pallas_call(
        _expert_kernel,
        out_shape=jax.ShapeDtypeStruct(xs.shape, F32),
        grid_spec=grid_spec,
        compiler_params=pltpu.CompilerParams(dimension_semantics=("arbitrary",), vmem_limit_bytes=EXPERT_VMEM_LIMIT),
        name="moe_experts",
    )(blk_e, n_used, xs, w1, b1.reshape(depth, N_EXPERTS, 1, 2 * dff), w2, b2.reshape(depth, N_EXPERTS, 1, d))


def _combine_kernel(row_cur_ref, row_next_ref, x_ref, gate_ref, yb_hbm, g_ref, b_ref, o_ref, buf, sems,
                    *, tm, alpha):
    i = pl.program_id(0)
    slot = lax.rem(i, 2)

    @pl.when(i == 0)
    def _():
        _start_row_tiles(yb_hbm, row_cur_ref, tm, buf.at[0], sems.at[0], per_row=TOP_K)

    @pl.when(i + 1 < pl.num_programs(0))
    def _():
        _start_row_tiles(yb_hbm, row_next_ref, tm, buf.at[1 - slot], sems.at[1 - slot], per_row=TOP_K)

    _wait_buffer(buf.at[slot], sems.at[slot])
    gates = gate_ref[...]
    y = gates[:, 0:1] * _load_rows_tiled(buf.at[slot, 0], tm)
    for k in range(1, TOP_K):
        y = y + gates[:, k:k + 1] * _load_rows_tiled(buf.at[slot, k], tm)
    o_ref[...] = _layer_norm(alpha * _load_rows_tiled(x_ref, tm) + y, g_ref[...], b_ref[...])


def _combine(x, gates, row_of, yb, g, b, alpha, tm=256):
    dl = x.shape[1]
    n, d = x.shape[0] // SUBLANES, dl * SUBLANES
    n_tiles = n // tm
    rows = row_of.reshape(n_tiles, 1, tm * TOP_K)
    return pl.pallas_call(
        functools.partial(_combine_kernel, tm=tm, alpha=alpha),
        out_shape=jax.ShapeDtypeStruct((n, d), F32),
        grid=(n_tiles,),
        in_specs=[
            pl.BlockSpec((None, 1, tm * TOP_K), lambda i: (i, 0, 0), memory_space=pltpu.SMEM),
            pl.BlockSpec((None, 1, tm * TOP_K), lambda i: (jnp.minimum(i + 1, n_tiles - 1), 0, 0),
                         memory_space=pltpu.SMEM),
            pl.BlockSpec((tm * SUBLANES, dl), lambda i: (i, 0)),
            pl.BlockSpec((tm, LANES), lambda i: (i, 0)),
            pl.BlockSpec(memory_space=pl.ANY),
            pl.BlockSpec((1, d), lambda i: (0, 0)),
            pl.BlockSpec((1, d), lambda i: (0, 0)),
        ],
        out_specs=pl.BlockSpec((tm, d), lambda i: (i, 0)),
        scratch_shapes=[pltpu.VMEM((2, TOP_K, tm * SUBLANES, dl), F32), pltpu.SemaphoreType.DMA((2,))],
        compiler_params=pltpu.CompilerParams(dimension_semantics=("arbitrary",), vmem_limit_bytes=VMEM_LIMIT,
                                             disable_bounds_checks=True),
        name="moe_combine_ln",
    )(rows, rows, x, gates, yb, g.reshape(1, d), b.reshape(1, d))


def _mixer_layer(x, w_in, b_in, pool_w, pool_b, pool_scale, cmp_pos, cmp_w1, cmp_b1, cmp_w2, cmp_b2,
                 w_up, w_o, ln_g, ln_b, alpha):
    bsz, t, d = x.shape
    n = bsz * t
    mix_w = d // 2
    g, hpg = NSA_KV_GROUPS, NSA_HPG
    nq_w = NSA_HEADS * HEAD_DIM
    nkv_w = N_BRANCH * 2 * g * HEAD_DIM
    nfox_w = 3 * FOX_HEADS * HEAD_DIM
    sizes = (mix_w, nq_w, nkv_w, N_BRANCH * NSA_HEADS, nfox_w, FOX_HEADS, N_BRANCH * d)
    o_pool, o_q, o_kv, o_g, o_fox, o_f, o_gm = (int(v) for v in np.cumsum((0,) + sizes[:-1]))
    n_gate = N_BRANCH * NSA_HEADS

    def cols(a, lo, width):
        return lax.slice_in_dim(a, lo, lo + width, axis=-1)

    qs = SCALE * LOG2E
    hw = FOX_HEADS * HEAD_DIM
    gw = g * HEAD_DIM
    pad_small = 2 * LANES - n_gate - FOX_HEADS
    kv0 = o_kv
    w_small = jnp.concatenate([cols(w_in, o_g, n_gate), cols(w_in, o_f, FOX_HEADS),
                               jnp.zeros((d, pad_small), F32)], axis=1).astype(BF16)
    b_small = jnp.concatenate([cols(b_in, o_g, n_gate), cols(b_in, o_f, FOX_HEADS), jnp.zeros((pad_small,), F32)])
    tok_cols = ((kv0, 2 * gw), (kv0 + 2 * gw, gw), (kv0 + 4 * gw, gw), (o_fox + hw, hw))
    w_k = jnp.concatenate([cols(w_in, lo, wd) for lo, wd in tok_cols], axis=1).astype(BF16)
    b_k = jnp.concatenate([cols(b_in, lo, wd) for lo, wd in tok_cols])
    feat_cols = ((o_q, nq_w, qs), (kv0 + 3 * gw, gw, 1.0), (kv0 + 5 * gw, gw, 1.0), (o_fox, hw, qs),
                 (o_fox + 2 * hw, hw, 1.0))
    w_t = jnp.concatenate([cols(w_in, lo, wd) * sc for lo, wd, sc in feat_cols], axis=1).T.astype(BF16)
    b_t = jnp.concatenate([cols(b_in, lo, wd) * sc for lo, wd, sc in feat_cols])
    r_vslc, r_vwin, r_qfox, r_vfox = nq_w, nq_w + gw, nq_w + 2 * gw, nq_w + 2 * gw + hw
    c_kslc, c_kwin, c_kfox = 2 * gw, 3 * gw, 4 * gw

    xf = x.reshape(n, d)
    u_pool = _matmul_bias(xf, cols(w_in, o_pool, mix_w).astype(BF16), cols(b_in, o_pool, mix_w), F32)
    z_small = _matmul_bias(xf, w_small, b_small, F32)
    zk = _matmul_bias(xf, w_k, b_k, BF16, tn=256)
    gm = _matmul_bias(xf, cols(w_in, o_gm, N_BRANCH * d).astype(BF16), cols(b_in, o_gm, N_BRANCH * d), BF16,
                      act="sigmoid", tn=1024)
    zt = _matmul_bias_t(x, w_t, b_t)

    y_pool = _pool_mixer(u_pool, t, pool_w, pool_b, pool_scale)

    n_chunk = t // CMP_STRIDE
    a_cmp = cols(zk, 0, 2 * gw).reshape(bsz, n_chunk, CMP_STRIDE, 2, g, HEAD_DIM)
    a_cmp = a_cmp.transpose(3, 0, 4, 1, 2, 5).reshape(2, bsz * g, n_chunk, CMP_STRIDE * HEAD_DIM)
    kvc = _compress(a_cmp, cmp_pos.reshape(2, CMP_LEN * HEAD_DIM), cmp_w1, cmp_b1, cmp_w2, cmp_b2)
    kvc = kvc.reshape(2, bsz, g, n_chunk, HEAD_DIM)
    n_s = t // SEL_LEN
    c_start = np.arange(n_chunk) * CMP_STRIDE
    s_start = np.arange(n_s) * SEL_LEN
    ov = (c_start[:, None] < s_start[None, :] + SEL_LEN) & (c_start[:, None] + CMP_LEN > s_start[None, :])
    ov[n_chunk - (CMP_LEN // CMP_STRIDE) + 1:] = False
    ovt = jnp.asarray(ov.T.astype(np.float32), BF16)
    o_cmp, qaug = _nsa_cmp(zt, kvc[0], kvc[1].transpose(0, 1, 3, 2), ovt)
    onehot = jnp.asarray((np.arange(t)[:, None] // SEL_LEN == np.arange(n_s)[None, :]).astype(np.float32), BF16)
    o_slc = _nsa_slc(qaug, onehot, zk, zt, c_kslc // LANES, r_vslc // LANES)
    o_win = _nsa_win(zt, zk, c_kwin // LANES, r_vwin // LANES)

    cum = _fox_prep(z_small, t)
    cq = cols(cum, n_gate, FOX_HEADS)
    cq_t = cq.transpose(0, 3, 1, 2)
    ones_t = jnp.ones((bsz, FOX_HEADS, 3, t), BF16)
    zero6 = jnp.zeros((bsz, FOX_HEADS, 6, t), BF16)
    even = (np.arange(FOX_HEADS) % 2 == 0)[None, :, None, None]
    augq = jnp.concatenate([jnp.where(even, jnp.concatenate([cq_t, ones_t], axis=2), zero6),
                            jnp.where(even, zero6, jnp.concatenate([cq_t, ones_t], axis=2)),
                            jnp.zeros((bsz, FOX_HEADS, 4, t), BF16)], axis=2)
    ck = cq.transpose(0, 3, 2, 1).reshape(bsz, FOX_HEADS // 2, 2, t, 3)
    ones_k = jnp.ones((bsz, FOX_HEADS // 2, t, 3), BF16)
    augk = jnp.concatenate([ones_k, -ck[:, :, 0], ones_k, -ck[:, :, 1],
                            jnp.zeros((bsz, FOX_HEADS // 2, t, LANES - 12), BF16)], axis=-1)
    y_fox = _fox_attention(zt, zk, augq, augk, r_qfox, c_kfox, r_vfox)

    hsel = np.arange(NSA_HEADS * HEAD_DIM) // HEAD_DIM
    expand = np.zeros((LANES, N_BRANCH * mix_w), np.float32)
    for br in range(N_BRANCH):
        expand[hsel * N_BRANCH + br, br * mix_w + np.arange(mix_w)] = 1.0
    return _merge(xf, y_pool, o_cmp, o_slc, o_win, y_fox, z_small, gm,
                  jnp.asarray(expand, BF16), w_up.astype(BF16), w_o.astype(BF16), ln_g, ln_b, alpha)


def _moe_layer(x, router_w, router_b, w1, b1, w2, b2, layer, ln_g, ln_b, alpha):
    n = x.shape[0] // SUBLANES
    nk = n * TOP_K
    idx_p, gate_p = _router(x, router_w, router_b)
    e_flat = idx_p[:, :TOP_K].reshape(nk)
    onehot = (e_flat[:, None] == jnp.arange(N_EXPERTS)[None, :]).astype(jnp.int32)
    incl = jnp.cumsum(onehot, axis=0)
    counts = incl[-1]
    rank = jnp.sum((incl - onehot) * onehot, axis=1)
    padded = (counts + MOE_BLOCK - 1) // MOE_BLOCK * MOE_BLOCK
    pend = jnp.cumsum(padded)
    pstart = pend - padded
    dest = pstart[e_flat] + rank
    n_blk = (nk + MOE_BLOCK - 1) // MOE_BLOCK + N_EXPERTS
    blk_e = jnp.minimum(jnp.sum(jnp.arange(n_blk)[:, None] * MOE_BLOCK >= pend[None, :], axis=1),
                        N_EXPERTS - 1).astype(jnp.int32)
    n_used = (pend[-1] // MOE_BLOCK).astype(jnp.int32).reshape(1)
    dest = dest.astype(jnp.int32)

    xs = _dispatch(x, dest, pend.astype(jnp.int32), n_used, n_blk)
    yb = _expert_ffn(xs, blk_e, n_used, w1, b1, w2, b2, layer)
    return _combine(x, gate_p, dest, yb, ln_g, ln_b, alpha)


def kernel(x, w_in, b_in, pool_w, pool_b, pool_scale, cmp_pos, cmp_w1, cmp_b1, cmp_w2, cmp_b2,
           w_up, w_o, ln1_g, ln1_b, router_w, router_b, moe_w1, moe_b1, moe_w2, moe_b2, ln2_g, ln2_b):
    depth = w_in.shape[0]
    alpha = (2 * depth) ** 0.25
    bsz, t, d = x.shape
    for l in range(depth):
        x1 = _mixer_layer(x, w_in[l], b_in[l], pool_w[l], pool_b[l], pool_scale[l], cmp_pos[l], cmp_w1[l],
                          cmp_b1[l], cmp_w2[l], cmp_b2[l], w_up[l], w_o[l], ln1_g[l], ln1_b[l], alpha)
        x2 = _moe_layer(x1, router_w[l], router_b[l], moe_w1, moe_b1, moe_w2, moe_b2, l,
                        ln2_g[l], ln2_b[l], alpha)
        x = x2.reshape(bsz, t, d)
    return x
```

```python
import functools

import numpy as np
import jax
import jax.numpy as jnp
from jax import lax
from jax.experimental import pallas as pl
from jax.experimental.pallas import tpu as pltpu

F32 = jnp.float32
BF16 = jnp.bfloat16

HEAD_DIM = 64
POOL_WINDOWS = (2, 4, 8, 16)
POOL_GC = 128
POOL_HALO = 16
NSA_HEADS = 8
NSA_KV_GROUPS = 2
NSA_HPG = NSA_HEADS // NSA_KV_GROUPS
N_BRANCH = 3
CMP_LEN = 32
CMP_STRIDE = 16
SEL_LEN = 64
N_SEL = 16
WINDOW = 512
FOX_HEADS = 8
Q_BLOCK = 128
N_EXPERTS = 32
TOP_K = 4
SWIGLU_LIMIT = 7.0
SWIGLU_ALPHA = 1.702
MOE_BLOCK = 256
LN_EPS = 1e-5
NEG_INF = -1e30
SCALE = HEAD_DIM ** -0.5
MASK_BIG = 2.0 ** 100
LOG2E = 1.4426950408889634
ONES_ROWS = 16
LANES = 128
SUBLANES = 8
VMEM_LIMIT = 48 * 1024 * 1024
EXPERT_VMEM_LIMIT = 58 * 1024 * 1024
DMA_ISSUE_GROUP = 8


def _cparams(sem):
    return pltpu.CompilerParams(dimension_semantics=sem, vmem_limit_bytes=VMEM_LIMIT)


def _split3(x):
    hi = x.astype(BF16)
    r1 = x - hi.astype(F32)
    mid = r1.astype(BF16)
    lo = (r1 - mid.astype(F32)).astype(BF16)
    return hi, mid, lo


def _layer_norm(r, g, b):
    mu = jnp.mean(r, axis=-1, keepdims=True)
    var = jnp.mean(jnp.square(r - mu), axis=-1, keepdims=True)
    return (r - mu) * lax.rsqrt(var + LN_EPS) * g + b


def _load_rows_tiled(ref, rows):
    return jnp.concatenate([ref[pl.ds(s, rows, stride=SUBLANES), :] for s in range(SUBLANES)], axis=1)


def _store_rows_tiled(ref, val):
    rows = val.shape[0]
    for s in range(SUBLANES):
        ref[pl.ds(s, rows, stride=SUBLANES), :] = val[:, s * LANES:(s + 1) * LANES]


def _row_tile_copy(src, src_row, dst, dst_row, sem):
    return pltpu.make_async_copy(src.at[pl.ds(pl.multiple_of(src_row * SUBLANES, SUBLANES), SUBLANES), :],
                                 dst.at[pl.ds(pl.multiple_of(dst_row * SUBLANES, SUBLANES), SUBLANES), :], sem)


def _mm_kernel(x_ref, w_ref, b_ref, o_ref, *, act):
    acc = jnp.dot(x_ref[...].astype(BF16), w_ref[...], preferred_element_type=F32) + b_ref[...]
    if act == "sigmoid":
        acc = jax.nn.sigmoid(acc)
    o_ref[...] = acc.astype(o_ref.dtype)


def _matmul_bias(x, w, b, out_dtype, act=None, tm=1024, tn=512):
    m, k = x.shape
    n = w.shape[1]
    tn = min(tn, n)
    assert m % tm == 0 and n % tn == 0
    return pl.pallas_call(
        functools.partial(_mm_kernel, act=act),
        out_shape=jax.ShapeDtypeStruct((m, n), out_dtype),
        grid=(m // tm, n // tn),
        in_specs=[
            pl.BlockSpec((tm, k), lambda i, j: (i, 0)),
            pl.BlockSpec((k, tn), lambda i, j: (0, j)),
            pl.BlockSpec((1, tn), lambda i, j: (0, j)),
        ],
        out_specs=pl.BlockSpec((tm, tn), lambda i, j: (i, j)),
        compiler_params=_cparams(("parallel", "arbitrary")),
        name="in_proj",
    )(x, w, b.reshape(1, n))


def _mm_split_kernel(x_ref, w_ref, b_ref, o0_ref, o1_ref):
    acc = jnp.dot(x_ref[...].astype(BF16), w_ref[...], preferred_element_type=F32) + b_ref[...]
    n0 = o0_ref.shape[1]
    o0_ref[...] = acc[:, 0:n0]
    o1_ref[...] = acc[:, n0:]


def _matmul_bias_split(x, w, b, n0, tm=1024):
    m, k = x.shape
    n = w.shape[1]
    return pl.pallas_call(
        _mm_split_kernel,
        out_shape=(jax.ShapeDtypeStruct((m, n0), F32), jax.ShapeDtypeStruct((m, n - n0), F32)),
        grid=(m // tm,),
        in_specs=[
            pl.BlockSpec((tm, k), lambda i: (i, 0)),
            pl.BlockSpec((k, n), lambda i: (0, 0)),
            pl.BlockSpec((1, n), lambda i: (0, 0)),
        ],
        out_specs=(pl.BlockSpec((tm, n0), lambda i: (i, 0)), pl.BlockSpec((tm, n - n0), lambda i: (i, 0))),
        compiler_params=_cparams(("parallel",)),
        name="in_proj_split",
    )(x, w, b.reshape(1, n))


def _mm_t_kernel(x_ref, wt_ref, b_ref, o_ref):
    acc = lax.dot_general(wt_ref[...], x_ref[...].astype(BF16), (((1,), (1,)), ((), ())),
                          preferred_element_type=F32) + b_ref[...]
    o_ref[...] = acc.astype(o_ref.dtype)


def _matmul_bias_t(x, t, wt, b, tm=1024, tf=896):
    n, k = x.shape
    bsz = n // t
    f = wt.shape[0]
    tiles = t // tm
    assert t % tm == 0 and f % tf == 0
    return pl.pallas_call(
        _mm_t_kernel,
        out_shape=jax.ShapeDtypeStruct((bsz, f, t), BF16),
        grid=(bsz, tiles, f // tf),
        in_specs=[
            pl.BlockSpec((tm, k), lambda bb, i, j: (bb * tiles + i, 0)),
            pl.BlockSpec((tf, k), lambda bb, i, j: (j, 0)),
            pl.BlockSpec((tf, 1), lambda bb, i, j: (j, 0)),
        ],
        out_specs=pl.BlockSpec((None, tf, tm), lambda bb, i, j: (bb, j, i)),
        compiler_params=_cparams(("parallel", "parallel", "arbitrary")),
        name="in_proj_t",
    )(x, wt, b.reshape(f, 1))


def _pool_kernel(prev_ref, cur_ref, w_ref, b_ref, sc_ref, o_ref, ext_ref, *, tile):
    i = pl.program_id(1)
    ext_ref[0:POOL_HALO, :] = jnp.where(i > 0, prev_ref[...], 0.0)
    ext_ref[POOL_HALO:POOL_HALO + tile, :] = cur_ref[...]
    t_idx = i * tile + lax.broadcasted_iota(jnp.int32, (tile, 1), 0)
    for gi, win in enumerate(POOL_WINDOWS):
        cols = slice(gi * POOL_GC, (gi + 1) * POOL_GC)
        u = ext_ref[POOL_HALO:POOL_HALO + tile, cols]
        wsum = u
        for j in range(1, win):
            wsum = wsum + ext_ref[POOL_HALO - j:POOL_HALO - j + tile, cols]
        cnt = jnp.minimum(t_idx + 1, win).astype(F32)
        d = wsum / cnt - u
        y = jnp.dot(d.astype(BF16), w_ref[gi], preferred_element_type=F32) + b_ref[gi]
        o_ref[:, cols] = (y * sc_ref[:, cols]).astype(o_ref.dtype)


def _pool_mixer(u, t, w, b, scale, tile=512):
    n, c = u.shape
    bsz = n // t
    tiles, halos = t // tile, t // POOL_HALO
    return pl.pallas_call(
        functools.partial(_pool_kernel, tile=tile),
        out_shape=jax.ShapeDtypeStruct((n, c), BF16),
        grid=(bsz, tiles),
        in_specs=[
            pl.BlockSpec((POOL_HALO, c),
                         lambda bb, i: (bb * halos + jnp.maximum(i * (tile // POOL_HALO) - 1, 0), 0)),
            pl.BlockSpec((tile, c), lambda bb, i: (bb * tiles + i, 0)),
            pl.BlockSpec((len(POOL_WINDOWS), POOL_GC, POOL_GC), lambda bb, i: (0, 0, 0)),
            pl.BlockSpec((len(POOL_WINDOWS), 1, POOL_GC), lambda bb, i: (0, 0, 0)),
            pl.BlockSpec((1, c), lambda bb, i: (0, 0)),
        ],
        out_specs=pl.BlockSpec((tile, c), lambda bb, i: (bb * tiles + i, 0)),
        scratch_shapes=[pltpu.VMEM((POOL_HALO + tile, c), F32)],
        compiler_params=_cparams(("parallel", "arbitrary")),
        name="pool_mixer",
    )(u, u, w.astype(BF16), b.reshape(len(POOL_WINDOWS), 1, POOL_GC), scale.reshape(1, c))


def _compress_kernel(a_ref, pos_ref, w1_ref, b1_ref, w2_ref, b2_ref, o_ref):
    half = CMP_STRIDE * HEAD_DIM
    a = a_ref[...].astype(F32)
    n_chunk = a.shape[0]
    top = (a + pos_ref[:, 0:half]).astype(BF16)
    bot = (a + pos_ref[:, half:2 * half]).astype(BF16)
    p1 = jnp.dot(top, w1_ref[0:half, :], preferred_element_type=F32)
    p2 = jnp.dot(bot, w1_ref[half:2 * half, :], preferred_element_type=F32)
    h = p1 + pltpu.roll(p2, n_chunk - 1, axis=0) + b1_ref[...]
    h = jax.nn.gelu(h)
    o = jnp.dot(h.astype(BF16), w2_ref[...], preferred_element_type=F32) + b2_ref[...]
    o_ref[...] = o.astype(o_ref.dtype)


def _compress(a, pos, w1, b1, w2, b2):
    _, bg, n_chunk, width = a.shape
    hid = w1.shape[-1]
    return pl.pallas_call(
        _compress_kernel,
        out_shape=jax.ShapeDtypeStruct((2, bg, n_chunk, HEAD_DIM), BF16),
        grid=(2, bg),
        in_specs=[
            pl.BlockSpec((None, None, n_chunk, width), lambda s, j: (s, j, 0, 0)),
            pl.BlockSpec((None, 1, 2 * width), lambda s, j: (s, 0, 0)),
            pl.BlockSpec((None, 2 * width, hid), lambda s, j: (s, 0, 0)),
            pl.BlockSpec((None, 1, hid), lambda s, j: (s, 0, 0)),
            pl.BlockSpec((None, hid, HEAD_DIM), lambda s, j: (s, 0, 0)),
            pl.BlockSpec((None, 1, HEAD_DIM), lambda s, j: (s, 0, 0)),
        ],
        out_specs=pl.BlockSpec((None, None, n_chunk, HEAD_DIM), lambda s, j: (s, j, 0, 0)),
        compiler_params=_cparams(("arbitrary", "arbitrary")),
        name="nsa_compress",
    )(a, pos.reshape(2, 1, 2 * width), w1.astype(BF16), b1.reshape(2, 1, hid),
      w2.astype(BF16), b2.reshape(2, 1, HEAD_DIM))


def _flash_steps_t(qts, ks, vts, carries, mask):
    scores = [jnp.dot(k, qt, preferred_element_type=F32) for k, qt in zip(ks, qts)]
    if mask is not None:
        masks = mask if isinstance(mask, (list, tuple)) else [mask] * len(scores)
        scores = [jnp.where(mk, s, NEG_INF) for mk, s in zip(masks, scores)]
    probs, stats = [], []
    for s, (m, _) in zip(scores, carries):
        m_new = jnp.maximum(m, jnp.max(s, axis=0, keepdims=True))
        probs.append(jnp.exp2(s - m_new).astype(BF16))
        stats.append((m_new, jnp.exp2(m - m_new)))
    return tuple((m_new, alpha * acc + jnp.dot(vt, p, preferred_element_type=F32))
                 for vt, p, (m_new, alpha), (_, acc) in zip(vts, probs, stats, carries))


def _flash_init_t(cols):
    return (jnp.full((1, cols), NEG_INF, F32), jnp.zeros((HEAD_DIM + ONES_ROWS, cols), F32))


def _flash_finish_t(acc):
    return acc[0:HEAD_DIM] / acc[HEAD_DIM:HEAD_DIM + 1]


def _with_ones(vt):
    return jnp.concatenate([vt, jnp.ones((ONES_ROWS, vt.shape[1]), vt.dtype)], axis=0)


def _group_queries(qt_ref, g):
    return jnp.concatenate([qt_ref[(g * NSA_HPG + h) * HEAD_DIM:(g * NSA_HPG + h + 1) * HEAD_DIM, :]
                            for h in range(NSA_HPG)], axis=1)


def _slot_rows(x, slot, n_slots):
    z = jnp.zeros_like(x)
    return jnp.concatenate([x if j == slot else z for j in range(n_slots)], axis=0)


def _store_group(o_ref, g, o, col0=0):
    for h in range(NSA_HPG):
        r0 = (g * NSA_HPG + h) * HEAD_DIM
        o_ref[r0:r0 + HEAD_DIM, col0:col0 + Q_BLOCK] = o[:, h * Q_BLOCK:(h + 1) * Q_BLOCK].astype(o_ref.dtype)


def _nsa_cmp_kernel(qt_ref, kc_ref, vct_ref, ovt_ref, ocmp_ref, qaug_ref):
    groups = range(kc_ref.shape[0])
    cols = NSA_HPG * Q_BLOCK
    t0 = pl.program_id(1) * Q_BLOCK
    n_c = kc_ref.shape[1]
    n_s = ovt_ref.shape[0]
    qts = [_group_queries(qt_ref, g) for g in groups]
    tq = t0 + (lax.broadcasted_iota(jnp.int32, (n_c, cols), 1) & (Q_BLOCK - 1))
    c_end = lax.broadcasted_iota(jnp.int32, (n_c, cols), 0) * CMP_STRIDE + (CMP_LEN - 1)
    mask = c_end <= tq
    ps = []
    for g in groups:
        sm = jnp.where(mask, jnp.dot(kc_ref[g], qts[g], preferred_element_type=F32), NEG_INF)
        e = jnp.where(mask, jnp.exp2(sm - jnp.max(sm, axis=0, keepdims=True)), 0.0)
        l = jnp.sum(e, axis=0, keepdims=True)
        ps.append(e / jnp.where(l > 0.0, l, 1.0))
    for g in groups:
        _store_group(ocmp_ref, g, jnp.dot(vct_ref[g], ps[g].astype(BF16), preferred_element_type=F32))

    ovt = ovt_ref[...]
    blk = lax.broadcasted_iota(jnp.int32, (n_s, Q_BLOCK), 0)
    tqq = t0 + lax.broadcasted_iota(jnp.int32, (n_s, Q_BLOCK), 1)
    future = blk * SEL_LEN > tqq
    cur = lax.shift_right_logical(tqq, 6)
    forced = (blk == 0) | (blk == cur) | (blk == cur - 1)
    scores = []
    for g in groups:
        psum = ps[g][:, 0:Q_BLOCK]
        for h in range(1, NSA_HPG):
            psum = psum + ps[g][:, h * Q_BLOCK:(h + 1) * Q_BLOCK]
        imp = None
        for part in _split3(psum):
            term = jnp.dot(ovt, part, preferred_element_type=F32)
            imp = term if imp is None else imp + term
        scores.append(jnp.where(future, -1.0, jnp.where(forced, 1e6, imp)))
    sels = [jnp.zeros((n_s, Q_BLOCK), F32) for _ in groups]
    for _ in range(min(N_SEL, n_s)):
        for g in groups:
            mx = jnp.max(scores[g], axis=0, keepdims=True)
            first = jnp.min(jnp.where(scores[g] == mx, blk, n_s), axis=0, keepdims=True)
            pick = blk == first
            sels[g] = jnp.where(pick, 1.0, sels[g])
            scores[g] = jnp.where(pick, -2.0, scores[g])
    for g in groups:
        sel = jnp.where(future, 0.0, sels[g])
        bias = ((sel - 1.0) * MASK_BIG).astype(BF16)
        for h in range(NSA_HPG):
            qaug_ref[g, 0:n_s, h * Q_BLOCK:(h + 1) * Q_BLOCK] = bias
        qaug_ref[g, n_s:n_s + len(groups) * HEAD_DIM, :] = _slot_rows(qts[g], g, len(groups))


def _nsa_cmp(zt, kc, vct, ovt):
    bsz, _, t = zt.shape
    g, n_c = kc.shape[1], kc.shape[2]
    n_s = ovt.shape[0]
    nq = t // Q_BLOCK
    rows_q = NSA_HEADS * HEAD_DIM
    cols = NSA_HPG * Q_BLOCK
    return pl.pallas_call(
        _nsa_cmp_kernel,
        out_shape=(jax.ShapeDtypeStruct((bsz, rows_q, t), BF16),
                   jax.ShapeDtypeStruct((bsz, g, nq, n_s + g * HEAD_DIM, cols), BF16)),
        grid=(bsz, nq),
        in_specs=[
            pl.BlockSpec((None, rows_q, Q_BLOCK), lambda b, i: (b, 0, i)),
            pl.BlockSpec((None, g, n_c, HEAD_DIM), lambda b, i: (b, 0, 0, 0)),
            pl.BlockSpec((None, g, HEAD_DIM, n_c), lambda b, i: (b, 0, 0, 0)),
            pl.BlockSpec((n_s, n_c), lambda b, i: (0, 0)),
        ],
        out_specs=(
            pl.BlockSpec((None, rows_q, Q_BLOCK), lambda b, i: (b, 0, i)),
            pl.BlockSpec((None, g, None, n_s + g * HEAD_DIM, cols), lambda b, i: (b, 0, i, 0, 0)),
        ),
        compiler_params=_cparams(("parallel", "arbitrary")),
        name="nsa_cmp_select",
    )(zt, kc, vct, ovt)


def _nsa_slc_kernel(qaug_ref, onehot_ref, k_ref, vt_ref, o_ref, *, tk, qb):
    groups = qaug_ref.shape[0]
    cols = NSA_HPG * Q_BLOCK
    t0 = pl.program_id(1) * (qb * Q_BLOCK)
    jd = t0 // tk
    probs = [(g, b) for g in range(groups) for b in range(qb)]

    def chunk(j, carries, mask):
        start = pl.multiple_of(j * tk, tk)
        k = jnp.concatenate([onehot_ref[pl.ds(start, tk), :], k_ref[pl.ds(start, tk), :]], axis=1)
        vts = [_with_ones(vt_ref[g * HEAD_DIM:(g + 1) * HEAD_DIM, pl.ds(start, tk)]) for g in range(groups)]
        return _flash_steps_t([qaug_ref[g, b] for g, b in probs], [k for _ in probs],
                              [vts[g] for g, _ in probs], carries, mask)

    carries = lax.fori_loop(0, jd, lambda j, c: chunk(j, c, None), tuple(_flash_init_t(cols) for _ in probs))
    kpos = jd * tk + lax.broadcasted_iota(jnp.int32, (tk, cols), 0)
    tq = t0 + (lax.broadcasted_iota(jnp.int32, (tk, cols), 1) & (Q_BLOCK - 1))
    causal = [kpos <= tq + b * Q_BLOCK for b in range(qb)]
    carries = chunk(jd, carries, [causal[b] for _, b in probs])
    for (g, b), (_, acc) in zip(probs, carries):
        _store_group(o_ref, g, _flash_finish_t(acc), b * Q_BLOCK)


def _nsa_slc(qaug, onehot, zk, zt, k_blk, vt_blk, tk=512, qb=2):
    bsz, g, nq, wa, cols = qaug.shape
    t = onehot.shape[0]
    tk = min(tk, t)
    assert tk % (qb * Q_BLOCK) == 0 and nq % qb == 0 and g * HEAD_DIM == LANES
    return pl.pallas_call(
        functools.partial(_nsa_slc_kernel, tk=tk, qb=qb),
        out_shape=jax.ShapeDtypeStruct((bsz, NSA_HEADS * HEAD_DIM, t), BF16),
        grid=(bsz, nq // qb),
        in_specs=[
            pl.BlockSpec((None, g, qb, wa, cols), lambda b, i: (b, 0, i, 0, 0)),
            pl.BlockSpec(onehot.shape, lambda b, i: (0, 0)),
            pl.BlockSpec((t, LANES), lambda b, i: (b, k_blk)),
            pl.BlockSpec((None, LANES, t), lambda b, i: (b, vt_blk, 0)),
        ],
        out_specs=pl.BlockSpec((None, NSA_HEADS * HEAD_DIM, qb * Q_BLOCK), lambda b, i: (b, 0, i)),
        compiler_params=_cparams(("parallel", "arbitrary")),
        name="nsa_selected",
    )(qaug, onehot, zk, zt)


def _nsa_win_kernel(qt_ref, k_ref, vt_ref, o_ref):
    groups = NSA_KV_GROUPS
    cols = NSA_HPG * Q_BLOCK
    span = WINDOW + Q_BLOCK
    t0 = pl.program_id(1) * Q_BLOCK
    start = pl.multiple_of(jnp.maximum(t0 - WINDOW, 0), Q_BLOCK)
    dist = (t0 - start) + (lax.broadcasted_iota(jnp.int32, (span, cols), 1) & (Q_BLOCK - 1)) \
        - lax.broadcasted_iota(jnp.int32, (span, cols), 0)
    mask = (dist >= 0) & (dist < WINDOW)
    k = k_ref[pl.ds(start, span), :]
    scores = [jnp.where(mask, jnp.dot(k, _slot_rows(_group_queries(qt_ref, g), g, groups),
                                      preferred_element_type=F32), NEG_INF) for g in range(groups)]
    probs = [jnp.exp2(s - jnp.max(s, axis=0, keepdims=True)).astype(BF16) for s in scores]
    for g in range(groups):
        vt = _with_ones(vt_ref[g * HEAD_DIM:(g + 1) * HEAD_DIM, pl.ds(start, span)])
        _store_group(o_ref, g, _flash_finish_t(jnp.dot(vt, probs[g], preferred_element_type=F32)))


def _nsa_win(zt, zk, k_blk, vt_blk):
    bsz, _, t = zt.shape
    rows_q = NSA_HEADS * HEAD_DIM
    assert t >= WINDOW + Q_BLOCK and NSA_KV_GROUPS * HEAD_DIM == LANES
    return pl.pallas_call(
        _nsa_win_kernel,
        out_shape=jax.ShapeDtypeStruct((bsz, rows_q, t), BF16),
        grid=(bsz, t // Q_BLOCK),
        in_specs=[
            pl.BlockSpec((None, rows_q, Q_BLOCK), lambda b, i: (b, 0, i)),
            pl.BlockSpec((t, LANES), lambda b, i: (b, k_blk)),
            pl.BlockSpec((None, LANES, t), lambda b, i: (b, vt_blk, 0)),
        ],
        out_specs=pl.BlockSpec((None, rows_q, Q_BLOCK), lambda b, i: (b, 0, i)),
        compiler_params=_cparams(("parallel", "arbitrary")),
        name="nsa_window",
    )(zt, zk, zt)


def _fox_prep_kernel(z_ref, tri_ref, o_ref, carry_ref):
    @pl.when(pl.program_id(1) == 0)
    def _():
        carry_ref[...] = jnp.zeros_like(carry_ref)

    lf = jax.nn.log_sigmoid(z_ref[...])
    hi, mid, lo = _split3(lf)
    tri = tri_ref[...]
    cum = (jnp.dot(tri, hi, preferred_element_type=F32) + jnp.dot(tri, mid, preferred_element_type=F32)
           + jnp.dot(tri, lo, preferred_element_type=F32)) + carry_ref[...]
    carry_ref[...] = cum[cum.shape[0] - 1:cum.shape[0], :]
    for part, val in enumerate(_split3(cum * LOG2E)):
        o_ref[part] = val


def _fox_prep(z_small, t, tile=256):
    bsz = z_small.shape[0] // t
    tiles = t // tile
    tri = jnp.asarray(np.tril(np.ones((tile, tile), np.float32)), BF16)
    return pl.pallas_call(
        _fox_prep_kernel,
        out_shape=jax.ShapeDtypeStruct((bsz, 3, t, LANES), BF16),
        grid=(bsz, tiles),
        in_specs=[
            pl.BlockSpec((tile, LANES), lambda b, i: (b * tiles + i, 0)),
            pl.BlockSpec((tile, tile), lambda b, i: (0, 0)),
        ],
        out_specs=pl.BlockSpec((None, 3, tile, LANES), lambda b, i: (b, 0, i, 0)),
        scratch_shapes=[pltpu.VMEM((1, LANES), F32)],
        compiler_params=_cparams(("parallel", "arbitrary")),
        name="fox_decay_cumsum",
    )(z_small, tri)


def _fox_kernel(qt_ref, augq_ref, k_ref, augk_ref, vt_ref, o_ref, *, tq, heads):
    i = pl.program_id(2)
    pair_w = LANES // HEAD_DIM
    qts = []
    for h in range(heads):
        q = _slot_rows(qt_ref[h * HEAD_DIM:(h + 1) * HEAD_DIM, :], h % pair_w, pair_w)
        aug = augq_ref[h]
        qts.append(jnp.concatenate([q, aug, jnp.zeros((LANES - aug.shape[0], tq), aug.dtype)], axis=0))

    def chunk(j, carries, mask):
        start = pl.multiple_of(j * tq, tq)
        ks = [jnp.concatenate([k_ref[pl.ds(start, tq), p * LANES:(p + 1) * LANES], augk_ref[p, pl.ds(start, tq), :]],
                              axis=1) for p in range(heads // pair_w)]
        return _flash_steps_t(qts, [ks[h // pair_w] for h in range(heads)],
                              [_with_ones(vt_ref[h * HEAD_DIM:(h + 1) * HEAD_DIM, pl.ds(start, tq)])
                               for h in range(heads)], carries, mask)

    carries = lax.fori_loop(0, i, lambda j, c: chunk(j, c, None), tuple(_flash_init_t(tq) for _ in range(heads)))
    causal = lax.broadcasted_iota(jnp.int32, (tq, tq), 0) <= lax.broadcasted_iota(jnp.int32, (tq, tq), 1)
    carries = chunk(i, carries, causal)
    for h in range(heads):
        o_ref[h * HEAD_DIM:(h + 1) * HEAD_DIM, :] = _flash_finish_t(carries[h][1]).astype(o_ref.dtype)


def _fox_attention(zt, zk, augq, augk, q_row0, k_col0, v_row0, tq=512, heads=4):
    bsz, _, t = zt.shape
    tq = min(tq, t)
    rows = heads * HEAD_DIM
    pairs = heads * HEAD_DIM // LANES
    assert q_row0 % rows == 0 and v_row0 % rows == 0 and k_col0 % rows == 0
    return pl.pallas_call(
        functools.partial(_fox_kernel, tq=tq, heads=heads),
        out_shape=jax.ShapeDtypeStruct((bsz, FOX_HEADS * HEAD_DIM, t), BF16),
        grid=(bsz, FOX_HEADS // heads, t // tq),
        in_specs=[
            pl.BlockSpec((None, rows, tq), lambda b, hh, i: (b, q_row0 // rows + hh, i)),
            pl.BlockSpec((None, heads, augq.shape[2], tq), lambda b, hh, i: (b, hh, 0, i)),
            pl.BlockSpec((t, rows), lambda b, hh, i: (b, k_col0 // rows + hh)),
            pl.BlockSpec((None, pairs, t, LANES), lambda b, hh, i: (b, hh, 0, 0)),
            pl.BlockSpec((None, rows, t), lambda b, hh, i: (b, v_row0 // rows + hh, 0)),
        ],
        out_specs=pl.BlockSpec((None, rows, tq), lambda b, hh, i: (b, hh, i)),
        compiler_params=_cparams(("parallel", "parallel", "arbitrary")),
        name="fox_attention",
    )(zt, augq, zk, augk, zt)


def _merge_kernel(x_ref, yp_ref, oc_ref, os_ref, ow_ref, yf_ref, zs_ref, gm_ref, ex_ref,
                  wup_ref, wo_ref, g_ref, b_ref, o_ref, *, alpha):
    mix_w = yp_ref.shape[-1]
    d = x_ref.shape[-1]
    sg = jax.nn.sigmoid(zs_ref[:, 0:LANES])
    hi, mid, lo = _split3(sg)
    ex = ex_ref[...]
    ge = (jnp.dot(hi, ex, preferred_element_type=F32) + jnp.dot(mid, ex, preferred_element_type=F32)
          + jnp.dot(lo, ex, preferred_element_type=F32))
    yn = (ge[:, 0:mix_w] * oc_ref[...].astype(F32).T + ge[:, mix_w:2 * mix_w] * os_ref[...].astype(F32).T
          + ge[:, 2 * mix_w:3 * mix_w] * ow_ref[...].astype(F32).T)
    ys = (yp_ref[...], yn.astype(BF16), yf_ref[...].astype(F32).T.astype(BF16))
    mix = None
    for n in range(N_BRANCH):
        up = jnp.dot(ys[n], wup_ref[n], preferred_element_type=F32)
        term = gm_ref[:, n * d:(n + 1) * d].astype(F32) * up
        mix = term if mix is None else mix + term
    h = jnp.dot(mix.astype(BF16), wo_ref[...], preferred_element_type=F32)
    _store_rows_tiled(o_ref, _layer_norm(alpha * x_ref[...] + h, g_ref[...], b_ref[...]))


def _merge(x, yp, oc, osl, ow, yf, zs, gm, expand, wup, wo, g, b, alpha, tm=512):
    n, d = x.shape
    mix_w = yp.shape[1]
    tiles_per_seq = oc.shape[2] // tm
    tok = lambda width: pl.BlockSpec((tm, width), lambda i: (i, 0))
    feat = pl.BlockSpec((None, mix_w, tm), lambda i: (i // tiles_per_seq, 0, i % tiles_per_seq))
    return pl.pallas_call(
        functools.partial(_merge_kernel, alpha=alpha),
        out_shape=jax.ShapeDtypeStruct((n * SUBLANES, d // SUBLANES), F32),
        grid=(n // tm,),
        in_specs=[
            tok(d), tok(mix_w), feat, feat, feat, feat,
            pl.BlockSpec((tm, 2 * LANES), lambda i: (i, 0)),
            tok(N_BRANCH * d),
            pl.BlockSpec((LANES, N_BRANCH * mix_w), lambda i: (0, 0)),
            pl.BlockSpec((N_BRANCH, mix_w, d), lambda i: (0, 0, 0)),
            pl.BlockSpec((d, d), lambda i: (0, 0)),
            pl.BlockSpec((1, d), lambda i: (0, 0)),
            pl.BlockSpec((1, d), lambda i: (0, 0)),
        ],
        out_specs=pl.BlockSpec((tm * SUBLANES, d // SUBLANES), lambda i: (i, 0)),
        compiler_params=_cparams(("parallel",)),
        name="merge_outproj_ln",
    )(x, yp, oc, osl, ow, yf, zs, gm, expand, wup, wo, g.reshape(1, d), b.reshape(1, d))


def _router_kernel(x_ref, w_ref, b_ref, idx_ref, gate_ref):
    x = _load_rows_tiled(x_ref, idx_ref.shape[0])
    logits = jnp.dot(x, w_ref[...], preferred_element_type=F32,
                     precision=lax.Precision.HIGHEST) + b_ref[...]
    tm, lanes = logits.shape
    lane = lax.broadcasted_iota(jnp.int32, (tm, lanes), 1)
    work = jnp.where(lane < N_EXPERTS, logits, -jnp.inf)
    idx_out = jnp.zeros((tm, lanes), jnp.int32)
    val_out = jnp.zeros((tm, lanes), F32)
    top = None
    denom = jnp.zeros((tm, 1), F32)
    for k in range(TOP_K):
        mx = jnp.max(work, axis=-1, keepdims=True)
        first = jnp.min(jnp.where(work == mx, lane, lanes), axis=-1, keepdims=True)
        if top is None:
            top = mx
        e = jnp.exp(mx - top)
        denom = denom + e
        idx_out = jnp.where(lane == k, first, idx_out)
        val_out = jnp.where(lane == k, e, val_out)
        work = jnp.where(lane == first, -jnp.inf, work)
    idx_ref[...] = idx_out
    gate_ref[...] = val_out / denom


def _router(x, w, b, tm=512):
    n, d = x.shape[0] // SUBLANES, x.shape[1] * SUBLANES
    wp = jnp.zeros((d, LANES), F32).at[:, :N_EXPERTS].set(w)
    bp = jnp.zeros((1, LANES), F32).at[0, :N_EXPERTS].set(b)
    return pl.pallas_call(
        _router_kernel,
        out_shape=(jax.ShapeDtypeStruct((n, LANES), jnp.int32), jax.ShapeDtypeStruct((n, LANES), F32)),
        grid=(n // tm,),
        in_specs=[
            pl.BlockSpec((tm * SUBLANES, d // SUBLANES), lambda i: (i, 0)),
            pl.BlockSpec((d, LANES), lambda i: (0, 0)),
            pl.BlockSpec((1, LANES), lambda i: (0, 0)),
        ],
        out_specs=(pl.BlockSpec((tm, LANES), lambda i: (i, 0)), pl.BlockSpec((tm, LANES), lambda i: (i, 0))),
        compiler_params=_cparams(("parallel",)),
        name="moe_router",
    )(x, wp, bp)


def _start_row_tiles(src_hbm, idx_ref, n, dst, sem, *, per_row=1):
    def body(gi, c):
        base = gi * DMA_ISSUE_GROUP
        idx = [idx_ref[0, (base + j) * per_row + k] for j in range(DMA_ISSUE_GROUP) for k in range(per_row)]
        for j in range(DMA_ISSUE_GROUP):
            for k in range(per_row):
                _row_tile_copy(src_hbm, idx[j * per_row + k], dst.at[k] if per_row > 1 else dst, base + j,
                               sem).start(priority=k % 2)
        return c
    lax.fori_loop(0, n // DMA_ISSUE_GROUP, body, 0)


def _wait_buffer(buf, sem):
    pltpu.make_async_copy(buf, buf, sem).wait()


def _dispatch_kernel(pend_ref, nu_ref, dest_ref, x_ref, xs_hbm, zeros, sems, *, tm, n_blk):
    i = pl.program_id(0)
    blk_rows = MOE_BLOCK * SUBLANES

    def zero_block(blk):
        return pltpu.make_async_copy(zeros, xs_hbm.at[pl.ds(pl.multiple_of(blk * blk_rows, blk_rows), blk_rows), :],
                                     sems.at[1])

    def last_block(e):
        prev = jnp.where(e > 0, pend_ref[jnp.maximum(e - 1, 0)], 0)
        return pend_ref[e] > prev, pend_ref[e] // MOE_BLOCK - 1

    @pl.when(i == 0)
    def _():
        zeros[...] = jnp.zeros_like(zeros)
        for wait in (False, True):
            def per_expert(e, c):
                used, blk = last_block(e)

                @pl.when(used)
                def _():
                    zero_block(blk).wait() if wait else zero_block(blk).start()
                return c

            def per_tail(blk, c):
                zero_block(blk).wait() if wait else zero_block(blk).start()
                return c
            lax.fori_loop(0, N_EXPERTS, per_expert, 0)
            lax.fori_loop(nu_ref[0], n_blk, per_tail, 0)

    def body(gi, c):
        base = gi * DMA_ISSUE_GROUP
        idx = [dest_ref[0, (base + j) * TOP_K + k] for j in range(DMA_ISSUE_GROUP) for k in range(TOP_K)]
        for j in range(DMA_ISSUE_GROUP):
            for k in range(TOP_K):
                _row_tile_copy(x_ref, base + j, xs_hbm, idx[j * TOP_K + k], sems.at[0]).start(priority=k % 2)
        return c
    lax.fori_loop(0, tm // DMA_ISSUE_GROUP, body, 0)
    for _ in range(TOP_K):
        _wait_buffer(x_ref, sems.at[0])


def _dispatch(x, dest, pend, n_used, n_blk, tm=256):
    dl = x.shape[1]
    n = x.shape[0] // SUBLANES
    n_tiles = n // tm
    grid_spec = pltpu.PrefetchScalarGridSpec(
        num_scalar_prefetch=2,
        grid=(n_tiles,),
        in_specs=[
            pl.BlockSpec((None, 1, tm * TOP_K), lambda i, pe, nu: (i, 0, 0), memory_space=pltpu.SMEM),
            pl.BlockSpec((tm * SUBLANES, dl), lambda i, pe, nu: (i, 0)),
        ],
        out_specs=pl.BlockSpec(memory_space=pl.ANY),
        scratch_shapes=[pltpu.VMEM((MOE_BLOCK * SUBLANES, dl), F32), pltpu.SemaphoreType.DMA((2,))],
    )
    return pl.pallas_call(
        functools.partial(_dispatch_kernel, tm=tm, n_blk=n_blk),
        out_shape=jax.ShapeDtypeStruct((n_blk * MOE_BLOCK * SUBLANES, dl), F32),
        grid_spec=grid_spec,
        compiler_params=pltpu.CompilerParams(dimension_semantics=("arbitrary",), vmem_limit_bytes=VMEM_LIMIT,
                                             disable_bounds_checks=True),
        name="moe_dispatch",
    )(pend, n_used, dest.reshape(n_tiles, 1, tm * TOP_K), x)


def _expert_kernel(be_ref, nu_ref, x_ref, w1_ref, b1_ref, w2_ref, b2_ref, o_ref, w1_bf, w2_bf):
    i = pl.program_id(0)
    n_used = nu_ref[0]

    @pl.when((i < n_used) & ((i == 0) | (be_ref[i] != be_ref[jnp.maximum(i - 1, 0)])))
    def _():
        w1_bf[...] = w1_ref[...].astype(BF16)
        w2_bf[...] = w2_ref[...].astype(BF16)

    @pl.when(i < n_used)
    def _():
        dff = w2_ref.shape[0]
        x = _load_rows_tiled(x_ref, MOE_BLOCK).astype(BF16)
        h = jnp.dot(x, w1_bf[...], preferred_element_type=F32) + b1_ref[...]
        gate = jnp.minimum(h[:, 0:dff], SWIGLU_LIMIT)
        upv = jnp.clip(h[:, dff:2 * dff], -SWIGLU_LIMIT, SWIGLU_LIMIT)
        act = (upv + 1.0) * (gate * jax.nn.sigmoid(SWIGLU_ALPHA * gate))
        _store_rows_tiled(o_ref, jnp.dot(act.astype(BF16), w2_bf[...], preferred_element_type=F32) + b2_ref[...])

    @pl.when(i >= n_used)
    def _():
        o_ref[...] = jnp.zeros_like(o_ref)


def _expert_ffn(xs, blk_e, n_used, w1, b1, w2, b2, layer):
    dl = xs.shape[1]
    n_blk = xs.shape[0] // (MOE_BLOCK * SUBLANES)
    d = dl * SUBLANES
    dff = w2.shape[2]
    depth = w1.shape[0]
    grid_spec = pltpu.PrefetchScalarGridSpec(
        num_scalar_prefetch=2,
        grid=(n_blk,),
        in_specs=[
            pl.BlockSpec((MOE_BLOCK * SUBLANES, dl), lambda i, be, nu: (jnp.minimum(i, nu[0] - 1), 0)),
            pl.BlockSpec((None, None, d, 2 * dff), lambda i, be, nu: (layer, be[i], 0, 0)),
            pl.BlockSpec((None, None, 1, 2 * dff), lambda i, be, nu: (layer, be[i], 0, 0)),
            pl.BlockSpec((None, None, dff, d), lambda i, be, nu: (layer, be[i], 0, 0)),
            pl.BlockSpec((None, None, 1, d), lambda i, be, nu: (layer, be[i], 0, 0)),
        ],
        out_specs=pl.BlockSpec((MOE_BLOCK * SUBLANES, dl), lambda i, be, nu: (i, 0)),
        scratch_shapes=[pltpu.VMEM((d, 2 * dff), BF16), pltpu.VMEM((dff, d), BF16)],
    )
    return pl.pallas_call(
        _expert_kernel,
        out_shape=jax.ShapeDtypeStruct(xs.shape, F32),
        grid_spec=grid_spec,
        compiler_params=pltpu.CompilerParams(dimension_semantics=("arbitrary",), vmem_limit_bytes=EXPERT_VMEM_LIMIT),
        name="moe_experts",
    )(blk_e, n_used, xs, w1, b1.reshape(depth, N_EXPERTS, 1, 2 * dff), w2, b2.reshape(depth, N_EXPERTS, 1, d))


def _combine_kernel(row_cur_ref, row_next_ref, x_ref, gate_ref, yb_hbm, g_ref, b_ref, o_ref, buf, sems,
                    *, tm, alpha):
    i = pl.program_id(0)
    slot = lax.rem(i, 2)

    @pl.when(i == 0)
    def _():
        _start_row_tiles(yb_hbm, row_cur_ref, tm, buf.at[0], sems.at[0], per_row=TOP_K)

    @pl.when(i + 1 < pl.num_programs(0))
    def _():
        _start_row_tiles(yb_hbm, row_next_ref, tm, buf.at[1 - slot], sems.at[1 - slot], per_row=TOP_K)

    _wait_buffer(buf.at[slot], sems.at[slot])
    gates = gate_ref[...]
    y = gates[:, 0:1] * _load_rows_tiled(buf.at[slot, 0], tm)
    for k in range(1, TOP_K):
        y = y + gates[:, k:k + 1] * _load_rows_tiled(buf.at[slot, k], tm)
    o_ref[...] = _layer_norm(alpha * _load_rows_tiled(x_ref, tm) + y, g_ref[...], b_ref[...])


def _combine(x, gates, row_of, yb, g, b, alpha, tm=256):
    dl = x.shape[1]
    n, d = x.shape[0] // SUBLANES, dl * SUBLANES
    n_tiles = n // tm
    rows = row_of.reshape(n_tiles, 1, tm * TOP_K)
    return pl.pallas_call(
        functools.partial(_combine_kernel, tm=tm, alpha=alpha),
        out_shape=jax.ShapeDtypeStruct((n, d), F32),
        grid=(n_tiles,),
        in_specs=[
            pl.BlockSpec((None, 1, tm * TOP_K), lambda i: (i, 0, 0), memory_space=pltpu.SMEM),
            pl.BlockSpec((None, 1, tm * TOP_K), lambda i: (jnp.minimum(i + 1, n_tiles - 1), 0, 0),
                         memory_space=pltpu.SMEM),
            pl.BlockSpec((tm * SUBLANES, dl), lambda i: (i, 0)),
            pl.BlockSpec((tm, LANES), lambda i: (i, 0)),
            pl.BlockSpec(memory_space=pl.ANY),
            pl.BlockSpec((1, d), lambda i: (0, 0)),
            pl.BlockSpec((1, d), lambda i: (0, 0)),
        ],
        out_specs=pl.BlockSpec((tm, d), lambda i: (i, 0)),
        scratch_shapes=[pltpu.VMEM((2, TOP_K, tm * SUBLANES, dl), F32), pltpu.SemaphoreType.DMA((2,))],
        compiler_params=pltpu.CompilerParams(dimension_semantics=("arbitrary",), vmem_limit_bytes=VMEM_LIMIT,
                                             disable_bounds_checks=True),
        name="moe_combine_ln",
    )(rows, rows, x, gates, yb, g.reshape(1, d), b.reshape(1, d))


def _mixer_layer(x, w_in, b_in, pool_w, pool_b, pool_scale, cmp_pos, cmp_w1, cmp_b1, cmp_w2, cmp_b2,
                 w_up, w_o, ln_g, ln_b, alpha):
    bsz, t, d = x.shape
    n = bsz * t
    mix_w = d // 2
    g, hpg = NSA_KV_GROUPS, NSA_HPG
    nq_w = NSA_HEADS * HEAD_DIM
    nkv_w = N_BRANCH * 2 * g * HEAD_DIM
    nfox_w = 3 * FOX_HEADS * HEAD_DIM
    sizes = (mix_w, nq_w, nkv_w, N_BRANCH * NSA_HEADS, nfox_w, FOX_HEADS, N_BRANCH * d)
    o_pool, o_q, o_kv, o_g, o_fox, o_f, o_gm = (int(v) for v in np.cumsum((0,) + sizes[:-1]))
    n_gate = N_BRANCH * NSA_HEADS

    def cols(a, lo, width):
        return lax.slice_in_dim(a, lo, lo + width, axis=-1)

    qs = SCALE * LOG2E
    hw = FOX_HEADS * HEAD_DIM
    gw = g * HEAD_DIM
    pad_small = 2 * LANES - n_gate - FOX_HEADS
    kv0 = o_kv
    w_f = jnp.concatenate([cols(w_in, o_pool, mix_w), cols(w_in, o_g, n_gate), cols(w_in, o_f, FOX_HEADS),
                           jnp.zeros((d, pad_small), F32)], axis=1).astype(BF16)
    b_f = jnp.concatenate([cols(b_in, o_pool, mix_w), cols(b_in, o_g, n_gate), cols(b_in, o_f, FOX_HEADS),
                           jnp.zeros((pad_small,), F32)])
    tok_cols = ((kv0, 2 * gw), (kv0 + 2 * gw, gw), (kv0 + 4 * gw, gw), (o_fox + hw, hw))
    w_k = jnp.concatenate([cols(w_in, lo, wd) for lo, wd in tok_cols], axis=1).astype(BF16)
    b_k = jnp.concatenate([cols(b_in, lo, wd) for lo, wd in tok_cols])
    feat_cols = ((o_q, nq_w, qs), (kv0 + 3 * gw, gw, 1.0), (kv0 + 5 * gw, gw, 1.0), (o_fox, hw, qs),
                 (o_fox + 2 * hw, hw, 1.0))
    w_t = jnp.concatenate([cols(w_in, lo, wd) * sc for lo, wd, sc in feat_cols], axis=1).T.astype(BF16)
    b_t = jnp.concatenate([cols(b_in, lo, wd) * sc for lo, wd, sc in feat_cols])
    r_vslc, r_vwin, r_qfox, r_vfox = nq_w, nq_w + gw, nq_w + 2 * gw, nq_w + 2 * gw + hw
    c_kslc, c_kwin, c_kfox = 2 * gw, 3 * gw, 4 * gw

    xf = x.reshape(n, d)
    u_pool, z_small = _matmul_bias_split(xf, w_f, b_f, mix_w)
    zk = _matmul_bias(xf, w_k, b_k, BF16)
    gm = _matmul_bias(xf, cols(w_in, o_gm, N_BRANCH * d).astype(BF16), cols(b_in, o_gm, N_BRANCH * d), BF16,
                      act="sigmoid", tn=1024)
    zt = _matmul_bias_t(xf, t, w_t, b_t)

    y_pool = _pool_mixer(u_pool, t, pool_w, pool_b, pool_scale)

    n_chunk = t // CMP_STRIDE
    a_cmp = cols(zk, 0, 2 * gw).reshape(bsz, n_chunk, CMP_STRIDE, 2, g, HEAD_DIM)
    a_cmp = a_cmp.transpose(3, 0, 4, 1, 2, 5).reshape(2, bsz * g, n_chunk, CMP_STRIDE * HEAD_DIM)
    kvc = _compress(a_cmp, cmp_pos.reshape(2, CMP_LEN * HEAD_DIM), cmp_w1, cmp_b1, cmp_w2, cmp_b2)
    kvc = kvc.reshape(2, bsz, g, n_chunk, HEAD_DIM)
    n_s = t // SEL_LEN
    c_start = np.arange(n_chunk) * CMP_STRIDE
    s_start = np.arange(n_s) * SEL_LEN
    ov = (c_start[:, None] < s_start[None, :] + SEL_LEN) & (c_start[:, None] + CMP_LEN > s_start[None, :])
    ov[n_chunk - (CMP_LEN // CMP_STRIDE) + 1:] = False
    ovt = jnp.asarray(ov.T.astype(np.float32), BF16)
    o_cmp, qaug = _nsa_cmp(zt, kvc[0], kvc[1].transpose(0, 1, 3, 2), ovt)
    onehot = jnp.asarray((np.arange(t)[:, None] // SEL_LEN == np.arange(n_s)[None, :]).astype(np.float32), BF16)
    o_slc = _nsa_slc(qaug, onehot, zk, zt, c_kslc // LANES, r_vslc // LANES)
    o_win = _nsa_win(zt, zk, c_kwin // LANES, r_vwin // LANES)

    cum = _fox_prep(z_small, t)
    cq = cols(cum, n_gate, FOX_HEADS)
    cq_t = cq.transpose(0, 3, 1, 2)
    ones_t = jnp.ones((bsz, FOX_HEADS, 3, t), BF16)
    zero6 = jnp.zeros((bsz, FOX_HEADS, 6, t), BF16)
    even = (np.arange(FOX_HEADS) % 2 == 0)[None, :, None, None]
    augq = jnp.concatenate([jnp.where(even, jnp.concatenate([cq_t, ones_t], axis=2), zero6),
                            jnp.where(even, zero6, jnp.concatenate([cq_t, ones_t], axis=2)),
                            jnp.zeros((bsz, FOX_HEADS, 4, t), BF16)], axis=2)
    ck = cq.transpose(0, 3, 2, 1).reshape(bsz, FOX_HEADS // 2, 2, t, 3)
    ones_k = jnp.ones((bsz, FOX_HEADS // 2, t, 3), BF16)
    augk = jnp.concatenate([ones_k, -ck[:, :, 0], ones_k, -ck[:, :, 1],
                            jnp.zeros((bsz, FOX_HEADS // 2, t, LANES - 12), BF16)], axis=-1)
    y_fox = _fox_attention(zt, zk, augq, augk, r_qfox, c_kfox, r_vfox)

    hsel = np.arange(NSA_HEADS * HEAD_DIM) // HEAD_DIM
    expand = np.zeros((LANES, N_BRANCH * mix_w), np.float32)
    for br in range(N_BRANCH):
        expand[hsel * N_BRANCH + br, br * mix_w + np.arange(mix_w)] = 1.0
    return _merge(xf, y_pool, o_cmp, o_slc, o_win, y_fox, z_small, gm,
                  jnp.asarray(expand, BF16), w_up.astype(BF16), w_o.astype(BF16), ln_g, ln_b, alpha)


def _moe_layer(x, router_w, router_b, w1, b1, w2, b2, layer, ln_g, ln_b, alpha):
    n = x.shape[0] // SUBLANES
    nk = n * TOP_K
    idx_p, gate_p = _router(x, router_w, router_b)
    e_flat = idx_p[:, :TOP_K].reshape(nk)
    onehot = (e_flat[:, None] == jnp.arange(N_EXPERTS)[None, :]).astype(jnp.int32)
    incl = jnp.cumsum(onehot, axis=0)
    counts = incl[-1]
    rank = jnp.sum((incl - onehot) * onehot, axis=1)
    padded = (counts + MOE_BLOCK - 1) // MOE_BLOCK * MOE_BLOCK
    pend = jnp.cumsum(padded)
    pstart = pend - padded
    dest = pstart[e_flat] + rank
    n_blk = (nk + MOE_BLOCK - 1) // MOE_BLOCK + N_EXPERTS
    blk_e = jnp.minimum(jnp.sum(jnp.arange(n_blk)[:, None] * MOE_BLOCK >= pend[None, :], axis=1),
                        N_EXPERTS - 1).astype(jnp.int32)
    n_used = (pend[-1] // MOE_BLOCK).astype(jnp.int32).reshape(1)
    dest = dest.astype(jnp.int32)

    xs = _dispatch(x, dest, pend.astype(jnp.int32), n_used, n_blk)
    yb = _expert_ffn(xs, blk_e, n_used, w1, b1, w2, b2, layer)
    return _combine(x, gate_p, dest, yb, ln_g, ln_b, alpha)


def kernel(x, w_in, b_in, pool_w, pool_b, pool_scale, cmp_pos, cmp_w1, cmp_b1, cmp_w2, cmp_b2,
           w_up, w_o, ln1_g, ln1_b, router_w, router_b, moe_w1, moe_b1, moe_w2, moe_b2, ln2_g, ln2_b):
    depth = w_in.shape[0]
    alpha = (2 * depth) ** 0.25
    bsz, t, d = x.shape
    for l in range(depth):
        x1 = _mixer_layer(x, w_in[l], b_in[l], pool_w[l], pool_b[l], pool_scale[l], cmp_pos[l], cmp_w1[l],
                          cmp_b1[l], cmp_w2[l], cmp_b2[l], w_up[l], w_o[l], ln1_g[l], ln1_b[l], alpha)
        x2 = _moe_layer(x1, router_w[l], router_b[l], moe_w1, moe_b1, moe_w2, moe_b2, l,
                        ln2_g[l], ln2_b[l], alpha)
        x = x2.reshape(bsz, t, d)
    return x
```

```python
import functools

import numpy as np
import jax
import jax.numpy as jnp
from jax import lax
from jax.experimental import pallas as pl
from jax.experimental.pallas import tpu as pltpu

F32 = jnp.float32
BF16 = jnp.bfloat16

HEAD_DIM = 64
POOL_WINDOWS = (2, 4, 8, 16)
POOL_GC = 128
POOL_HALO = 16
NSA_HEADS = 8
NSA_KV_GROUPS = 2
NSA_HPG = NSA_HEADS // NSA_KV_GROUPS
N_BRANCH = 3
CMP_LEN = 32
CMP_STRIDE = 16
SEL_LEN = 64
N_SEL = 16
WINDOW = 512
FOX_HEADS = 8
Q_BLOCK = 128
N_EXPERTS = 32
TOP_K = 4
SWIGLU_LIMIT = 7.0
SWIGLU_ALPHA = 1.702
MOE_BLOCK = 256
LN_EPS = 1e-5
NEG_INF = -1e30
SCALE = HEAD_DIM ** -0.5
MASK_BIG = 2.0 ** 100
LOG2E = 1.4426950408889634
ONES_ROWS = 16
LANES = 128
SUBLANES = 8
VMEM_LIMIT = 48 * 1024 * 1024
EXPERT_VMEM_LIMIT = 58 * 1024 * 1024
DMA_ISSUE_GROUP = 8


def _cparams(sem):
    return pltpu.CompilerParams(dimension_semantics=sem, vmem_limit_bytes=VMEM_LIMIT)


def _split3(x):
    hi = x.astype(BF16)
    r1 = x - hi.astype(F32)
    mid = r1.astype(BF16)
    lo = (r1 - mid.astype(F32)).astype(BF16)
    return hi, mid, lo


def _layer_norm(r, g, b):
    mu = jnp.mean(r, axis=-1, keepdims=True)
    var = jnp.mean(jnp.square(r - mu), axis=-1, keepdims=True)
    return (r - mu) * lax.rsqrt(var + LN_EPS) * g + b


def _load_rows_tiled(ref, rows):
    return jnp.concatenate([ref[pl.ds(s, rows, stride=SUBLANES), :] for s in range(SUBLANES)], axis=1)


def _store_rows_tiled(ref, val):
    rows = val.shape[0]
    for s in range(SUBLANES):
        ref[pl.ds(s, rows, stride=SUBLANES), :] = val[:, s * LANES:(s + 1) * LANES]


def _row_tile_copy(src, src_row, dst, dst_row, sem):
    return pltpu.make_async_copy(src.at[pl.ds(pl.multiple_of(src_row * SUBLANES, SUBLANES), SUBLANES), :],
                                 dst.at[pl.ds(pl.multiple_of(dst_row * SUBLANES, SUBLANES), SUBLANES), :], sem)


def _mm_kernel(x_ref, w_ref, b_ref, o_ref, *, act):
    acc = jnp.dot(x_ref[...].astype(BF16), w_ref[...], preferred_element_type=F32) + b_ref[...]
    if act == "sigmoid":
        acc = jax.nn.sigmoid(acc)
    o_ref[...] = acc.astype(o_ref.dtype)


def _matmul_bias(x, w, b, out_dtype, act=None, tm=1024, tn=512):
    m, k = x.shape
    n = w.shape[1]
    tn = min(tn, n)
    assert m % tm == 0 and n % tn == 0
    return pl.pallas_call(
        functools.partial(_mm_kernel, act=act),
        out_shape=jax.ShapeDtypeStruct((m, n), out_dtype),
        grid=(m // tm, n // tn),
        in_specs=[
            pl.BlockSpec((tm, k), lambda i, j: (i, 0)),
            pl.BlockSpec((k, tn), lambda i, j: (0, j)),
            pl.BlockSpec((1, tn), lambda i, j: (0, j)),
        ],
        out_specs=pl.BlockSpec((tm, tn), lambda i, j: (i, j)),
        compiler_params=_cparams(("parallel", "arbitrary")),
        name="in_proj",
    )(x, w, b.reshape(1, n))


def _mm_split_kernel(x_ref, w_ref, b_ref, o0_ref, o1_ref):
    acc = jnp.dot(x_ref[...].astype(BF16), w_ref[...], preferred_element_type=F32) + b_ref[...]
    n0 = o0_ref.shape[1]
    o0_ref[...] = acc[:, 0:n0]
    o1_ref[...] = acc[:, n0:]


def _matmul_bias_split(x, w, b, n0, tm=1024):
    m, k = x.shape
    n = w.shape[1]
    return pl.pallas_call(
        _mm_split_kernel,
        out_shape=(jax.ShapeDtypeStruct((m, n0), F32), jax.ShapeDtypeStruct((m, n - n0), F32)),
        grid=(m // tm,),
        in_specs=[
            pl.BlockSpec((tm, k), lambda i: (i, 0)),
            pl.BlockSpec((k, n), lambda i: (0, 0)),
            pl.BlockSpec((1, n), lambda i: (0, 0)),
        ],
        out_specs=(pl.BlockSpec((tm, n0), lambda i: (i, 0)), pl.BlockSpec((tm, n - n0), lambda i: (i, 0))),
        compiler_params=_cparams(("parallel",)),
        name="in_proj_split",
    )(x, w, b.reshape(1, n))


def _mm_t_kernel(x_ref, wt_ref, b_ref, o_ref):
    acc = lax.dot_general(wt_ref[...], x_ref[...].astype(BF16), (((1,), (1,)), ((), ())),
                          preferred_element_type=F32) + b_ref[...]
    o_ref[...] = acc.astype(o_ref.dtype)


def _matmul_bias_t(x, t, wt, b, tm=1024, tf=896):
    n, k = x.shape
    bsz = n // t
    f = wt.shape[0]
    tiles = t // tm
    assert t % tm == 0 and f % tf == 0
    return pl.pallas_call(
        _mm_t_kernel,
        out_shape=jax.ShapeDtypeStruct((bsz, f, t), BF16),
        grid=(bsz, tiles, f // tf),
        in_specs=[
            pl.BlockSpec((tm, k), lambda bb, i, j: (bb * tiles + i, 0)),
            pl.BlockSpec((tf, k), lambda bb, i, j: (j, 0)),
            pl.BlockSpec((tf, 1), lambda bb, i, j: (j, 0)),
        ],
        out_specs=pl.BlockSpec((None, tf, tm), lambda bb, i, j: (bb, j, i)),
        compiler_params=_cparams(("parallel", "parallel", "arbitrary")),
        name="in_proj_t",
    )(x, wt, b.reshape(f, 1))


def _pool_kernel(prev_ref, cur_ref, w_ref, b_ref, sc_ref, o_ref, ext_ref, *, tile):
    i = pl.program_id(1)
    ext_ref[0:POOL_HALO, :] = jnp.where(i > 0, prev_ref[...], 0.0)
    ext_ref[POOL_HALO:POOL_HALO + tile, :] = cur_ref[...]
    t_idx = i * tile + lax.broadcasted_iota(jnp.int32, (tile, 1), 0)
    for gi, win in enumerate(POOL_WINDOWS):
        cols = slice(gi * POOL_GC, (gi + 1) * POOL_GC)
        u = ext_ref[POOL_HALO:POOL_HALO + tile, cols]
        wsum = u
        for j in range(1, win):
            wsum = wsum + ext_ref[POOL_HALO - j:POOL_HALO - j + tile, cols]
        cnt = jnp.minimum(t_idx + 1, win).astype(F32)
        d = wsum / cnt - u
        y = jnp.dot(d.astype(BF16), w_ref[gi], preferred_element_type=F32) + b_ref[gi]
        o_ref[:, cols] = (y * sc_ref[:, cols]).astype(o_ref.dtype)


def _pool_mixer(u, t, w, b, scale, tile=512):
    n, c = u.shape
    bsz = n // t
    tiles, halos = t // tile, t // POOL_HALO
    return pl.pallas_call(
        functools.partial(_pool_kernel, tile=tile),
        out_shape=jax.ShapeDtypeStruct((n, c), BF16),
        grid=(bsz, tiles),
        in_specs=[
            pl.BlockSpec((POOL_HALO, c),
                         lambda bb, i: (bb * halos + jnp.maximum(i * (tile // POOL_HALO) - 1, 0), 0)),
            pl.BlockSpec((tile, c), lambda bb, i: (bb * tiles + i, 0)),
            pl.BlockSpec((len(POOL_WINDOWS), POOL_GC, POOL_GC), lambda bb, i: (0, 0, 0)),
            pl.BlockSpec((len(POOL_WINDOWS), 1, POOL_GC), lambda bb, i: (0, 0, 0)),
            pl.BlockSpec((1, c), lambda bb, i: (0, 0)),
        ],
        out_specs=pl.BlockSpec((tile, c), lambda bb, i: (bb * tiles + i, 0)),
        scratch_shapes=[pltpu.VMEM((POOL_HALO + tile, c), F32)],
        compiler_params=_cparams(("parallel", "arbitrary")),
        name="pool_mixer",
    )(u, u, w.astype(BF16), b.reshape(len(POOL_WINDOWS), 1, POOL_GC), scale.reshape(1, c))


def _compress_kernel(a_ref, pos_ref, w1_ref, b1_ref, w2_ref, b2_ref, o_ref):
    half = CMP_STRIDE * HEAD_DIM
    a = a_ref[...].astype(F32)
    n_chunk = a.shape[0]
    top = (a + pos_ref[:, 0:half]).astype(BF16)
    bot = (a + pos_ref[:, half:2 * half]).astype(BF16)
    p1 = jnp.dot(top, w1_ref[0:half, :], preferred_element_type=F32)
    p2 = jnp.dot(bot, w1_ref[half:2 * half, :], preferred_element_type=F32)
    h = p1 + pltpu.roll(p2, n_chunk - 1, axis=0) + b1_ref[...]
    h = jax.nn.gelu(h)
    o = jnp.dot(h.astype(BF16), w2_ref[...], preferred_element_type=F32) + b2_ref[...]
    o_ref[...] = o.astype(o_ref.dtype)


def _compress(a, pos, w1, b1, w2, b2):
    _, bg, n_chunk, width = a.shape
    hid = w1.shape[-1]
    return pl.pallas_call(
        _compress_kernel,
        out_shape=jax.ShapeDtypeStruct((2, bg, n_chunk, HEAD_DIM), BF16),
        grid=(2, bg),
        in_specs=[
            pl.BlockSpec((None, None, n_chunk, width), lambda s, j: (s, j, 0, 0)),
            pl.BlockSpec((None, 1, 2 * width), lambda s, j: (s, 0, 0)),
            pl.BlockSpec((None, 2 * width, hid), lambda s, j: (s, 0, 0)),
            pl.BlockSpec((None, 1, hid), lambda s, j: (s, 0, 0)),
            pl.BlockSpec((None, hid, HEAD_DIM), lambda s, j: (s, 0, 0)),
            pl.BlockSpec((None, 1, HEAD_DIM), lambda s, j: (s, 0, 0)),
        ],
        out_specs=pl.BlockSpec((None, None, n_chunk, HEAD_DIM), lambda s, j: (s, j, 0, 0)),
        compiler_params=_cparams(("arbitrary", "arbitrary")),
        name="nsa_compress",
    )(a, pos.reshape(2, 1, 2 * width), w1.astype(BF16), b1.reshape(2, 1, hid),
      w2.astype(BF16), b2.reshape(2, 1, HEAD_DIM))


def _flash_steps_t(qts, ks, vts, carries, mask):
    scores = [jnp.dot(k, qt, preferred_element_type=F32) for k, qt in zip(ks, qts)]
    if mask is not None:
        masks = mask if isinstance(mask, (list, tuple)) else [mask] * len(scores)
        scores = [jnp.where(mk, s, NEG_INF) for mk, s in zip(masks, scores)]
    probs, stats = [], []
    for s, (m, _) in zip(scores, carries):
        m_new = jnp.maximum(m, jnp.max(s, axis=0, keepdims=True))
        probs.append(jnp.exp2(s - m_new).astype(BF16))
        stats.append((m_new, jnp.exp2(m - m_new)))
    return tuple((m_new, alpha * acc + jnp.dot(vt, p, preferred_element_type=F32))
                 for vt, p, (m_new, alpha), (_, acc) in zip(vts, probs, stats, carries))


def _flash_init_t(cols):
    return (jnp.full((1, cols), NEG_INF, F32), jnp.zeros((HEAD_DIM + ONES_ROWS, cols), F32))


def _flash_finish_t(acc):
    return acc[0:HEAD_DIM] / acc[HEAD_DIM:HEAD_DIM + 1]


def _with_ones(vt):
    return jnp.concatenate([vt, jnp.ones((ONES_ROWS, vt.shape[1]), vt.dtype)], axis=0)


def _group_queries(qt_ref, g):
    return jnp.concatenate([qt_ref[(g * NSA_HPG + h) * HEAD_DIM:(g * NSA_HPG + h + 1) * HEAD_DIM, :]
                            for h in range(NSA_HPG)], axis=1)


def _slot_rows(x, slot, n_slots):
    z = jnp.zeros_like(x)
    return jnp.concatenate([x if j == slot else z for j in range(n_slots)], axis=0)


def _store_group(o_ref, g, o, col0=0):
    for h in range(NSA_HPG):
        r0 = (g * NSA_HPG + h) * HEAD_DIM
        o_ref[r0:r0 + HEAD_DIM, col0:col0 + Q_BLOCK] = o[:, h * Q_BLOCK:(h + 1) * Q_BLOCK].astype(o_ref.dtype)


def _nsa_cmp_kernel(qt_ref, kc_ref, vct_ref, ovt_ref, ocmp_ref, qaug_ref):
    groups = range(kc_ref.shape[0])
    cols = NSA_HPG * Q_BLOCK
    t0 = pl.program_id(1) * Q_BLOCK
    n_c = kc_ref.shape[1]
    n_s = ovt_ref.shape[0]
    qts = [_group_queries(qt_ref, g) for g in groups]
    tq = t0 + (lax.broadcasted_iota(jnp.int32, (n_c, cols), 1) & (Q_BLOCK - 1))
    c_end = lax.broadcasted_iota(jnp.int32, (n_c, cols), 0) * CMP_STRIDE + (CMP_LEN - 1)
    mask = c_end <= tq
    ps = []
    for g in groups:
        sm = jnp.where(mask, jnp.dot(kc_ref[g], qts[g], preferred_element_type=F32), NEG_INF)
        e = jnp.where(mask, jnp.exp2(sm - jnp.max(sm, axis=0, keepdims=True)), 0.0)
        l = jnp.sum(e, axis=0, keepdims=True)
        ps.append(e / jnp.where(l > 0.0, l, 1.0))
    for g in groups:
        _store_group(ocmp_ref, g, jnp.dot(vct_ref[g], ps[g].astype(BF16), preferred_element_type=F32))

    ovt = ovt_ref[...]
    blk = lax.broadcasted_iota(jnp.int32, (n_s, Q_BLOCK), 0)
    tqq = t0 + lax.broadcasted_iota(jnp.int32, (n_s, Q_BLOCK), 1)
    future = blk * SEL_LEN > tqq
    cur = lax.shift_right_logical(tqq, 6)
    forced = (blk == 0) | (blk == cur) | (blk == cur - 1)
    scores = []
    for g in groups:
        psum = ps[g][:, 0:Q_BLOCK]
        for h in range(1, NSA_HPG):
            psum = psum + ps[g][:, h * Q_BLOCK:(h + 1) * Q_BLOCK]
        imp = None
        for part in _split3(psum):
            term = jnp.dot(ovt, part, preferred_element_type=F32)
            imp = term if imp is None else imp + term
        scores.append(jnp.where(future, -1.0, jnp.where(forced, 1e6, imp)))
    sels = [jnp.zeros((n_s, Q_BLOCK), F32) for _ in groups]
    for _ in range(min(N_SEL, n_s)):
        for g in groups:
            mx = jnp.max(scores[g], axis=0, keepdims=True)
            first = jnp.min(jnp.where(scores[g] == mx, blk, n_s), axis=0, keepdims=True)
            pick = blk == first
            sels[g] = jnp.where(pick, 1.0, sels[g])
            scores[g] = jnp.where(pick, -2.0, scores[g])
    for g in groups:
        sel = jnp.where(future, 0.0, sels[g])
        bias = ((sel - 1.0) * MASK_BIG).astype(BF16)
        for h in range(NSA_HPG):
            qaug_ref[g, 0:n_s, h * Q_BLOCK:(h + 1) * Q_BLOCK] = bias
        qaug_ref[g, n_s:n_s + len(groups) * HEAD_DIM, :] = _slot_rows(qts[g], g, len(groups))


def _nsa_cmp(zt, kc, vct, ovt):
    bsz, _, t = zt.shape
    g, n_c = kc.shape[1], kc.shape[2]
    n_s = ovt.shape[0]
    nq = t // Q_BLOCK
    rows_q = NSA_HEADS * HEAD_DIM
    cols = NSA_HPG * Q_BLOCK
    return pl.pallas_call(
        _nsa_cmp_kernel,
        out_shape=(jax.ShapeDtypeStruct((bsz, rows_q, t), BF16),
                   jax.ShapeDtypeStruct((bsz, g, nq, n_s + g * HEAD_DIM, cols), BF16)),
        grid=(bsz, nq),
        in_specs=[
            pl.BlockSpec((None, rows_q, Q_BLOCK), lambda b, i: (b, 0, i)),
            pl.BlockSpec((None, g, n_c, HEAD_DIM), lambda b, i: (b, 0, 0, 0)),
            pl.BlockSpec((None, g, HEAD_DIM, n_c), lambda b, i: (b, 0, 0, 0)),
            pl.BlockSpec((n_s, n_c), lambda b, i: (0, 0)),
        ],
        out_specs=(
            pl.BlockSpec((None, rows_q, Q_BLOCK), lambda b, i: (b, 0, i)),
            pl.BlockSpec((None, g, None, n_s + g * HEAD_DIM, cols), lambda b, i: (b, 0, i, 0, 0)),
        ),
        compiler_params=_cparams(("parallel", "arbitrary")),
        name="nsa_cmp_select",
    )(zt, kc, vct, ovt)


def _nsa_slc_kernel(qaug_ref, onehot_ref, k_ref, vt_ref, o_ref, *, tk, qb):
    groups = qaug_ref.shape[0]
    cols = NSA_HPG * Q_BLOCK
    t0 = pl.program_id(1) * (qb * Q_BLOCK)
    jd = t0 // tk
    probs = [(g, b) for g in range(groups) for b in range(qb)]

    def chunk(j, carries, mask):
        start = pl.multiple_of(j * tk, tk)
        k = jnp.concatenate([onehot_ref[pl.ds(start, tk), :], k_ref[pl.ds(start, tk), :]], axis=1)
        vts = [_with_ones(vt_ref[g * HEAD_DIM:(g + 1) * HEAD_DIM, pl.ds(start, tk)]) for g in range(groups)]
        return _flash_steps_t([qaug_ref[g, b] for g, b in probs], [k for _ in probs],
                              [vts[g] for g, _ in probs], carries, mask)

    carries = lax.fori_loop(0, jd, lambda j, c: chunk(j, c, None), tuple(_flash_init_t(cols) for _ in probs))
    kpos = jd * tk + lax.broadcasted_iota(jnp.int32, (tk, cols), 0)
    tq = t0 + (lax.broadcasted_iota(jnp.int32, (tk, cols), 1) & (Q_BLOCK - 1))
    causal = [kpos <= tq + b * Q_BLOCK for b in range(qb)]
    carries = chunk(jd, carries, [causal[b] for _, b in probs])
    for (g, b), (_, acc) in zip(probs, carries):
        _store_group(o_ref, g, _flash_finish_t(acc), b * Q_BLOCK)


def _nsa_slc(qaug, onehot, zk, zt, k_blk, vt_blk, tk=512, qb=4):
    bsz, g, nq, wa, cols = qaug.shape
    t = onehot.shape[0]
    tk = min(tk, t)
    assert tk % (qb * Q_BLOCK) == 0 and nq % qb == 0 and g * HEAD_DIM == LANES
    return pl.pallas_call(
        functools.partial(_nsa_slc_kernel, tk=tk, qb=qb),
        out_shape=jax.ShapeDtypeStruct((bsz, NSA_HEADS * HEAD_DIM, t), BF16),
        grid=(bsz, nq // qb),
        in_specs=[
            pl.BlockSpec((None, g, qb, wa, cols), lambda b, i: (b, 0, i, 0, 0)),
            pl.BlockSpec(onehot.shape, lambda b, i: (0, 0)),
            pl.BlockSpec((t, LANES), lambda b, i: (b, k_blk)),
            pl.BlockSpec((None, LANES, t), lambda b, i: (b, vt_blk, 0)),
        ],
        out_specs=pl.BlockSpec((None, NSA_HEADS * HEAD_DIM, qb * Q_BLOCK), lambda b, i: (b, 0, i)),
        compiler_params=_cparams(("parallel", "arbitrary")),
        name="nsa_selected",
    )(qaug, onehot, zk, zt)


def _nsa_win_kernel(qt_ref, k_ref, vt_ref, o_ref):
    groups = NSA_KV_GROUPS
    cols = NSA_HPG * Q_BLOCK
    span = WINDOW + Q_BLOCK
    t0 = pl.program_id(1) * Q_BLOCK
    start = pl.multiple_of(jnp.maximum(t0 - WINDOW, 0), Q_BLOCK)
    dist = (t0 - start) + (lax.broadcasted_iota(jnp.int32, (span, cols), 1) & (Q_BLOCK - 1)) \
        - lax.broadcasted_iota(jnp.int32, (span, cols), 0)
    mask = (dist >= 0) & (dist < WINDOW)
    k = k_ref[pl.ds(start, span), :]
    scores = [jnp.where(mask, jnp.dot(k, _slot_rows(_group_queries(qt_ref, g), g, groups),
                                      preferred_element_type=F32), NEG_INF) for g in range(groups)]
    probs = [jnp.exp2(s - jnp.max(s, axis=0, keepdims=True)).astype(BF16) for s in scores]
    for g in range(groups):
        vt = _with_ones(vt_ref[g * HEAD_DIM:(g + 1) * HEAD_DIM, pl.ds(start, span)])
        _store_group(o_ref, g, _flash_finish_t(jnp.dot(vt, probs[g], preferred_element_type=F32)))


def _nsa_win(zt, zk, k_blk, vt_blk):
    bsz, _, t = zt.shape
    rows_q = NSA_HEADS * HEAD_DIM
    assert t >= WINDOW + Q_BLOCK and NSA_KV_GROUPS * HEAD_DIM == LANES
    return pl.pallas_call(
        _nsa_win_kernel,
        out_shape=jax.ShapeDtypeStruct((bsz, rows_q, t), BF16),
        grid=(bsz, t // Q_BLOCK),
        in_specs=[
            pl.BlockSpec((None, rows_q, Q_BLOCK), lambda b, i: (b, 0, i)),
            pl.BlockSpec((t, LANES), lambda b, i: (b, k_blk)),
            pl.BlockSpec((None, LANES, t), lambda b, i: (b, vt_blk, 0)),
        ],
        out_specs=pl.BlockSpec((None, rows_q, Q_BLOCK), lambda b, i: (b, 0, i)),
        compiler_params=_cparams(("parallel", "arbitrary")),
        name="nsa_window",
    )(zt, zk, zt)


def _fox_prep_kernel(z_ref, tri_ref, o_ref, carry_ref):
    @pl.when(pl.program_id(1) == 0)
    def _():
        carry_ref[...] = jnp.zeros_like(carry_ref)

    lf = jax.nn.log_sigmoid(z_ref[...])
    hi, mid, lo = _split3(lf)
    tri = tri_ref[...]
    cum = (jnp.dot(tri, hi, preferred_element_type=F32) + jnp.dot(tri, mid, preferred_element_type=F32)
           + jnp.dot(tri, lo, preferred_element_type=F32)) + carry_ref[...]
    carry_ref[...] = cum[cum.shape[0] - 1:cum.shape[0], :]
    for part, val in enumerate(_split3(cum * LOG2E)):
        o_ref[part] = val


def _fox_prep(z_small, t, tile=256):
    bsz = z_small.shape[0] // t
    tiles = t // tile
    tri = jnp.asarray(np.tril(np.ones((tile, tile), np.float32)), BF16)
    return pl.pallas_call(
        _fox_prep_kernel,
        out_shape=jax.ShapeDtypeStruct((bsz, 3, t, LANES), BF16),
        grid=(bsz, tiles),
        in_specs=[
            pl.BlockSpec((tile, LANES), lambda b, i: (b * tiles + i, 0)),
            pl.BlockSpec((tile, tile), lambda b, i: (0, 0)),
        ],
        out_specs=pl.BlockSpec((None, 3, tile, LANES), lambda b, i: (b, 0, i, 0)),
        scratch_shapes=[pltpu.VMEM((1, LANES), F32)],
        compiler_params=_cparams(("parallel", "arbitrary")),
        name="fox_decay_cumsum",
    )(z_small, tri)


def _fox_kernel(qt_ref, augq_ref, k_ref, augk_ref, vt_ref, o_ref, *, tq, heads):
    i = pl.program_id(2)
    pair_w = LANES // HEAD_DIM
    qts = []
    for h in range(heads):
        q = _slot_rows(qt_ref[h * HEAD_DIM:(h + 1) * HEAD_DIM, :], h % pair_w, pair_w)
        aug = augq_ref[h]
        qts.append(jnp.concatenate([q, aug, jnp.zeros((LANES - aug.shape[0], tq), aug.dtype)], axis=0))

    def chunk(j, carries, mask):
        start = pl.multiple_of(j * tq, tq)
        ks = [jnp.concatenate([k_ref[pl.ds(start, tq), p * LANES:(p + 1) * LANES], augk_ref[p, pl.ds(start, tq), :]],
                              axis=1) for p in range(heads // pair_w)]
        return _flash_steps_t(qts, [ks[h // pair_w] for h in range(heads)],
                              [_with_ones(vt_ref[h * HEAD_DIM:(h + 1) * HEAD_DIM, pl.ds(start, tq)])
                               for h in range(heads)], carries, mask)

    carries = lax.fori_loop(0, i, lambda j, c: chunk(j, c, None), tuple(_flash_init_t(tq) for _ in range(heads)))
    causal = lax.broadcasted_iota(jnp.int32, (tq, tq), 0) <= lax.broadcasted_iota(jnp.int32, (tq, tq), 1)
    carries = chunk(i, carries, causal)
    for h in range(heads):
        o_ref[h * HEAD_DIM:(h + 1) * HEAD_DIM, :] = _flash_finish_t(carries[h][1]).astype(o_ref.dtype)


def _fox_attention(zt, zk, augq, augk, q_row0, k_col0, v_row0, tq=512, heads=4):
    bsz, _, t = zt.shape
    tq = min(tq, t)
    rows = heads * HEAD_DIM
    pairs = heads * HEAD_DIM // LANES
    assert q_row0 % rows == 0 and v_row0 % rows == 0 and k_col0 % rows == 0
    return pl.pallas_call(
        functools.partial(_fox_kernel, tq=tq, heads=heads),
        out_shape=jax.ShapeDtypeStruct((bsz, FOX_HEADS * HEAD_DIM, t), BF16),
        grid=(bsz, FOX_HEADS // heads, t // tq),
        in_specs=[
            pl.BlockSpec((None, rows, tq), lambda b, hh, i: (b, q_row0 // rows + hh, i)),
            pl.BlockSpec((None, heads, augq.shape[2], tq), lambda b, hh, i: (b, hh, 0, i)),
            pl.BlockSpec((t, rows), lambda b, hh, i: (b, k_col0 // rows + hh)),
            pl.BlockSpec((None, pairs, t, LANES), lambda b, hh, i: (b, hh, 0, 0)),
            pl.BlockSpec((None, rows, t), lambda b, hh, i: (b, v_row0 // rows + hh, 0)),
        ],
        out_specs=pl.BlockSpec((None, rows, tq), lambda b, hh, i: (b, hh, i)),
        compiler_params=_cparams(("parallel", "parallel", "arbitrary")),
        name="fox_attention",
    )(zt, augq, zk, augk, zt)


def _merge_kernel(x_ref, yp_ref, oc_ref, os_ref, ow_ref, yf_ref, zs_ref, gm_ref, ex_ref,
                  wup_ref, wo_ref, g_ref, b_ref, o_ref, *, alpha):
    mix_w = yp_ref.shape[-1]
    d = x_ref.shape[-1]
    sg = jax.nn.sigmoid(zs_ref[:, 0:LANES])
    hi, mid, lo = _split3(sg)
    ex = ex_ref[...]
    ge = (jnp.dot(hi, ex, preferred_element_type=F32) + jnp.dot(mid, ex, preferred_element_type=F32)
          + jnp.dot(lo, ex, preferred_element_type=F32))
    yn = (ge[:, 0:mix_w] * oc_ref[...].astype(F32).T + ge[:, mix_w:2 * mix_w] * os_ref[...].astype(F32).T
          + ge[:, 2 * mix_w:3 * mix_w] * ow_ref[...].astype(F32).T)
    ys = (yp_ref[...], yn.astype(BF16), yf_ref[...].astype(F32).T.astype(BF16))
    mix = None
    for n in range(N_BRANCH):
        up = jnp.dot(ys[n], wup_ref[n], preferred_element_type=F32)
        term = gm_ref[:, n * d:(n + 1) * d].astype(F32) * up
        mix = term if mix is None else mix + term
    h = jnp.dot(mix.astype(BF16), wo_ref[...], preferred_element_type=F32)
    _store_rows_tiled(o_ref, _layer_norm(alpha * x_ref[...] + h, g_ref[...], b_ref[...]))


def _merge(x, yp, oc, osl, ow, yf, zs, gm, expand, wup, wo, g, b, alpha, tm=512):
    n, d = x.shape
    mix_w = yp.shape[1]
    tiles_per_seq = oc.shape[2] // tm
    tok = lambda width: pl.BlockSpec((tm, width), lambda i: (i, 0))
    feat = pl.BlockSpec((None, mix_w, tm), lambda i: (i // tiles_per_seq, 0, i % tiles_per_seq))
    return pl.pallas_call(
        functools.partial(_merge_kernel, alpha=alpha),
        out_shape=jax.ShapeDtypeStruct((n * SUBLANES, d // SUBLANES), F32),
        grid=(n // tm,),
        in_specs=[
            tok(d), tok(mix_w), feat, feat, feat, feat,
            pl.BlockSpec((tm, 2 * LANES), lambda i: (i, 0)),
            tok(N_BRANCH * d),
            pl.BlockSpec((LANES, N_BRANCH * mix_w), lambda i: (0, 0)),
            pl.BlockSpec((N_BRANCH, mix_w, d), lambda i: (0, 0, 0)),
            pl.BlockSpec((d, d), lambda i: (0, 0)),
            pl.BlockSpec((1, d), lambda i: (0, 0)),
            pl.BlockSpec((1, d), lambda i: (0, 0)),
        ],
        out_specs=pl.BlockSpec((tm * SUBLANES, d // SUBLANES), lambda i: (i, 0)),
        compiler_params=_cparams(("parallel",)),
        name="merge_outproj_ln",
    )(x, yp, oc, osl, ow, yf, zs, gm, expand, wup, wo, g.reshape(1, d), b.reshape(1, d))


def _router_kernel(x_ref, w_ref, b_ref, idx_ref, gate_ref):
    x = _load_rows_tiled(x_ref, idx_ref.shape[0])
    logits = jnp.dot(x, w_ref[...], preferred_element_type=F32,
                     precision=lax.Precision.HIGHEST) + b_ref[...]
    tm, lanes = logits.shape
    lane = lax.broadcasted_iota(jnp.int32, (tm, lanes), 1)
    work = jnp.where(lane < N_EXPERTS, logits, -jnp.inf)
    idx_out = jnp.zeros((tm, lanes), jnp.int32)
    val_out = jnp.zeros((tm, lanes), F32)
    top = None
    denom = jnp.zeros((tm, 1), F32)
    for k in range(TOP_K):
        mx = jnp.max(work, axis=-1, keepdims=True)
        first = jnp.min(jnp.where(work == mx, lane, lanes), axis=-1, keepdims=True)
        if top is None:
            top = mx
        e = jnp.exp(mx - top)
        denom = denom + e
        idx_out = jnp.where(lane == k, first, idx_out)
        val_out = jnp.where(lane == k, e, val_out)
        work = jnp.where(lane == first, -jnp.inf, work)
    idx_ref[...] = idx_out
    gate_ref[...] = val_out / denom


def _router(x, w, b, tm=512):
    n, d = x.shape[0] // SUBLANES, x.shape[1] * SUBLANES
    wp = jnp.zeros((d, LANES), F32).at[:, :N_EXPERTS].set(w)
    bp = jnp.zeros((1, LANES), F32).at[0, :N_EXPERTS].set(b)
    return pl.pallas_call(
        _router_kernel,
        out_shape=(jax.ShapeDtypeStruct((n, LANES), jnp.int32), jax.ShapeDtypeStruct((n, LANES), F32)),
        grid=(n // tm,),
        in_specs=[
            pl.BlockSpec((tm * SUBLANES, d // SUBLANES), lambda i: (i, 0)),
            pl.BlockSpec((d, LANES), lambda i: (0, 0)),
            pl.BlockSpec((1, LANES), lambda i: (0, 0)),
        ],
        out_specs=(pl.BlockSpec((tm, LANES), lambda i: (i, 0)), pl.BlockSpec((tm, LANES), lambda i: (i, 0))),
        compiler_params=_cparams(("parallel",)),
        name="moe_router",
    )(x, wp, bp)


def _start_row_tiles(src_hbm, idx_ref, n, dst, sem, *, per_row=1):
    def body(gi, c):
        base = gi * DMA_ISSUE_GROUP
        idx = [idx_ref[0, (base + j) * per_row + k] for j in range(DMA_ISSUE_GROUP) for k in range(per_row)]
        for j in range(DMA_ISSUE_GROUP):
            for k in range(per_row):
                _row_tile_copy(src_hbm, idx[j * per_row + k], dst.at[k] if per_row > 1 else dst, base + j,
                               sem).start(priority=k % 2)
        return c
    lax.fori_loop(0, n // DMA_ISSUE_GROUP, body, 0)


def _wait_buffer(buf, sem):
    pltpu.make_async_copy(buf, buf, sem).wait()


def _dispatch_kernel(pend_ref, nu_ref, dest_ref, x_ref, xs_hbm, zeros, sems, *, tm, n_blk):
    i = pl.program_id(0)
    blk_rows = MOE_BLOCK * SUBLANES

    def zero_block(blk):
        return pltpu.make_async_copy(zeros, xs_hbm.at[pl.ds(pl.multiple_of(blk * blk_rows, blk_rows), blk_rows), :],
                                     sems.at[1])

    def last_block(e):
        prev = jnp.where(e > 0, pend_ref[jnp.maximum(e - 1, 0)], 0)
        return pend_ref[e] > prev, pend_ref[e] // MOE_BLOCK - 1

    @pl.when(i == 0)
    def _():
        zeros[...] = jnp.zeros_like(zeros)
        for wait in (False, True):
            def per_expert(e, c):
                used, blk = last_block(e)

                @pl.when(used)
                def _():
                    zero_block(blk).wait() if wait else zero_block(blk).start()
                return c

            def per_tail(blk, c):
                zero_block(blk).wait() if wait else zero_block(blk).start()
                return c
            lax.fori_loop(0, N_EXPERTS, per_expert, 0)
            lax.fori_loop(nu_ref[0], n_blk, per_tail, 0)

    def body(gi, c):
        base = gi * DMA_ISSUE_GROUP
        idx = [dest_ref[0, (base + j) * TOP_K + k] for j in range(DMA_ISSUE_GROUP) for k in range(TOP_K)]
        for j in range(DMA_ISSUE_GROUP):
            for k in range(TOP_K):
                _row_tile_copy(x_ref, base + j, xs_hbm, idx[j * TOP_K + k], sems.at[0]).start(priority=k % 2)
        return c
    lax.fori_loop(0, tm // DMA_ISSUE_GROUP, body, 0)
    for _ in range(TOP_K):
        _wait_buffer(x_ref, sems.at[0])


def _dispatch(x, dest, pend, n_used, n_blk, tm=256):
    dl = x.shape[1]
    n = x.shape[0] // SUBLANES
    n_tiles = n // tm
    grid_spec = pltpu.PrefetchScalarGridSpec(
        num_scalar_prefetch=2,
        grid=(n_tiles,),
        in_specs=[
            pl.BlockSpec((None, 1, tm * TOP_K), lambda i, pe, nu: (i, 0, 0), memory_space=pltpu.SMEM),
            pl.BlockSpec((tm * SUBLANES, dl), lambda i, pe, nu: (i, 0)),
        ],
        out_specs=pl.BlockSpec(memory_space=pl.ANY),
        scratch_shapes=[pltpu.VMEM((MOE_BLOCK * SUBLANES, dl), F32), pltpu.SemaphoreType.DMA((2,))],
    )
    return pl.pallas_call(
        functools.partial(_dispatch_kernel, tm=tm, n_blk=n_blk),
        out_shape=jax.ShapeDtypeStruct((n_blk * MOE_BLOCK * SUBLANES, dl), F32),
        grid_spec=grid_spec,
        compiler_params=pltpu.CompilerParams(dimension_semantics=("arbitrary",), vmem_limit_bytes=VMEM_LIMIT,
                                             disable_bounds_checks=True),
        name="moe_dispatch",
    )(pend, n_used, dest.reshape(n_tiles, 1, tm * TOP_K), x)


def _expert_kernel(be_ref, nu_ref, x_ref, w1_ref, b1_ref, w2_ref, b2_ref, o_ref, w1_bf, w2_bf):
    i = pl.program_id(0)
    n_used = nu_ref[0]

    @pl.when((i < n_used) & ((i == 0) | (be_ref[i] != be_ref[jnp.maximum(i - 1, 0)])))
    def _():
        w1_bf[...] = w1_ref[...].astype(BF16)
        w2_bf[...] = w2_ref[...].astype(BF16)

    @pl.when(i < n_used)
    def _():
        dff = w2_ref.shape[0]
        x = _load_rows_tiled(x_ref, MOE_BLOCK).astype(BF16)
        h = jnp.dot(x, w1_bf[...], preferred_element_type=F32) + b1_ref[...]
        gate = jnp.minimum(h[:, 0:dff], SWIGLU_LIMIT)
        upv = jnp.clip(h[:, dff:2 * dff], -SWIGLU_LIMIT, SWIGLU_LIMIT)
        act = (upv + 1.0) * (gate * jax.nn.sigmoid(SWIGLU_ALPHA * gate))
        _store_rows_tiled(o_ref, jnp.dot(act.astype(BF16), w2_bf[...], preferred_element_type=F32) + b2_ref[...])

    @pl.when(i >= n_used)
    def _():
        o_ref[...] = jnp.zeros_like(o_ref)


def _expert_ffn(xs, blk_e, n_used, w1, b1, w2, b2, layer):
    dl = xs.shape[1]
    n_blk = xs.shape[0] // (MOE_BLOCK * SUBLANES)
    d = dl * SUBLANES
    dff = w2.shape[2]
    depth = w1.shape[0]
    grid_spec = pltpu.PrefetchScalarGridSpec(
        num_scalar_prefetch=2,
        grid=(n_blk,),
        in_specs=[
            pl.BlockSpec((MOE_BLOCK * SUBLANES, dl), lambda i, be, nu: (jnp.minimum(i, nu[0] - 1), 0)),
            pl.BlockSpec((None, None, d, 2 * dff), lambda i, be, nu: (layer, be[i], 0, 0)),
            pl.BlockSpec((None, None, 1, 2 * dff), lambda i, be, nu: (layer, be[i], 0, 0)),
            pl.BlockSpec((None, None, dff, d), lambda i, be, nu: (layer, be[i], 0, 0)),
            pl.BlockSpec((None, None, 1, d), lambda i, be, nu: (layer, be[i], 0, 0)),
        ],
        out_specs=pl.BlockSpec((MOE_BLOCK * SUBLANES, dl), lambda i, be, nu: (i, 0)),
        scratch_shapes=[pltpu.VMEM((d, 2 * dff), BF16), pltpu.VMEM((dff, d), BF16)],
    )
    return pl.pallas_call(
        _expert_kernel,
        out_shape=jax.ShapeDtypeStruct(xs.shape, F32),
        grid_spec=grid_spec,
        compiler_params=pltpu.CompilerParams(dimension_semantics=("arbitrary",), vmem_limit_bytes=EXPERT_VMEM_LIMIT),
        name="moe_experts",
    )(blk_e, n_used, xs, w1, b1.reshape(depth, N_EXPERTS, 1, 2 * dff), w2, b2.reshape(depth, N_EXPERTS, 1, d))


def _combine_kernel(row_cur_ref, row_next_ref, x_ref, gate_ref, yb_hbm, g_ref, b_ref, o_ref, buf, sems,
                    *, tm, alpha):
    i = pl.program_id(0)
    slot = lax.rem(i, 2)

    @pl.when(i == 0)
    def _():
        _start_row_tiles(yb_hbm, row_cur_ref, tm, buf.at[0], sems.at[0], per_row=TOP_K)

    @pl.when(i + 1 < pl.num_programs(0))
    def _():
        _start_row_tiles(yb_hbm, row_next_ref, tm, buf.at[1 - slot], sems.at[1 - slot], per_row=TOP_K)

    _wait_buffer(buf.at[slot], sems.at[slot])
    gates = gate_ref[...]
    y = gates[:, 0:1] * _load_rows_tiled(buf.at[slot, 0], tm)
    for k in range(1, TOP_K):
        y = y + gates[:, k:k + 1] * _load_rows_tiled(buf.at[slot, k], tm)
    o_ref[...] = _layer_norm(alpha * _load_rows_tiled(x_ref, tm) + y, g_ref[...], b_ref[...])


def _combine(x, gates, row_of, yb, g, b, alpha, tm=256):
    dl = x.shape[1]
    n, d = x.shape[0] // SUBLANES, dl * SUBLANES
    n_tiles = n // tm
    rows = row_of.reshape(n_tiles, 1, tm * TOP_K)
    return pl.pallas_call(
        functools.partial(_combine_kernel, tm=tm, alpha=alpha),
        out_shape=jax.ShapeDtypeStruct((n, d), F32),
        grid=(n_tiles,),
        in_specs=[
            pl.BlockSpec((None, 1, tm * TOP_K), lambda i: (i, 0, 0), memory_space=pltpu.SMEM),
            pl.BlockSpec((None, 1, tm * TOP_K), lambda i: (jnp.minimum(i + 1, n_tiles - 1), 0, 0),
                         memory_space=pltpu.SMEM),
            pl.BlockSpec((tm * SUBLANES, dl), lambda i: (i, 0)),
            pl.BlockSpec((tm, LANES), lambda i: (i, 0)),
            pl.BlockSpec(memory_space=pl.ANY),
            pl.BlockSpec((1, d), lambda i: (0, 0)),
            pl.BlockSpec((1, d), lambda i: (0, 0)),
        ],
        out_specs=pl.BlockSpec((tm, d), lambda i: (i, 0)),
        scratch_shapes=[pltpu.VMEM((2, TOP_K, tm * SUBLANES, dl), F32), pltpu.SemaphoreType.DMA((2,))],
        compiler_params=pltpu.CompilerParams(dimension_semantics=("arbitrary",), vmem_limit_bytes=VMEM_LIMIT,
                                             disable_bounds_checks=True),
        name="moe_combine_ln",
    )(rows, rows, x, gates, yb, g.reshape(1, d), b.reshape(1, d))


def _mixer_layer(x, w_in, b_in, pool_w, pool_b, pool_scale, cmp_pos, cmp_w1, cmp_b1, cmp_w2, cmp_b2,
                 w_up, w_o, ln_g, ln_b, alpha):
    bsz, t, d = x.shape
    n = bsz * t
    mix_w = d // 2
    g, hpg = NSA_KV_GROUPS, NSA_HPG
    nq_w = NSA_HEADS * HEAD_DIM
    nkv_w = N_BRANCH * 2 * g * HEAD_DIM
    nfox_w = 3 * FOX_HEADS * HEAD_DIM
    sizes = (mix_w, nq_w, nkv_w, N_BRANCH * NSA_HEADS, nfox_w, FOX_HEADS, N_BRANCH * d)
    o_pool, o_q, o_kv, o_g, o_fox, o_f, o_gm = (int(v) for v in np.cumsum((0,) + sizes[:-1]))
    n_gate = N_BRANCH * NSA_HEADS

    def cols(a, lo, width):
        return lax.slice_in_dim(a, lo, lo + width, axis=-1)

    qs = SCALE * LOG2E
    hw = FOX_HEADS * HEAD_DIM
    gw = g * HEAD_DIM
    pad_small = 2 * LANES - n_gate - FOX_HEADS
    kv0 = o_kv
    w_f = jnp.concatenate([cols(w_in, o_pool, mix_w), cols(w_in, o_g, n_gate), cols(w_in, o_f, FOX_HEADS),
                           jnp.zeros((d, pad_small), F32)], axis=1).astype(BF16)
    b_f = jnp.concatenate([cols(b_in, o_pool, mix_w), cols(b_in, o_g, n_gate), cols(b_in, o_f, FOX_HEADS),
                           jnp.zeros((pad_small,), F32)])
    tok_cols = ((kv0, 2 * gw), (kv0 + 2 * gw, gw), (kv0 + 4 * gw, gw), (o_fox + hw, hw))
    w_k = jnp.concatenate([cols(w_in, lo, wd) for lo, wd in tok_cols], axis=1).astype(BF16)
    b_k = jnp.concatenate([cols(b_in, lo, wd) for lo, wd in tok_cols])
    feat_cols = ((o_q, nq_w, qs), (kv0 + 3 * gw, gw, 1.0), (kv0 + 5 * gw, gw, 1.0), (o_fox, hw, qs),
                 (o_fox + 2 * hw, hw, 1.0))
    w_t = jnp.concatenate([cols(w_in, lo, wd) * sc for lo, wd, sc in feat_cols], axis=1).T.astype(BF16)
    b_t = jnp.concatenate([cols(b_in, lo, wd) * sc for lo, wd, sc in feat_cols])
    r_vslc, r_vwin, r_qfox, r_vfox = nq_w, nq_w + gw, nq_w + 2 * gw, nq_w + 2 * gw + hw
    c_kslc, c_kwin, c_kfox = 2 * gw, 3 * gw, 4 * gw

    xf = x.reshape(n, d)
    u_pool, z_small = _matmul_bias_split(xf, w_f, b_f, mix_w)
    zk = _matmul_bias(xf, w_k, b_k, BF16)
    gm = _matmul_bias(xf, cols(w_in, o_gm, N_BRANCH * d).astype(BF16), cols(b_in, o_gm, N_BRANCH * d), BF16,
                      act="sigmoid", tn=1024)
    zt = _matmul_bias_t(xf, t, w_t, b_t)

    y_pool = _pool_mixer(u_pool, t, pool_w, pool_b, pool_scale)

    n_chunk = t // CMP_STRIDE
    a_cmp = cols(zk, 0, 2 * gw).reshape(bsz, n_chunk, CMP_STRIDE, 2, g, HEAD_DIM)
    a_cmp = a_cmp.transpose(3, 0, 4, 1, 2, 5).reshape(2, bsz * g, n_chunk, CMP_STRIDE * HEAD_DIM)
    kvc = _compress(a_cmp, cmp_pos.reshape(2, CMP_LEN * HEAD_DIM), cmp_w1, cmp_b1, cmp_w2, cmp_b2)
    kvc = kvc.reshape(2, bsz, g, n_chunk, HEAD_DIM)
    n_s = t // SEL_LEN
    c_start = np.arange(n_chunk) * CMP_STRIDE
    s_start = np.arange(n_s) * SEL_LEN
    ov = (c_start[:, None] < s_start[None, :] + SEL_LEN) & (c_start[:, None] + CMP_LEN > s_start[None, :])
    ov[n_chunk - (CMP_LEN // CMP_STRIDE) + 1:] = False
    ovt = jnp.asarray(ov.T.astype(np.float32), BF16)
    o_cmp, qaug = _nsa_cmp(zt, kvc[0], kvc[1].transpose(0, 1, 3, 2), ovt)
    onehot = jnp.asarray((np.arange(t)[:, None] // SEL_LEN == np.arange(n_s)[None, :]).astype(np.float32), BF16)
    o_slc = _nsa_slc(qaug, onehot, zk, zt, c_kslc // LANES, r_vslc // LANES)
    o_win = _nsa_win(zt, zk, c_kwin // LANES, r_vwin // LANES)

    cum = _fox_prep(z_small, t)
    cq = cols(cum, n_gate, FOX_HEADS)
    cq_t = cq.transpose(0, 3, 1, 2)
    ones_t = jnp.ones((bsz, FOX_HEADS, 3, t), BF16)
    zero6 = jnp.zeros((bsz, FOX_HEADS, 6, t), BF16)
    even = (np.arange(FOX_HEADS) % 2 == 0)[None, :, None, None]
    augq = jnp.concatenate([jnp.where(even, jnp.concatenate([cq_t, ones_t], axis=2), zero6),
                            jnp.where(even, zero6, jnp.concatenate([cq_t, ones_t], axis=2)),
                            jnp.zeros((bsz, FOX_HEADS, 4, t), BF16)], axis=2)
    ck = cq.transpose(0, 3, 2, 1).reshape(bsz, FOX_HEADS // 2, 2, t, 3)
    ones_k = jnp.ones((bsz, FOX_HEADS // 2, t, 3), BF16)
    augk = jnp.concatenate([ones_k, -ck[:, :, 0], ones_k, -ck[:, :, 1],
                            jnp.zeros((bsz, FOX_HEADS // 2, t, LANES - 12), BF16)], axis=-1)
    y_fox = _fox_attention(zt, zk, augq, augk, r_qfox, c_kfox, r_vfox)

    hsel = np.arange(NSA_HEADS * HEAD_DIM) // HEAD_DIM
    expand = np.zeros((LANES, N_BRANCH * mix_w), np.float32)
    for br in range(N_BRANCH):
        expand[hsel * N_BRANCH + br, br * mix_w + np.arange(mix_w)] = 1.0
    return _merge(xf, y_pool, o_cmp, o_slc, o_win, y_fox, z_small, gm,
                  jnp.asarray(expand, BF16), w_up.astype(BF16), w_o.astype(BF16), ln_g, ln_b, alpha)


def _moe_layer(x, router_w, router_b, w1, b1, w2, b2, layer, ln_g, ln_b, alpha):
    n = x.shape[0] // SUBLANES
    nk = n * TOP_K
    idx_p, gate_p = _router(x, router_w, router_b)
    e_flat = idx_p[:, :TOP_K].reshape(nk)
    onehot = (e_flat[:, None] == jnp.arange(N_EXPERTS)[None, :]).astype(jnp.int32)
    incl = jnp.cumsum(onehot, axis=0)
    counts = incl[-1]
    rank = jnp.sum((incl - onehot) * onehot, axis=1)
    padded = (counts + MOE_BLOCK - 1) // MOE_BLOCK * MOE_BLOCK
    pend = jnp.cumsum(padded)
    pstart = pend - padded
    dest = pstart[e_flat] + rank
    n_blk = (nk + MOE_BLOCK - 1) // MOE_BLOCK + N_EXPERTS
    blk_e = jnp.minimum(jnp.sum(jnp.arange(n_blk)[:, None] * MOE_BLOCK >= pend[None, :], axis=1),
                        N_EXPERTS - 1).astype(jnp.int32)
    n_used = (pend[-1] // MOE_BLOCK).astype(jnp.int32).reshape(1)
    dest = dest.astype(jnp.int32)

    xs = _dispatch(x, dest, pend.astype(jnp.int32), n_used, n_blk)
    yb = _expert_ffn(xs, blk_e, n_used, w1, b1, w2, b2, layer)
    return _combine(x, gate_p, dest, yb, ln_g, ln_b, alpha)


def kernel(x, w_in, b_in, pool_w, pool_b, pool_scale, cmp_pos, cmp_w1, cmp_b1, cmp_w2, cmp_b2,
           w_up, w_o, ln1_g, ln1_b, router_w, router_b, moe_w1, moe_b1, moe_w2, moe_b2, ln2_g, ln2_b):
    depth = w_in.shape[0]
    alpha = (2 * depth) ** 0.25
    bsz, t, d = x.shape
    for l in range(depth):
        x1 = _mixer_layer(x, w_in[l], b_in[l], pool_w[l], pool_b[l], pool_scale[l], cmp_pos[l], cmp_w1[l],
                          cmp_b1[l], cmp_w2[l], cmp_b2[l], w_up[l], w_o[l], ln1_g[l], ln1_b[l], alpha)
        x2 = _moe_layer(x1, router_w[l], router_b[l], moe_w1, moe_b1, moe_w2, moe_b2, l,
                        ln2_g[l], ln2_b[l], alpha)
        x = x2.reshape(bsz, t, d)
    return x
```
